```python
import math
import jax, jax.numpy as jnp
from jax import lax
import numpy as np

D_MODEL = 2048
BATCH = 4
SEQ = 2048
DEPTH = 4

HEAD_DIM = 128
N_MEM_HEADS = 4
MEM_WIDTH = N_MEM_HEADS * HEAD_DIM
SELF_WIDTH = D_MODEL - MEM_WIDTH
N_SELF_HEADS = SELF_WIDTH // HEAD_DIM
DIFF_HALF = HEAD_DIM // 2
N_MEM = 256
MOBA_BLOCK = 256
MOBA_TOPK = 3
MOBA_Q_CHUNK = 16
DENSE_Q_BLOCK = 128
N_BUCKETS = 32
MAX_DISTANCE = 128
D_FF = -(-8 * D_MODEL // (3 * 256)) * 256
N_A = DEPTH // 2
N_B = DEPTH - N_A
RMS_EPS = 1e-6
NEG = -1e30

kernel_name = "yoco_moba_diffattn_hybrid"


def rmsnorm(x, g):
    xf = x.astype(jnp.float32)
    y = xf * lax.rsqrt(jnp.mean(xf * xf, axis=-1, keepdims=True) + RMS_EPS)
    return (y * g.astype(jnp.float32)).astype(x.dtype)


def rel_bucket(rel):
    n = jnp.maximum(rel, 0)
    max_exact = N_BUCKETS // 2
    nf = jnp.maximum(n, 1).astype(jnp.float32)
    large = max_exact + (jnp.log(nf / max_exact) / math.log(MAX_DISTANCE / max_exact)
                         * (N_BUCKETS - max_exact)).astype(jnp.int32)
    large = jnp.minimum(large, N_BUCKETS - 1)
    return jnp.where(n < max_exact, n, large)


def split_heads(t, n):
    B, S, _ = t.shape
    return t.reshape(B, S, n, -1).transpose(0, 2, 1, 3)


def merge_heads(t):
    B, n, S, Dh = t.shape
    return t.transpose(0, 2, 1, 3).reshape(B, S, n * Dh)


def moba_attention(q, k, v, rel_bias):
    B, H, S, Dh = q.shape
    nb = -(-S // MOBA_BLOCK)
    S_pad = nb * MOBA_BLOCK
    pad = [(0, 0), (0, 0), (0, S_pad - S), (0, 0)]
    q, k, v = jnp.pad(q, pad), jnp.pad(k, pad), jnp.pad(v, pad)
    kb = k.reshape(B, H, nb, MOBA_BLOCK, Dh)
    vb = v.reshape(B, H, nb, MOBA_BLOCK, Dh)
    k_mean = jnp.mean(kb.astype(jnp.float32), axis=3)
    gate = jnp.einsum('bhsd,bhnd->bhsn', q.astype(jnp.float32), k_mean)
    q_blk = jnp.arange(S_pad, dtype=jnp.int32) // MOBA_BLOCK
    past = jnp.arange(nb, dtype=jnp.int32)[None, :] < q_blk[:, None]
    gate = jnp.where(past, gate, NEG)
    n_sel = max(1, min(MOBA_TOPK, nb - 1))
    _, sel = lax.top_k(gate, n_sel)
    own = jnp.broadcast_to(q_blk[None, None, :, None], (B, H, S_pad, 1)).astype(sel.dtype)
    blocks = jnp.concatenate([sel, own], axis=-1)
    nk = n_sel + 1
    n_chunk = S_pad // MOBA_Q_CHUNK

    def to_chunks(t):
        t = t.reshape(B, H, n_chunk, MOBA_Q_CHUNK, *t.shape[3:])
        return jnp.moveaxis(t, 2, 0)

    pos = jnp.arange(S_pad, dtype=jnp.int32).reshape(n_chunk, MOBA_Q_CHUNK)
    bi = jnp.arange(B)[:, None, None, None]
    hi = jnp.arange(H)[None, :, None, None]
    bias_t = rel_bias.T.astype(jnp.float32)
    own_slot = (jnp.arange(nk) == nk - 1)[:, None]
    scale = Dh ** -0.5
    offs = jnp.arange(MOBA_BLOCK, dtype=jnp.int32)

    def chunk(args):
        qc, blk, qpos = args
        ks = kb[bi, hi, blk]
        vs = vb[bi, hi, blk]
        kpos = blk[..., None].astype(jnp.int32) * MOBA_BLOCK + offs
        s = jnp.einsum('bhcd,bhcnld->bhcnl', qc, ks).astype(jnp.float32) * scale
        rel = qpos[:, None, None] - kpos
        s = s + bias_t[hi[..., None], rel_bucket(rel)]
        qb = qpos // MOBA_BLOCK
        past_ok = (blk < qb[:, None])[..., None]
        causal_ok = kpos <= qpos[:, None, None]
        valid = jnp.where(own_slot, causal_ok, past_ok)
        s = jnp.where(valid, s, NEG)
        Cq = qc.shape[2]
        p = jax.nn.softmax(s.reshape(B, H, Cq, nk * MOBA_BLOCK), axis=-1)
        p = p.reshape(B, H, Cq, nk, MOBA_BLOCK).astype(vs.dtype)
        return jnp.einsum('bhcnl,bhcnld->bhcd', p, vs)

    out = lax.map(chunk, (to_chunks(q), to_chunks(blocks), pos))
    out = jnp.moveaxis(out, 0, 2).reshape(B, H, S_pad, Dh)
    return out[:, :, :S]


def diff_attention(q1, q2, k1, k2, v, lam, rel_bias):
    B, H, S, d = q1.shape
    nqb = S // DENSE_Q_BLOCK
    kpos = jnp.arange(S, dtype=jnp.int32)
    scale = d ** -0.5
    table = rel_bias.astype(jnp.float32)

    def to_chunks(t):
        t = t.reshape(B, H, nqb, DENSE_Q_BLOCK, t.shape[-1])
        return jnp.moveaxis(t, 2, 0)

    qpos_all = jnp.arange(S, dtype=jnp.int32).reshape(nqb, DENSE_Q_BLOCK)

    def blk(args):
        q1c, q2c, qpos = args
        rel = qpos[:, None] - kpos[None, :]
        bias = jnp.moveaxis(table[rel_bucket(rel)], -1, 0)
        causal = rel >= 0

        def probs(qc, kc):
            s = jnp.einsum('bhqd,bhkd->bhqk', qc, kc).astype(jnp.float32) * scale + bias
            return jax.nn.softmax(jnp.where(causal, s, NEG), axis=-1)

        a = probs(q1c, k1) - lam * probs(q2c, k2)
        return jnp.einsum('bhqk,bhkd->bhqd', a.astype(v.dtype), v)

    out = lax.map(blk, (to_chunks(q1), to_chunks(q2), qpos_all))
    return jnp.moveaxis(out, 0, 2).reshape(B, H, S, v.shape[-1])


def mem_attention(q, mk, mv):
    s = jnp.einsum('bhsd,bhmd->bhsm', q, mk).astype(jnp.float32) * (q.shape[-1] ** -0.5)
    p = jax.nn.softmax(s, axis=-1).astype(mv.dtype)
    return jnp.einsum('bhsm,bhmd->bhsd', p, mv)


def setup_inputs(seed: int = 0) -> dict:
    key = jax.random.key(seed)
    ks = jax.random.split(key, 20)
    f32 = jnp.float32

    def w(k, shape, fan_in):
        return jax.random.normal(k, shape, f32) * (fan_in ** -0.5)

    def gain(k, shape):
        return 1.0 + 0.02 * jax.random.normal(k, shape, f32)

    return {
        "x": jax.random.normal(ks[0], (BATCH, SEQ, D_MODEL), f32),
        "mem": jax.random.normal(ks[1], (BATCH, N_MEM, D_MODEL), f32),
        "rel_bias": 0.5 * jax.random.normal(ks[2], (N_BUCKETS, N_SELF_HEADS), f32),
        "g_mix": gain(ks[3], (DEPTH, D_MODEL)),
        "w_in_a": w(ks[4], (N_A, D_MODEL, 3 * SELF_WIDTH + MEM_WIDTH), D_MODEL),
        "w_in_b": w(ks[5], (N_B, D_MODEL, SELF_WIDTH + MEM_WIDTH), D_MODEL),
        "g_mem": gain(ks[6], (DEPTH, D_MODEL)),
        "w_mem_kv": w(ks[7], (DEPTH, D_MODEL, 2 * MEM_WIDTH), D_MODEL),
        "w_o": w(ks[8], (DEPTH, D_MODEL, D_MODEL), D_MODEL),
        "g_ffn": gain(ks[9], (DEPTH, D_MODEL)),
        "w_gate_up": w(ks[10], (DEPTH, D_MODEL, 2 * D_FF), D_MODEL),
        "w_down": w(ks[11], (DEPTH, D_FF, D_MODEL), D_FF),
        "g_kv": gain(ks[12], (D_MODEL,)),
        "w_kv_shared": w(ks[13], (D_MODEL, 2 * SELF_WIDTH), D_MODEL),
        "lambda_qk": 0.1 * jax.random.normal(ks[14], (N_B, 4, DIFF_HALF), f32),
        "g_subln": gain(ks[15], (N_B, HEAD_DIM)),
        "g_final": gain(ks[16], (D_MODEL,)),
    }


def reference(x, mem, rel_bias, g_mix, w_in_a, w_in_b, g_mem, w_mem_kv, w_o, g_ffn,
              w_gate_up, w_down, g_kv, w_kv_shared, lambda_qk, g_subln, g_final):
    h = x
    k1 = k2 = v_sh = None
    for l in range(DEPTH):
        xn = rmsnorm(h, g_mix[l])
        mkv = rmsnorm(mem, g_mem[l]) @ w_mem_kv[l]
        mk = split_heads(mkv[..., :MEM_WIDTH], N_MEM_HEADS)
        mv = split_heads(mkv[..., MEM_WIDTH:], N_MEM_HEADS)
        if l < N_A:
            proj = xn @ w_in_a[l]
            q = split_heads(proj[..., :SELF_WIDTH], N_SELF_HEADS)
            k = split_heads(proj[..., SELF_WIDTH:2 * SELF_WIDTH], N_SELF_HEADS)
            v = split_heads(proj[..., 2 * SELF_WIDTH:3 * SELF_WIDTH], N_SELF_HEADS)
            qm = proj[..., 3 * SELF_WIDTH:]
            y_self = moba_attention(q, k, v, rel_bias)
        else:
            if l == N_A:
                kv = rmsnorm(h, g_kv) @ w_kv_shared
                kh = split_heads(kv[..., :SELF_WIDTH], N_SELF_HEADS)
                k1, k2 = kh[..., :DIFF_HALF], kh[..., DIFF_HALF:]
                v_sh = split_heads(kv[..., SELF_WIDTH:], N_SELF_HEADS)
            j = l - N_A
            proj = xn @ w_in_b[j]
            qh = split_heads(proj[..., :SELF_WIDTH], N_SELF_HEADS)
            q1, q2 = qh[..., :DIFF_HALF], qh[..., DIFF_HALF:]
            qm = proj[..., SELF_WIDTH:]
            lam_init = 0.8 - 0.6 * math.exp(-0.3 * l)
            lq = lambda_qk[j].astype(jnp.float32)
            lam = jnp.exp(jnp.sum(lq[0] * lq[1])) - jnp.exp(jnp.sum(lq[2] * lq[3])) + lam_init
            o = diff_attention(q1, q2, k1, k2, v_sh, lam, rel_bias)
            y_self = (rmsnorm(o, g_subln[j]) * (1.0 - lam_init)).astype(o.dtype)
        y_mem = mem_attention(split_heads(qm, N_MEM_HEADS), mk, mv)
        y = jnp.concatenate([merge_heads(y_self), merge_heads(y_mem)], axis=-1) @ w_o[l]
        h = h + y
        gu = rmsnorm(h, g_ffn[l]) @ w_gate_up[l]
        h = h + (jax.nn.silu(gu[..., :D_FF]) * gu[..., D_FF:]) @ w_down[l]
    return rmsnorm(h, g_final)
```

```python
import functools
import math

import jax
import jax.numpy as jnp
import numpy as np
from jax import lax
from jax.experimental import pallas as pl
from jax.experimental.pallas import tpu as pltpu

D_MODEL = 2048
HEAD_DIM = 128
N_MEM_HEADS = 4
MEM_WIDTH = N_MEM_HEADS * HEAD_DIM
SELF_WIDTH = D_MODEL - MEM_WIDTH
N_SELF_HEADS = SELF_WIDTH // HEAD_DIM
DIFF_HALF = HEAD_DIM // 2
MOBA_BLOCK = 256
MOBA_TOPK = 3
N_BUCKETS = 32
MAX_DISTANCE = 128
RMS_EPS = 1e-6
NEG = -1e30

BLK = MOBA_BLOCK
V7X_VMEM_BYTES = 64 * 1024 * 1024
VMEM_LIMIT = 48 * 1024 * 1024

F32 = jnp.float32
BF16 = jnp.bfloat16
NT_DIMS = (((1,), (1,)), ((), ()))


def _params(n_grid_axes):
    return pltpu.CompilerParams(
        dimension_semantics=("arbitrary",) * n_grid_axes, vmem_limit_bytes=VMEM_LIMIT)


def _rmsnorm_rows(x, g):
    ms = jnp.mean(x * x, axis=-1, keepdims=True)
    return x * lax.rsqrt(ms + RMS_EPS) * g


def _norm_matmul_kernel(x_ref, g_ref, w_ref, cs_ref, o_ref, xn_ref):
    @pl.when(pl.program_id(1) == 0)
    def _():
        xn_ref[...] = _rmsnorm_rows(x_ref[...], g_ref[...]).astype(BF16)

    acc = jnp.dot(xn_ref[...], w_ref[...], preferred_element_type=F32)
    o_ref[...] = (acc * cs_ref[...]).astype(o_ref.dtype)


def norm_matmul(x, g, w, layer, col_scale, *, tm, tn):
    m, d = x.shape
    n = w.shape[-1]
    return pl.pallas_call(
        _norm_matmul_kernel,
        grid=(m // tm, n // tn),
        in_specs=[
            pl.BlockSpec((tm, d), lambda i, j: (i, 0)),
            pl.BlockSpec((1, d), lambda i, j: (0, 0)),
            pl.BlockSpec((None, d, tn), lambda i, j: (layer, 0, j)),
            pl.BlockSpec((1, tn), lambda i, j: (0, j)),
        ],
        out_specs=pl.BlockSpec((tm, tn), lambda i, j: (i, j)),
        out_shape=jax.ShapeDtypeStruct((m, n), BF16),
        scratch_shapes=[pltpu.VMEM((tm, d), BF16)],
        compiler_params=_params(2),
    )(x, g.reshape(1, d), w, col_scale.reshape(1, n))


def _ffn_up_kernel(x_ref, g_ref, wg_ref, wu_ref, o_ref, xn_ref):
    @pl.when(pl.program_id(1) == 0)
    def _():
        xn_ref[...] = _rmsnorm_rows(x_ref[...], g_ref[...]).astype(BF16)

    xn = xn_ref[...]
    gate = jnp.dot(xn, wg_ref[...], preferred_element_type=F32)
    up = jnp.dot(xn, wu_ref[...], preferred_element_type=F32)
    o_ref[...] = (gate * jax.nn.sigmoid(gate) * up).astype(o_ref.dtype)


def ffn_up(x, g, w_gate_up, layer, *, tm, tn):
    m, d = x.shape
    d_ff = w_gate_up.shape[-1] // 2
    n_tiles = d_ff // tn
    return pl.pallas_call(
        _ffn_up_kernel,
        grid=(m // tm, n_tiles),
        in_specs=[
            pl.BlockSpec((tm, d), lambda i, j: (i, 0)),
            pl.BlockSpec((1, d), lambda i, j: (0, 0)),
            pl.BlockSpec((None, d, tn), lambda i, j: (layer, 0, j)),
            pl.BlockSpec((None, d, tn), lambda i, j: (layer, 0, j + n_tiles)),
        ],
        out_specs=pl.BlockSpec((tm, tn), lambda i, j: (i, j)),
        out_shape=jax.ShapeDtypeStruct((m, d_ff), BF16),
        scratch_shapes=[pltpu.VMEM((tm, d), BF16)],
        compiler_params=_params(2),
    )(x, g.reshape(1, d), w_gate_up, w_gate_up)


def _residual_matmul_kernel(h_ref, a_ref, w_ref, o_ref):
    o_ref[...] = h_ref[...] + jnp.dot(a_ref[...], w_ref[...], preferred_element_type=F32)


def residual_matmul(h, a, w, layer, *, tm, tn):
    m, n = h.shape
    k = a.shape[1]
    return pl.pallas_call(
        _residual_matmul_kernel,
        grid=(m // tm, n // tn),
        in_specs=[
            pl.BlockSpec((tm, tn), lambda i, j: (i, j)),
            pl.BlockSpec((tm, k), lambda i, j: (i, 0)),
            pl.BlockSpec((None, k, tn), lambda i, j: (layer, 0, j)),
        ],
        out_specs=pl.BlockSpec((tm, tn), lambda i, j: (i, j)),
        out_shape=jax.ShapeDtypeStruct((m, n), F32),
        compiler_params=_params(2),
    )(h, a, w)


def _residual_matmul2_kernel(h_ref, a1_ref, a2_ref, w1_ref, w2_ref, o_ref):
    acc = jnp.dot(a1_ref[...], w1_ref[...], preferred_element_type=F32)
    acc = acc + jnp.dot(a2_ref[...], w2_ref[...], preferred_element_type=F32)
    o_ref[...] = h_ref[...] + acc


def residual_matmul2(h, a1, a2, w, layer, *, tm, tn):
    m, n = h.shape
    k1, k2 = a1.shape[1], a2.shape[1]
    assert k1 % k2 == 0
    return pl.pallas_call(
        _residual_matmul2_kernel,
        grid=(m // tm, n // tn),
        in_specs=[
            pl.BlockSpec((tm, tn), lambda i, j: (i, j)),
            pl.BlockSpec((tm, k1), lambda i, j: (i, 0)),
            pl.BlockSpec((tm, k2), lambda i, j: (i, 0)),
            pl.BlockSpec((None, k1, tn), lambda i, j: (layer, 0, j)),
            pl.BlockSpec((None, k2, tn), lambda i, j: (layer, k1 // k2, j)),
        ],
        out_specs=pl.BlockSpec((tm, tn), lambda i, j: (i, j)),
        out_shape=jax.ShapeDtypeStruct((m, n), F32),
        compiler_params=_params(2),
    )(h, a1, a2, w, w)


def _final_norm_kernel(x_ref, g_ref, o_ref):
    o_ref[...] = _rmsnorm_rows(x_ref[...], g_ref[...])


def final_norm(x, g, *, tm):
    m, d = x.shape
    return pl.pallas_call(
        _final_norm_kernel,
        grid=(m // tm,),
        in_specs=[pl.BlockSpec((tm, d), lambda i: (i, 0)), pl.BlockSpec((1, d), lambda i: (0, 0))],
        out_specs=pl.BlockSpec((tm, d), lambda i: (i, 0)),
        out_shape=jax.ShapeDtypeStruct((m, d), F32),
        compiler_params=_params(1),
    )(x, g.reshape(1, d))


def _rel_bucket_np(rel):
    n = np.maximum(rel, 0)
    max_exact = N_BUCKETS // 2
    nf = np.maximum(n, 1).astype(np.float32)
    large = max_exact + (np.log(nf / np.float32(max_exact)) / np.float32(math.log(MAX_DISTANCE / max_exact))
                         * np.float32(N_BUCKETS - max_exact)).astype(np.int32)
    large = np.minimum(large, N_BUCKETS - 1)
    return np.where(n < max_exact, n, large).astype(np.int32)


def _bucket_tiles():
    qi = np.arange(BLK, dtype=np.int32)[:, None]
    ki = np.arange(BLK, dtype=np.int32)[None, :]
    diag = _rel_bucket_np(qi - ki)
    prev = _rel_bucket_np(qi - ki + BLK)
    assert _rel_bucket_np(np.array([BLK + 1]))[0] == N_BUCKETS - 1
    return jnp.asarray(diag), jnp.asarray(prev)


def _bias_tile(bkt, tab_ref, head):
    out = jnp.zeros(bkt.shape, F32)
    for b in range(N_BUCKETS):
        out = jnp.where(bkt == b, tab_ref[head, b], out)
    return out


def _causal_tile(tile):
    qi = lax.broadcasted_iota(jnp.int32, tile.shape, 0)
    ki = lax.broadcasted_iota(jnp.int32, tile.shape, 1)
    return jnp.where(ki <= qi, tile, NEG)


def _softmax_first(s, v, m_ref, l_ref, acc_ref):
    m = jnp.max(s, axis=-1, keepdims=True)
    p = jnp.exp(s - m)
    m_ref[...] = m
    l_ref[...] = jnp.sum(p, axis=-1, keepdims=True)
    acc_ref[...] = jnp.dot(p.astype(BF16), v, preferred_element_type=F32)


def _softmax_update(s, v, m_ref, l_ref, acc_ref):
    m_old = m_ref[...]
    m_new = jnp.maximum(m_old, jnp.max(s, axis=-1, keepdims=True))
    alpha = jnp.exp(m_old - m_new)
    p = jnp.exp(s - m_new)
    m_ref[...] = m_new
    l_ref[...] = alpha * l_ref[...] + jnp.sum(p, axis=-1, keepdims=True)
    acc_ref[...] = alpha * acc_ref[...] + jnp.dot(p.astype(BF16), v, preferred_element_type=F32)


def _moba_kernel(tab_ref, bd_ref, bp_ref, q_ref, k_ref, v_ref, o_ref,
                 bias_d_ref, bias_p_ref, kmean_ref, m_ref, l_ref, acc_ref, *, gate_fill):
    h, b, qb = pl.program_id(0), pl.program_id(1), pl.program_id(2)
    nb = k_ref.shape[0] // BLK

    @pl.when((b == 0) & (qb == 0))
    def _():
        bias_d_ref[...] = _causal_tile(_bias_tile(bd_ref[...], tab_ref, h))
        bias_p_ref[...] = _bias_tile(bp_ref[...], tab_ref, h)

    @pl.when(qb == 0)
    def _():
        kf = k_ref[...].astype(F32).reshape(nb, BLK, HEAD_DIM)
        kmean_ref[...] = jnp.mean(kf, axis=1)

    q = q_ref[...]

    gate = lax.dot_general(kmean_ref[...], q.astype(F32), NT_DIMS,
                           precision=lax.Precision.HIGHEST, preferred_element_type=F32)
    row = lax.broadcasted_iota(jnp.int32, gate.shape, 0)
    past = row < qb
    gate = jnp.where(past, gate, gate_fill)
    beaten_by = jnp.zeros(gate.shape, F32)
    for i in range(nb):
        gi = gate[i:i + 1, :]
        wins = (gi > gate) | ((gi == gate) & (i < row))
        beaten_by = beaten_by + jnp.where(wins, 1.0, 0.0)
    n_sel = max(1, min(MOBA_TOPK, nb - 1))
    keep = (beaten_by < n_sel) & past
    sel_neg = jnp.where(keep, 0.0, NEG)
    sel_neg = jnp.concatenate([sel_neg, jnp.zeros((HEAD_DIM - nb, BLK), F32)], axis=0)
    q_aug = jnp.concatenate([q, sel_neg.T.astype(BF16)], axis=1)
    lane = lax.broadcasted_iota(jnp.int32, (BLK, HEAD_DIM), 1)

    def block_scores(j):
        start = pl.multiple_of(j * BLK, BLK)
        kj = k_ref[pl.ds(start, BLK), :]
        onehot = jnp.where(lane == j, 1.0, 0.0).astype(BF16)
        k_aug = jnp.concatenate([kj, onehot], axis=1)
        s = lax.dot_general(q_aug, k_aug, NT_DIMS, preferred_element_type=F32)
        return s, v_ref[pl.ds(start, BLK), :]

    start = pl.multiple_of(qb * BLK, BLK)
    s = lax.dot_general(q, k_ref[pl.ds(start, BLK), :], NT_DIMS, preferred_element_type=F32)
    _softmax_first(s + bias_d_ref[...], v_ref[pl.ds(start, BLK), :], m_ref, l_ref, acc_ref)

    @pl.when(qb >= 1)
    def _():
        s, v = block_scores(qb - 1)
        _softmax_update(s + bias_p_ref[...], v, m_ref, l_ref, acc_ref)

    far_bias = tab_ref[h, N_BUCKETS - 1]

    def far_body(d, carry):
        s, v = block_scores(qb - d)
        _softmax_update(s + far_bias, v, m_ref, l_ref, acc_ref)
        return carry

    lax.fori_loop(2, qb + 1, far_body, 0)

    o_ref[...] = (acc_ref[...] / l_ref[...]).astype(o_ref.dtype)


def moba_attention(proj, tab, bkt_diag, bkt_prev, *, batch, seq, gate_fill):
    nqb = seq // BLK
    k_off = SELF_WIDTH // HEAD_DIM
    kern = functools.partial(_moba_kernel, gate_fill=gate_fill)
    return pl.pallas_call(
        kern,
        grid=(N_SELF_HEADS, batch, nqb),
        in_specs=[
            pl.BlockSpec(memory_space=pltpu.SMEM),
            pl.BlockSpec((BLK, BLK), lambda h, b, qb: (0, 0)),
            pl.BlockSpec((BLK, BLK), lambda h, b, qb: (0, 0)),
            pl.BlockSpec((BLK, HEAD_DIM), lambda h, b, qb: (b * nqb + qb, h)),
            pl.BlockSpec((seq, HEAD_DIM), lambda h, b, qb: (b, k_off + h)),
            pl.BlockSpec((seq, HEAD_DIM), lambda h, b, qb: (b, 2 * k_off + h)),
        ],
        out_specs=pl.BlockSpec((BLK, HEAD_DIM), lambda h, b, qb: (b * nqb + qb, h)),
        out_shape=jax.ShapeDtypeStruct((batch * seq, SELF_WIDTH), BF16),
        scratch_shapes=[
            pltpu.VMEM((BLK, BLK), F32), pltpu.VMEM((BLK, BLK), F32),
            pltpu.VMEM((seq // BLK, HEAD_DIM), F32),
            pltpu.VMEM((BLK, 1), F32), pltpu.VMEM((BLK, 1), F32), pltpu.VMEM((BLK, HEAD_DIM), F32),
        ],
        compiler_params=_params(3),
    )(tab, bkt_diag, bkt_prev, proj, proj, proj)


def _diff_kernel(tab_ref, bd_ref, bp_ref, lq_ref, gs_ref, q_ref, k_ref, v_ref, o_ref,
                 bias_d_ref, bias_p_ref, m_ref, l_ref, acc_ref, *, lam_init):
    h, b, qb = pl.program_id(0), pl.program_id(1), pl.program_id(2)

    @pl.when((b == 0) & (qb == 0))
    def _():
        bd = _causal_tile(_bias_tile(bd_ref[...], tab_ref, h))
        bp = _bias_tile(bp_ref[...], tab_ref, h)
        bias_d_ref[...] = jnp.concatenate([bd, bd], axis=0)
        bias_p_ref[...] = jnp.concatenate([bp, bp], axis=0)

    q = q_ref[...]
    lane = lax.broadcasted_iota(jnp.int32, q.shape, 1)
    zero = jnp.zeros_like(q)
    q2x = jnp.concatenate([jnp.where(lane < DIFF_HALF, q, zero),
                           jnp.where(lane >= DIFF_HALF, q, zero)], axis=0)

    def block_scores(j):
        start = pl.multiple_of(j * BLK, BLK)
        s = lax.dot_general(q2x, k_ref[pl.ds(start, BLK), :], NT_DIMS, preferred_element_type=F32)
        return s, v_ref[pl.ds(start, BLK), :]

    s, v = block_scores(qb)
    _softmax_first(s + bias_d_ref[...], v, m_ref, l_ref, acc_ref)

    @pl.when(qb >= 1)
    def _():
        s, v = block_scores(qb - 1)
        _softmax_update(s + bias_p_ref[...], v, m_ref, l_ref, acc_ref)

    far_bias = tab_ref[h, N_BUCKETS - 1]

    def far_body(d, carry):
        s, v = block_scores(qb - d)
        _softmax_update(s + far_bias, v, m_ref, l_ref, acc_ref)
        return carry

    lax.fori_loop(2, qb + 1, far_body, 0)

    lq = lq_ref[...]
    lam = (jnp.exp(jnp.sum(lq[0:1] * lq[1:2], axis=-1, keepdims=True))
           - jnp.exp(jnp.sum(lq[2:3] * lq[3:4], axis=-1, keepdims=True)) + lam_init)
    o_all = acc_ref[...] / l_ref[...]
    o = o_all[:BLK] - lam * o_all[BLK:]
    o_ref[...] = (_rmsnorm_rows(o, gs_ref[...]) * (1.0 - lam_init)).astype(o_ref.dtype)


def diff_attention(proj, kv, tab, bkt_diag, bkt_prev, lq, g_subln, *, batch, seq, lam_init):
    nqb = seq // BLK
    v_off = SELF_WIDTH // HEAD_DIM
    kern = functools.partial(_diff_kernel, lam_init=lam_init)
    return pl.pallas_call(
        kern,
        grid=(N_SELF_HEADS, batch, nqb),
        in_specs=[
            pl.BlockSpec(memory_space=pltpu.SMEM),
            pl.BlockSpec((BLK, BLK), lambda h, b, qb: (0, 0)),
            pl.BlockSpec((BLK, BLK), lambda h, b, qb: (0, 0)),
            pl.BlockSpec((4, DIFF_HALF), lambda h, b, qb: (0, 0)),
            pl.BlockSpec((1, HEAD_DIM), lambda h, b, qb: (0, 0)),
            pl.BlockSpec((BLK, HEAD_DIM), lambda h, b, qb: (b * nqb + qb, h)),
            pl.BlockSpec((seq, HEAD_DIM), lambda h, b, qb: (b, h)),
            pl.BlockSpec((seq, HEAD_DIM), lambda h, b, qb: (b, v_off + h)),
        ],
        out_specs=pl.BlockSpec((BLK, HEAD_DIM), lambda h, b, qb: (b * nqb + qb, h)),
        out_shape=jax.ShapeDtypeStruct((batch * seq, SELF_WIDTH), BF16),
        scratch_shapes=[
            pltpu.VMEM((2 * BLK, BLK), F32), pltpu.VMEM((2 * BLK, BLK), F32),
            pltpu.VMEM((2 * BLK, 1), F32), pltpu.VMEM((2 * BLK, 1), F32),
            pltpu.VMEM((2 * BLK, HEAD_DIM), F32),
        ],
        compiler_params=_params(3),
    )(tab, bkt_diag, bkt_prev, lq, g_subln.reshape(1, HEAD_DIM), proj, kv, kv)


def _mem_kernel(q_ref, mk_ref, mv_ref, o_ref):
    s = lax.dot_general(q_ref[...], mk_ref[...], NT_DIMS, preferred_element_type=F32)
    m = jnp.max(s, axis=-1, keepdims=True)
    p = jnp.exp(s - m)
    l = jnp.sum(p, axis=-1, keepdims=True)
    o = jnp.dot(p.astype(BF16), mv_ref[...], preferred_element_type=F32)
    o_ref[...] = (o / l).astype(o_ref.dtype)


def mem_attention(proj, mkv, *, batch, seq, n_mem, q_col_block, tq):
    nsb = seq // tq
    return pl.pallas_call(
        _mem_kernel,
        grid=(N_MEM_HEADS, batch, nsb),
        in_specs=[
            pl.BlockSpec((tq, HEAD_DIM), lambda h, b, sb: (b * nsb + sb, q_col_block + h)),
            pl.BlockSpec((n_mem, HEAD_DIM), lambda h, b, sb: (b, h)),
            pl.BlockSpec((n_mem, HEAD_DIM), lambda h, b, sb: (b, N_MEM_HEADS + h)),
        ],
        out_specs=pl.BlockSpec((tq, HEAD_DIM), lambda h, b, sb: (b * nsb + sb, h)),
        out_shape=jax.ShapeDtypeStruct((batch * seq, MEM_WIDTH), BF16),
        compiler_params=_params(3),
    )(proj, mkv, mkv)


def kernel(x, mem, rel_bias, g_mix, w_in_a, w_in_b, g_mem, w_mem_kv, w_o, g_ffn, w_gate_up, w_down,
           g_kv, w_kv_shared, lambda_qk, g_subln, g_final):
    batch, seq, d = x.shape
    n_mem = mem.shape[1]
    depth = g_mix.shape[0]
    n_a = w_in_a.shape[0]
    tokens = batch * seq

    w_in_a, w_in_b, w_mem_kv, w_o, w_gate_up, w_down = (
        w.astype(BF16) for w in (w_in_a, w_in_b, w_mem_kv, w_o, w_gate_up, w_down))
    w_kv_shared = w_kv_shared.astype(BF16)[None]

    tab = rel_bias.T.astype(F32)
    bkt_diag, bkt_prev = _bucket_tiles()

    moba_scale = HEAD_DIM ** -0.5
    diff_scale = DIFF_HALF ** -0.5
    mem_scale = HEAD_DIM ** -0.5
    ones = functools.partial(jnp.ones, dtype=F32)
    cs_a = jnp.concatenate([jnp.full((SELF_WIDTH,), moba_scale, F32), ones((2 * SELF_WIDTH,)),
                            jnp.full((MEM_WIDTH,), mem_scale, F32)])
    cs_b = jnp.concatenate([jnp.full((SELF_WIDTH,), diff_scale, F32),
                            jnp.full((MEM_WIDTH,), mem_scale, F32)])

    h = x.reshape(tokens, d)
    mem2 = mem.reshape(batch * n_mem, d)
    kv = None
    for l in range(depth):
        mkv = norm_matmul(mem2, g_mem[l], w_mem_kv, l, ones((2 * MEM_WIDTH,)),
                          tm=batch * n_mem, tn=512)
        if l < n_a:
            proj = norm_matmul(h, g_mix[l], w_in_a, l, cs_a, tm=1024, tn=512)
            y_self = moba_attention(proj, tab, bkt_diag, bkt_prev, batch=batch, seq=seq,
                                    gate_fill=NEG * moba_scale)
            q_col_block = 3 * SELF_WIDTH // HEAD_DIM
        else:
            j = l - n_a
            if kv is None:
                kv = norm_matmul(h, g_kv, w_kv_shared, 0, ones((2 * SELF_WIDTH,)), tm=1024, tn=512)
            proj = norm_matmul(h, g_mix[l], w_in_b, j, cs_b, tm=1024, tn=512)
            lam_init = 0.8 - 0.6 * math.exp(-0.3 * l)
            y_self = diff_attention(proj, kv, tab, bkt_diag, bkt_prev, lambda_qk[j], g_subln[j],
                                    batch=batch, seq=seq, lam_init=lam_init)
            q_col_block = SELF_WIDTH // HEAD_DIM
        y_mem = mem_attention(proj, mkv, batch=batch, seq=seq, n_mem=n_mem,
                              q_col_block=q_col_block, tq=512)
        h = residual_matmul2(h, y_self, y_mem, w_o, l, tm=1024, tn=512)
        act = ffn_up(h, g_ffn[l], w_gate_up, l, tm=1024, tn=512)
        h = residual_matmul(h, act, w_down, l, tm=512, tn=512)
    return final_norm(h, g_final, tm=512).reshape(batch, seq, d)
```

```python
import functools
import math

import jax
import jax.numpy as jnp
import numpy as np
from jax import lax
from jax.experimental import pallas as pl
from jax.experimental.pallas import tpu as pltpu

D_MODEL = 2048
HEAD_DIM = 128
N_MEM_HEADS = 4
MEM_WIDTH = N_MEM_HEADS * HEAD_DIM
SELF_WIDTH = D_MODEL - MEM_WIDTH
N_SELF_HEADS = SELF_WIDTH // HEAD_DIM
DIFF_HALF = HEAD_DIM // 2
MOBA_BLOCK = 256
MOBA_TOPK = 3
N_BUCKETS = 32
MAX_DISTANCE = 128
RMS_EPS = 1e-6
NEG = -1e30
LOG2E = math.log2(math.e)

BLK = MOBA_BLOCK
V7X_VMEM_BYTES = 64 * 1024 * 1024
VMEM_LIMIT = 48 * 1024 * 1024

F32 = jnp.float32
BF16 = jnp.bfloat16
NT_DIMS = (((1,), (1,)), ((), ()))


def _params(n_grid_axes):
    return pltpu.CompilerParams(
        dimension_semantics=("arbitrary",) * n_grid_axes, vmem_limit_bytes=VMEM_LIMIT)


def _rmsnorm_rows(x, g):
    ms = jnp.mean(x * x, axis=-1, keepdims=True)
    return x * lax.rsqrt(ms + RMS_EPS) * g


def _norm_matmul_kernel(x_ref, g_ref, w_ref, cs_ref, o_ref, xn_ref):
    @pl.when(pl.program_id(1) == 0)
    def _():
        xn_ref[...] = _rmsnorm_rows(x_ref[...], g_ref[...]).astype(BF16)

    acc = jnp.dot(xn_ref[...], w_ref[...], preferred_element_type=F32)
    o_ref[...] = (acc * cs_ref[...]).astype(o_ref.dtype)


def norm_matmul(x, g, w, layer, col_scale, *, tm, tn):
    m, d = x.shape
    n = w.shape[-1]
    return pl.pallas_call(
        _norm_matmul_kernel,
        grid=(m // tm, n // tn),
        in_specs=[
            pl.BlockSpec((tm, d), lambda i, j: (i, 0)),
            pl.BlockSpec((1, d), lambda i, j: (0, 0)),
            pl.BlockSpec((None, d, tn), lambda i, j: (layer, 0, j)),
            pl.BlockSpec((1, tn), lambda i, j: (0, j)),
        ],
        out_specs=pl.BlockSpec((tm, tn), lambda i, j: (i, j)),
        out_shape=jax.ShapeDtypeStruct((m, n), BF16),
        scratch_shapes=[pltpu.VMEM((tm, d), BF16)],
        compiler_params=_params(2),
        name="norm_matmul",
    )(x, g.reshape(1, d), w, col_scale.reshape(1, n))


def _ffn_up_kernel(x_ref, g_ref, wg_ref, wu_ref, o_ref, xn_ref):
    @pl.when(pl.program_id(1) == 0)
    def _():
        xn_ref[...] = _rmsnorm_rows(x_ref[...], g_ref[...]).astype(BF16)

    xn = xn_ref[...]
    gate = jnp.dot(xn, wg_ref[...], preferred_element_type=F32)
    up = jnp.dot(xn, wu_ref[...], preferred_element_type=F32)
    o_ref[...] = (gate * jax.nn.sigmoid(gate) * up).astype(o_ref.dtype)


def ffn_up(x, g, w_gate_up, layer, *, tm, tn):
    m, d = x.shape
    d_ff = w_gate_up.shape[-1] // 2
    n_tiles = d_ff // tn
    return pl.pallas_call(
        _ffn_up_kernel,
        grid=(m // tm, n_tiles),
        in_specs=[
            pl.BlockSpec((tm, d), lambda i, j: (i, 0)),
            pl.BlockSpec((1, d), lambda i, j: (0, 0)),
            pl.BlockSpec((None, d, tn), lambda i, j: (layer, 0, j)),
            pl.BlockSpec((None, d, tn), lambda i, j: (layer, 0, j + n_tiles)),
        ],
        out_specs=pl.BlockSpec((tm, tn), lambda i, j: (i, j)),
        out_shape=jax.ShapeDtypeStruct((m, d_ff), BF16),
        scratch_shapes=[pltpu.VMEM((tm, d), BF16)],
        compiler_params=_params(2),
        name="ffn_up",
    )(x, g.reshape(1, d), w_gate_up, w_gate_up)


def _residual_matmul_kernel(h_ref, a_ref, w_ref, o_ref):
    o_ref[...] = h_ref[...] + jnp.dot(a_ref[...], w_ref[...], preferred_element_type=F32)


def residual_matmul(h, a, w, layer, *, tm, tn):
    m, n = h.shape
    k = a.shape[1]
    return pl.pallas_call(
        _residual_matmul_kernel,
        grid=(m // tm, n // tn),
        in_specs=[
            pl.BlockSpec((tm, tn), lambda i, j: (i, j)),
            pl.BlockSpec((tm, k), lambda i, j: (i, 0)),
            pl.BlockSpec((None, k, tn), lambda i, j: (layer, 0, j)),
        ],
        out_specs=pl.BlockSpec((tm, tn), lambda i, j: (i, j)),
        out_shape=jax.ShapeDtypeStruct((m, n), F32),
        compiler_params=_params(2),
        name="residual_matmul",
    )(h, a, w)


def _residual_matmul2_kernel(h_ref, a1_ref, a2_ref, w1_ref, w2_ref, o_ref):
    acc = jnp.dot(a1_ref[...], w1_ref[...], preferred_element_type=F32)
    acc = acc + jnp.dot(a2_ref[...], w2_ref[...], preferred_element_type=F32)
    o_ref[...] = h_ref[...] + acc


def residual_matmul2(h, a1, a2, w, layer, *, tm, tn):
    m, n = h.shape
    k1, k2 = a1.shape[1], a2.shape[1]
    assert k1 % k2 == 0
    return pl.pallas_call(
        _residual_matmul2_kernel,
        grid=(m // tm, n // tn),
        in_specs=[
            pl.BlockSpec((tm, tn), lambda i, j: (i, j)),
            pl.BlockSpec((tm, k1), lambda i, j: (i, 0)),
            pl.BlockSpec((tm, k2), lambda i, j: (i, 0)),
            pl.BlockSpec((None, k1, tn), lambda i, j: (layer, 0, j)),
            pl.BlockSpec((None, k2, tn), lambda i, j: (layer, k1 // k2, j)),
        ],
        out_specs=pl.BlockSpec((tm, tn), lambda i, j: (i, j)),
        out_shape=jax.ShapeDtypeStruct((m, n), F32),
        compiler_params=_params(2),
        name="residual_matmul2",
    )(h, a1, a2, w, w)


def _final_norm_kernel(x_ref, g_ref, o_ref):
    o_ref[...] = _rmsnorm_rows(x_ref[...], g_ref[...])


def final_norm(x, g, *, tm):
    m, d = x.shape
    return pl.pallas_call(
        _final_norm_kernel,
        grid=(m // tm,),
        in_specs=[pl.BlockSpec((tm, d), lambda i: (i, 0)), pl.BlockSpec((1, d), lambda i: (0, 0))],
        out_specs=pl.BlockSpec((tm, d), lambda i: (i, 0)),
        out_shape=jax.ShapeDtypeStruct((m, d), F32),
        compiler_params=_params(1),
        name="final_norm",
    )(x, g.reshape(1, d))


def _rel_bucket_np(rel):
    n = np.maximum(rel, 0)
    max_exact = N_BUCKETS // 2
    nf = np.maximum(n, 1).astype(np.float32)
    large = max_exact + (np.log(nf / np.float32(max_exact)) / np.float32(math.log(MAX_DISTANCE / max_exact))
                         * np.float32(N_BUCKETS - max_exact)).astype(np.int32)
    large = np.minimum(large, N_BUCKETS - 1)
    return np.where(n < max_exact, n, large).astype(np.int32)


def _bucket_tiles():
    qi = np.arange(BLK, dtype=np.int32)[:, None]
    ki = np.arange(BLK, dtype=np.int32)[None, :]
    diag = _rel_bucket_np(qi - ki)
    prev = _rel_bucket_np(qi - ki + BLK)
    assert _rel_bucket_np(np.array([BLK + 1]))[0] == N_BUCKETS - 1
    return jnp.asarray(diag), jnp.asarray(prev)


def _bias_tile(bkt, tab_ref, head):
    out = jnp.zeros(bkt.shape, F32)
    for b in range(N_BUCKETS):
        out = jnp.where(bkt == b, tab_ref[head, b] * LOG2E, out)
    return out


def _causal_tile(tile):
    qi = lax.broadcasted_iota(jnp.int32, tile.shape, 0)
    ki = lax.broadcasted_iota(jnp.int32, tile.shape, 1)
    return jnp.where(ki <= qi, tile, NEG)


def _wide(x):
    return jnp.tile(x, (1, BLK // HEAD_DIM))


def _softmax_step(s, v, m_ref, acc_ref, *, first, shift=None):
    v_aug = jnp.concatenate([v, jnp.ones(v.shape, v.dtype)], axis=1)
    m_cur = jnp.max(s, axis=-1, keepdims=True)
    if shift is not None:
        m_cur = m_cur + shift
    if first:
        m_new = jnp.broadcast_to(m_cur, m_ref.shape)
    else:
        m_old = m_ref[...]
        m_new = jnp.maximum(m_old, m_cur)
    p = jnp.exp2(s - _wide(m_new if shift is None else m_new - shift))
    pv = jnp.dot(p.astype(BF16), v_aug, preferred_element_type=F32)
    if first:
        acc_ref[...] = pv
    else:
        acc_ref[...] = _wide(jnp.exp2(m_old - m_new)) * acc_ref[...] + pv
    m_ref[...] = m_new


def _softmax_result(acc_ref):
    acc = acc_ref[...]
    return acc[:, :HEAD_DIM] / acc[:, HEAD_DIM:]


def _moba_kernel(tab_ref, bd_ref, bp_ref, q_ref, k_ref, v_ref, o_ref,
                 bias_d_ref, bias_p_ref, kmean_ref, qa_ref, m_ref, acc_ref, *, gate_fill):
    h, bg, qb = pl.program_id(0), pl.program_id(1), pl.program_id(2)
    group, seq, _ = k_ref.shape
    nb = seq // BLK

    @pl.when((bg == 0) & (qb == 0))
    def _():
        bias_d_ref[...] = _causal_tile(_bias_tile(bd_ref[...], tab_ref, h))
        bias_p_ref[...] = _bias_tile(bp_ref[...], tab_ref, h)

    @pl.when(qb == 0)
    def _():
        for g in range(group):
            kf = k_ref[g].astype(F32).reshape(nb, BLK, HEAD_DIM)
            kmean_ref[g] = jnp.mean(kf, axis=1)

    row = lax.broadcasted_iota(jnp.int32, (nb, BLK), 0)
    past = row < qb
    n_sel = max(1, min(MOBA_TOPK, nb - 1))
    for g in range(group):
        q = q_ref[g]
        gate = lax.dot_general(kmean_ref[g], q.astype(F32), NT_DIMS,
                               precision=lax.Precision.HIGHEST, preferred_element_type=F32)
        gate = jnp.where(past, gate, gate_fill)
        beaten_by = jnp.zeros(gate.shape, F32)
        for i in range(nb):
            gi = gate[i:i + 1, :]
            wins = (gi > gate) | ((gi == gate) & (i < row))
            beaten_by = beaten_by + jnp.where(wins, 1.0, 0.0)
        keep = (beaten_by < n_sel) & past
        sel_neg = jnp.where(keep, 0.0, NEG)
        sel_neg = jnp.concatenate([sel_neg, jnp.zeros((HEAD_DIM - nb, BLK), F32)], axis=0)
        qa_ref[g] = jnp.concatenate([q, sel_neg.T.astype(BF16)], axis=1)

    lane = lax.broadcasted_iota(jnp.int32, (BLK, HEAD_DIM), 1)

    def block_scores(g, j):
        start = pl.multiple_of(j * BLK, BLK)
        kj = k_ref[g, pl.ds(start, BLK), :]
        onehot = jnp.where(lane == j, 1.0, 0.0).astype(BF16)
        k_aug = jnp.concatenate([kj, onehot], axis=1)
        s = lax.dot_general(qa_ref[g], k_aug, NT_DIMS, preferred_element_type=F32)
        return s, v_ref[g, pl.ds(start, BLK), :]

    start = pl.multiple_of(qb * BLK, BLK)
    for g in range(group):
        s = lax.dot_general(q_ref[g], k_ref[g, pl.ds(start, BLK), :], NT_DIMS,
                            preferred_element_type=F32)
        _softmax_step(s + bias_d_ref[...], v_ref[g, pl.ds(start, BLK), :],
                      m_ref.at[g], acc_ref.at[g], first=True)

    @pl.when(qb >= 1)
    def _():
        for g in range(group):
            s, v = block_scores(g, qb - 1)
            _softmax_step(s + bias_p_ref[...], v, m_ref.at[g], acc_ref.at[g], first=False)

    far_bias = tab_ref[h, N_BUCKETS - 1] * LOG2E

    def far_body(d, carry):
        for g in range(group):
            s, v = block_scores(g, qb - d)
            _softmax_step(s, v, m_ref.at[g], acc_ref.at[g], first=False, shift=far_bias)
        return carry

    lax.fori_loop(2, qb + 1, far_body, 0)

    for g in range(group):
        o_ref[g] = _softmax_result(acc_ref.at[g]).astype(o_ref.dtype)


def moba_attention(proj, tab, bkt_diag, bkt_prev, *, group, gate_fill):
    batch, seq, _ = proj.shape
    nqb = seq // BLK
    k_off = SELF_WIDTH // HEAD_DIM
    kern = functools.partial(_moba_kernel, gate_fill=gate_fill)
    return pl.pallas_call(
        kern,
        grid=(N_SELF_HEADS, batch // group, nqb),
        in_specs=[
            pl.BlockSpec(memory_space=pltpu.SMEM),
            pl.BlockSpec((BLK, BLK), lambda h, b, qb: (0, 0)),
            pl.BlockSpec((BLK, BLK), lambda h, b, qb: (0, 0)),
            pl.BlockSpec((group, BLK, HEAD_DIM), lambda h, b, qb: (b, qb, h)),
            pl.BlockSpec((group, seq, HEAD_DIM), lambda h, b, qb: (b, 0, k_off + h)),
            pl.BlockSpec((group, seq, HEAD_DIM), lambda h, b, qb: (b, 0, 2 * k_off + h)),
        ],
        out_specs=pl.BlockSpec((group, BLK, HEAD_DIM), lambda h, b, qb: (b, qb, h)),
        out_shape=jax.ShapeDtypeStruct((batch, seq, SELF_WIDTH), BF16),
        scratch_shapes=[
            pltpu.VMEM((BLK, BLK), F32), pltpu.VMEM((BLK, BLK), F32),
            pltpu.VMEM((group, seq // BLK, HEAD_DIM), F32),
            pltpu.VMEM((group, BLK, 2 * HEAD_DIM), BF16),
            pltpu.VMEM((group, BLK, HEAD_DIM), F32), pltpu.VMEM((group, BLK, 2 * HEAD_DIM), F32),
        ],
        compiler_params=_params(3),
        name="moba_attention",
    )(tab, bkt_diag, bkt_prev, proj, proj, proj)


def _diff_kernel(tab_ref, bd_ref, bp_ref, lq_ref, gs_ref, q_ref, k_ref, v_ref, o_ref,
                 bias_d_ref, bias_p_ref, q2_ref, m_ref, acc_ref, *, lam_init):
    h, bg, qb = pl.program_id(0), pl.program_id(1), pl.program_id(2)
    group = k_ref.shape[0]

    @pl.when((bg == 0) & (qb == 0))
    def _():
        bd = _causal_tile(_bias_tile(bd_ref[...], tab_ref, h))
        bp = _bias_tile(bp_ref[...], tab_ref, h)
        bias_d_ref[...] = jnp.concatenate([bd, bd], axis=0)
        bias_p_ref[...] = jnp.concatenate([bp, bp], axis=0)

    lane = lax.broadcasted_iota(jnp.int32, (BLK, HEAD_DIM), 1)
    for g in range(group):
        q = q_ref[g]
        zero = jnp.zeros_like(q)
        q2_ref[g] = jnp.concatenate([jnp.where(lane < DIFF_HALF, q, zero),
                                     jnp.where(lane >= DIFF_HALF, q, zero)], axis=0)

    def block_scores(g, j):
        start = pl.multiple_of(j * BLK, BLK)
        s = lax.dot_general(q2_ref[g], k_ref[g, pl.ds(start, BLK), :], NT_DIMS,
                            preferred_element_type=F32)
        return s, v_ref[g, pl.ds(start, BLK), :]

    for g in range(group):
        s, v = block_scores(g, qb)
        _softmax_step(s + bias_d_ref[...], v, m_ref.at[g], acc_ref.at[g], first=True)

    @pl.when(qb >= 1)
    def _():
        for g in range(group):
            s, v = block_scores(g, qb - 1)
            _softmax_step(s + bias_p_ref[...], v, m_ref.at[g], acc_ref.at[g], first=False)

    far_bias = tab_ref[h, N_BUCKETS - 1] * LOG2E

    def far_body(d, carry):
        for g in range(group):
            s, v = block_scores(g, qb - d)
            _softmax_step(s, v, m_ref.at[g], acc_ref.at[g], first=False, shift=far_bias)
        return carry

    lax.fori_loop(2, qb + 1, far_body, 0)

    lq = lq_ref[...]
    lam = (jnp.exp(jnp.sum(lq[0:1] * lq[1:2], axis=-1, keepdims=True))
           - jnp.exp(jnp.sum(lq[2:3] * lq[3:4], axis=-1, keepdims=True)) + lam_init)
    for g in range(group):
        o_all = _softmax_result(acc_ref.at[g])
        o = o_all[:BLK] - lam * o_all[BLK:]
        o_ref[g] = (_rmsnorm_rows(o, gs_ref[...]) * (1.0 - lam_init)).astype(o_ref.dtype)


def diff_attention(proj, kv, tab, bkt_diag, bkt_prev, lq, g_subln, *, group, lam_init):
    batch, seq, _ = proj.shape
    nqb = seq // BLK
    v_off = SELF_WIDTH // HEAD_DIM
    kern = functools.partial(_diff_kernel, lam_init=lam_init)
    return pl.pallas_call(
        kern,
        grid=(N_SELF_HEADS, batch // group, nqb),
        in_specs=[
            pl.BlockSpec(memory_space=pltpu.SMEM),
            pl.BlockSpec((BLK, BLK), lambda h, b, qb: (0, 0)),
            pl.BlockSpec((BLK, BLK), lambda h, b, qb: (0, 0)),
            pl.BlockSpec((4, DIFF_HALF), lambda h, b, qb: (0, 0)),
            pl.BlockSpec((1, HEAD_DIM), lambda h, b, qb: (0, 0)),
            pl.BlockSpec((group, BLK, HEAD_DIM), lambda h, b, qb: (b, qb, h)),
            pl.BlockSpec((group, seq, HEAD_DIM), lambda h, b, qb: (b, 0, h)),
            pl.BlockSpec((group, seq, HEAD_DIM), lambda h, b, qb: (b, 0, v_off + h)),
        ],
        out_specs=pl.BlockSpec((group, BLK, HEAD_DIM), lambda h, b, qb: (b, qb, h)),
        out_shape=jax.ShapeDtypeStruct((batch, seq, SELF_WIDTH), BF16),
        scratch_shapes=[
            pltpu.VMEM((2 * BLK, BLK), F32), pltpu.VMEM((2 * BLK, BLK), F32),
            pltpu.VMEM((group, 2 * BLK, HEAD_DIM), BF16),
            pltpu.VMEM((group, 2 * BLK, HEAD_DIM), F32),
            pltpu.VMEM((group, 2 * BLK, 2 * HEAD_DIM), F32),
        ],
        compiler_params=_params(3),
        name="diff_attention",
    )(tab, bkt_diag, bkt_prev, lq, g_subln.reshape(1, HEAD_DIM), proj, kv, kv)


def _mem_kernel(q_ref, mk_ref, mv_ref, o_ref):
    s = lax.dot_general(q_ref[...], mk_ref[...], NT_DIMS, preferred_element_type=F32)
    m = jnp.max(s, axis=-1, keepdims=True)
    p = jnp.exp2(s - m)
    l = jnp.sum(p, axis=-1, keepdims=True)
    o = jnp.dot(p.astype(BF16), mv_ref[...], preferred_element_type=F32)
    o_ref[...] = (o / l).astype(o_ref.dtype)


def mem_attention(proj, mkv, *, batch, seq, n_mem, q_col_block, tq):
    nsb = seq // tq
    return pl.pallas_call(
        _mem_kernel,
        grid=(N_MEM_HEADS, batch, nsb),
        in_specs=[
            pl.BlockSpec((tq, HEAD_DIM), lambda h, b, sb: (b * nsb + sb, q_col_block + h)),
            pl.BlockSpec((n_mem, HEAD_DIM), lambda h, b, sb: (b, h)),
            pl.BlockSpec((n_mem, HEAD_DIM), lambda h, b, sb: (b, N_MEM_HEADS + h)),
        ],
        out_specs=pl.BlockSpec((tq, HEAD_DIM), lambda h, b, sb: (b * nsb + sb, h)),
        out_shape=jax.ShapeDtypeStruct((batch * seq, MEM_WIDTH), BF16),
        compiler_params=_params(3),
        name="mem_attention",
    )(proj, mkv, mkv)


ATTN_GROUP = 4


def kernel(x, mem, rel_bias, g_mix, w_in_a, w_in_b, g_mem, w_mem_kv, w_o, g_ffn, w_gate_up, w_down,
           g_kv, w_kv_shared, lambda_qk, g_subln, g_final):
    batch, seq, d = x.shape
    n_mem = mem.shape[1]
    depth = g_mix.shape[0]
    n_a = w_in_a.shape[0]
    tokens = batch * seq
    group = math.gcd(batch, ATTN_GROUP)

    w_in_a, w_in_b, w_mem_kv, w_o, w_gate_up, w_down = (
        w.astype(BF16) for w in (w_in_a, w_in_b, w_mem_kv, w_o, w_gate_up, w_down))
    w_kv_shared = w_kv_shared.astype(BF16)[None]

    tab = rel_bias.T.astype(F32)
    bkt_diag, bkt_prev = _bucket_tiles()

    moba_scale = HEAD_DIM ** -0.5 * LOG2E
    diff_scale = DIFF_HALF ** -0.5 * LOG2E
    mem_scale = HEAD_DIM ** -0.5 * LOG2E
    ones = functools.partial(jnp.ones, dtype=F32)
    cs_a = jnp.concatenate([jnp.full((SELF_WIDTH,), moba_scale, F32), ones((2 * SELF_WIDTH,)),
                            jnp.full((MEM_WIDTH,), mem_scale, F32)])
    cs_b = jnp.concatenate([jnp.full((SELF_WIDTH,), diff_scale, F32),
                            jnp.full((MEM_WIDTH,), mem_scale, F32)])

    h = x.reshape(tokens, d)
    mem2 = mem.reshape(batch * n_mem, d)
    kv = None
    for l in range(depth):
        mkv = norm_matmul(mem2, g_mem[l], w_mem_kv, l, ones((2 * MEM_WIDTH,)),
                          tm=batch * n_mem, tn=512)
        if l < n_a:
            proj = norm_matmul(h, g_mix[l], w_in_a, l, cs_a, tm=1024, tn=512)
            y_self = moba_attention(proj.reshape(batch, seq, -1), tab, bkt_diag, bkt_prev,
                                    group=group, gate_fill=NEG * moba_scale)
            q_col_block = 3 * SELF_WIDTH // HEAD_DIM
        else:
            j = l - n_a
            if kv is None:
                kv = norm_matmul(h, g_kv, w_kv_shared, 0, ones((2 * SELF_WIDTH,)), tm=1024, tn=512)
                kv = kv.reshape(batch, seq, -1)
            proj = norm_matmul(h, g_mix[l], w_in_b, j, cs_b, tm=1024, tn=512)
            lam_init = 0.8 - 0.6 * math.exp(-0.3 * l)
            y_self = diff_attention(proj.reshape(batch, seq, -1), kv, tab, bkt_diag, bkt_prev,
                                    lambda_qk[j], g_subln[j], group=group, lam_init=lam_init)
            q_col_block = SELF_WIDTH // HEAD_DIM
        y_mem = mem_attention(proj, mkv, batch=batch, seq=seq, n_mem=n_mem,
                              q_col_block=q_col_block, tq=512)
        h = residual_matmul2(h, y_self.reshape(tokens, -1), y_mem, w_o, l, tm=1024, tn=512)
        act = ffn_up(h, g_ffn[l], w_gate_up, l, tm=1024, tn=512)
        h = residual_matmul(h, act, w_down, l, tm=512, tn=512)
    return final_norm(h, g_final, tm=512).reshape(batch, seq, d)
```

```python
import functools
import math

import jax
import jax.numpy as jnp
import numpy as np
from jax import lax
from jax.experimental import pallas as pl
from jax.experimental.pallas import tpu as pltpu

D_MODEL = 2048
HEAD_DIM = 128
N_MEM_HEADS = 4
MEM_WIDTH = N_MEM_HEADS * HEAD_DIM
SELF_WIDTH = D_MODEL - MEM_WIDTH
N_SELF_HEADS = SELF_WIDTH // HEAD_DIM
DIFF_HALF = HEAD_DIM // 2
MOBA_BLOCK = 256
MOBA_TOPK = 3
N_BUCKETS = 32
MAX_DISTANCE = 128
RMS_EPS = 1e-6
NEG = -1e30
LOG2E = math.log2(math.e)

BLK = MOBA_BLOCK
V7X_VMEM_BYTES = 64 * 1024 * 1024
VMEM_LIMIT = 56 * 1024 * 1024

F32 = jnp.float32
BF16 = jnp.bfloat16
NT_DIMS = (((1,), (1,)), ((), ()))


def _params(n_grid_axes):
    return pltpu.CompilerParams(
        dimension_semantics=("arbitrary",) * n_grid_axes, vmem_limit_bytes=VMEM_LIMIT)


def _rmsnorm_rows(x, g):
    ms = jnp.mean(x * x, axis=-1, keepdims=True)
    return x * lax.rsqrt(ms + RMS_EPS) * g


def _norm_matmul_kernel(x_ref, g_ref, w_ref, cs_ref, o_ref, xn_ref):
    @pl.when(pl.program_id(1) == 0)
    def _():
        xn_ref[...] = _rmsnorm_rows(x_ref[...], g_ref[...]).astype(BF16)

    acc = jnp.dot(xn_ref[...], w_ref[...], preferred_element_type=F32)
    o_ref[...] = (acc * cs_ref[...]).astype(o_ref.dtype)


def norm_matmul(x, g, w, layer, col_scale, *, tm, tn):
    m, d = x.shape
    n = w.shape[-1]
    return pl.pallas_call(
        _norm_matmul_kernel,
        grid=(m // tm, n // tn),
        in_specs=[
            pl.BlockSpec((tm, d), lambda i, j: (i, 0)),
            pl.BlockSpec((1, d), lambda i, j: (0, 0)),
            pl.BlockSpec((None, d, tn), lambda i, j: (layer, 0, j)),
            pl.BlockSpec((1, tn), lambda i, j: (0, j)),
        ],
        out_specs=pl.BlockSpec((tm, tn), lambda i, j: (i, j)),
        out_shape=jax.ShapeDtypeStruct((m, n), BF16),
        scratch_shapes=[pltpu.VMEM((tm, d), BF16)],
        compiler_params=_params(2),
        name="norm_matmul",
    )(x, g.reshape(1, d), w, col_scale.reshape(1, n))


def _ffn_up_kernel(x_ref, g_ref, wg_ref, wu_ref, o_ref, xn_ref):
    @pl.when(pl.program_id(1) == 0)
    def _():
        xn_ref[...] = _rmsnorm_rows(x_ref[...], g_ref[...]).astype(BF16)

    xn = xn_ref[...]
    gate = jnp.dot(xn, wg_ref[...], preferred_element_type=F32)
    up = jnp.dot(xn, wu_ref[...], preferred_element_type=F32)
    o_ref[...] = (gate * jax.nn.sigmoid(gate) * up).astype(o_ref.dtype)


def ffn_up(x, g, w_gate_up, layer, *, tm, tn):
    m, d = x.shape
    d_ff = w_gate_up.shape[-1] // 2
    n_tiles = d_ff // tn
    return pl.pallas_call(
        _ffn_up_kernel,
        grid=(m // tm, n_tiles),
        in_specs=[
            pl.BlockSpec((tm, d), lambda i, j: (i, 0)),
            pl.BlockSpec((1, d), lambda i, j: (0, 0)),
            pl.BlockSpec((None, d, tn), lambda i, j: (layer, 0, j)),
            pl.BlockSpec((None, d, tn), lambda i, j: (layer, 0, j + n_tiles)),
        ],
        out_specs=pl.BlockSpec((tm, tn), lambda i, j: (i, j)),
        out_shape=jax.ShapeDtypeStruct((m, d_ff), BF16),
        scratch_shapes=[pltpu.VMEM((tm, d), BF16)],
        compiler_params=_params(2),
        name="ffn_up",
    )(x, g.reshape(1, d), w_gate_up, w_gate_up)


def _residual_matmul_kernel(h_ref, a_ref, w_ref, o_ref):
    o_ref[...] = h_ref[...] + jnp.dot(a_ref[...], w_ref[...], preferred_element_type=F32)


def residual_matmul(h, a, w, layer, *, tm, tn):
    m, n = h.shape
    k = a.shape[1]
    return pl.pallas_call(
        _residual_matmul_kernel,
        grid=(m // tm, n // tn),
        in_specs=[
            pl.BlockSpec((tm, tn), lambda i, j: (i, j)),
            pl.BlockSpec((tm, k), lambda i, j: (i, 0)),
            pl.BlockSpec((None, k, tn), lambda i, j: (layer, 0, j)),
        ],
        out_specs=pl.BlockSpec((tm, tn), lambda i, j: (i, j)),
        out_shape=jax.ShapeDtypeStruct((m, n), F32),
        compiler_params=_params(2),
        name="residual_matmul",
    )(h, a, w)


def _residual_matmul2_kernel(h_ref, a1_ref, a2_ref, w1_ref, w2_ref, o_ref):
    acc = jnp.dot(a1_ref[...], w1_ref[...], preferred_element_type=F32)
    acc = acc + jnp.dot(a2_ref[...], w2_ref[...], preferred_element_type=F32)
    o_ref[...] = h_ref[...] + acc


def residual_matmul2(h, a1, a2, w, layer, *, tm, tn):
    m, n = h.shape
    k1, k2 = a1.shape[1], a2.shape[1]
    assert k1 % k2 == 0
    return pl.pallas_call(
        _residual_matmul2_kernel,
        grid=(m // tm, n // tn),
        in_specs=[
            pl.BlockSpec((tm, tn), lambda i, j: (i, j)),
            pl.BlockSpec((tm, k1), lambda i, j: (i, 0)),
            pl.BlockSpec((tm, k2), lambda i, j: (i, 0)),
            pl.BlockSpec((None, k1, tn), lambda i, j: (layer, 0, j)),
            pl.BlockSpec((None, k2, tn), lambda i, j: (layer, k1 // k2, j)),
        ],
        out_specs=pl.BlockSpec((tm, tn), lambda i, j: (i, j)),
        out_shape=jax.ShapeDtypeStruct((m, n), F32),
        compiler_params=_params(2),
        name="residual_matmul2",
    )(h, a1, a2, w, w)


def _final_norm_kernel(x_ref, g_ref, o_ref):
    o_ref[...] = _rmsnorm_rows(x_ref[...], g_ref[...])


def final_norm(x, g, *, tm):
    m, d = x.shape
    return pl.pallas_call(
        _final_norm_kernel,
        grid=(m // tm,),
        in_specs=[pl.BlockSpec((tm, d), lambda i: (i, 0)), pl.BlockSpec((1, d), lambda i: (0, 0))],
        out_specs=pl.BlockSpec((tm, d), lambda i: (i, 0)),
        out_shape=jax.ShapeDtypeStruct((m, d), F32),
        compiler_params=_params(1),
        name="final_norm",
    )(x, g.reshape(1, d))


def _rel_bucket_np(rel):
    n = np.maximum(rel, 0)
    max_exact = N_BUCKETS // 2
    nf = np.maximum(n, 1).astype(np.float32)
    large = max_exact + (np.log(nf / np.float32(max_exact)) / np.float32(math.log(MAX_DISTANCE / max_exact))
                         * np.float32(N_BUCKETS - max_exact)).astype(np.int32)
    large = np.minimum(large, N_BUCKETS - 1)
    return np.where(n < max_exact, n, large).astype(np.int32)


def _bucket_tiles():
    qi = np.arange(BLK, dtype=np.int32)[:, None]
    ki = np.arange(BLK, dtype=np.int32)[None, :]
    diag = _rel_bucket_np(qi - ki)
    prev = _rel_bucket_np(qi - ki + BLK)
    assert _rel_bucket_np(np.array([BLK + 1]))[0] == N_BUCKETS - 1
    return jnp.asarray(diag), jnp.asarray(prev)


def _bias_tile(bkt, tab_ref, head):
    out = jnp.zeros(bkt.shape, F32)
    for b in range(N_BUCKETS):
        out = jnp.where(bkt == b, tab_ref[head, b] * LOG2E, out)
    return out


def _causal_tile(tile):
    qi = lax.broadcasted_iota(jnp.int32, tile.shape, 0)
    ki = lax.broadcasted_iota(jnp.int32, tile.shape, 1)
    return jnp.where(ki <= qi, tile, NEG)


def _wide(x):
    return jnp.tile(x, (1, BLK // HEAD_DIM))


def _softmax_step(s, v, m_ref, acc_ref, *, first, shift=None):
    v_aug = jnp.concatenate([v, jnp.ones(v.shape, v.dtype)], axis=1)
    m_cur = jnp.max(s, axis=-1, keepdims=True)
    if shift is not None:
        m_cur = m_cur + shift
    if first:
        m_new = jnp.broadcast_to(m_cur, m_ref.shape)
    else:
        m_old = m_ref[...]
        m_new = jnp.maximum(m_old, m_cur)
    p = jnp.exp2(s - _wide(m_new if shift is None else m_new - shift))
    pv = jnp.dot(p.astype(BF16), v_aug, preferred_element_type=F32)
    if first:
        acc_ref[...] = pv
    else:
        acc_ref[...] = _wide(jnp.exp2(m_old - m_new)) * acc_ref[...] + pv
    m_ref[...] = m_new


def _softmax_result(acc_ref):
    acc = acc_ref[...]
    return acc[:, :HEAD_DIM] / acc[:, HEAD_DIM:]


def _head_cols(hh):
    return slice(hh * HEAD_DIM, (hh + 1) * HEAD_DIM)


def _moba_kernel(tab_ref, bd_ref, bp_ref, q_ref, k_ref, v_ref, o_ref,
                 bias_d_ref, bias_p_ref, kmean_ref, qa_ref, m_ref, acc_ref, *, gate_fill, heads):
    hg, bg, qb = pl.program_id(0), pl.program_id(1), pl.program_id(2)
    group, seq, _ = k_ref.shape
    nb = seq // BLK
    chains = [(hh, g) for hh in range(heads) for g in range(group)]

    @pl.when((bg == 0) & (qb == 0))
    def _():
        for hh in range(heads):
            h = hg * heads + hh
            bias_d_ref[hh] = _causal_tile(_bias_tile(bd_ref[...], tab_ref, h))
            bias_p_ref[hh] = _bias_tile(bp_ref[...], tab_ref, h)

    @pl.when(qb == 0)
    def _():
        for c, (hh, g) in enumerate(chains):
            kf = k_ref[g, :, _head_cols(hh)].astype(F32).reshape(nb, BLK, HEAD_DIM)
            kmean_ref[c] = jnp.mean(kf, axis=1)

    row = lax.broadcasted_iota(jnp.int32, (nb, BLK), 0)
    past = row < qb
    n_sel = max(1, min(MOBA_TOPK, nb - 1))
    for c, (hh, g) in enumerate(chains):
        q = q_ref[g, :, _head_cols(hh)]
        gate = lax.dot_general(kmean_ref[c], q.astype(F32), NT_DIMS,
                               precision=lax.Precision.HIGHEST, preferred_element_type=F32)
        gate = jnp.where(past, gate, gate_fill)
        beaten_by = jnp.zeros(gate.shape, F32)
        for i in range(nb):
            gi = gate[i:i + 1, :]
            wins = (gi > gate) | ((gi == gate) & (i < row))
            beaten_by = beaten_by + jnp.where(wins, 1.0, 0.0)
        keep = (beaten_by < n_sel) & past
        sel_neg = jnp.where(keep, 0.0, NEG)
        sel_neg = jnp.concatenate([sel_neg, jnp.zeros((HEAD_DIM - nb, BLK), F32)], axis=0)
        qa_ref[c] = jnp.concatenate([q, sel_neg.T.astype(BF16)], axis=1)

    lane = lax.broadcasted_iota(jnp.int32, (BLK, HEAD_DIM), 1)

    def block_scores(c, j):
        hh, g = chains[c]
        start = pl.multiple_of(j * BLK, BLK)
        kj = k_ref[g, pl.ds(start, BLK), _head_cols(hh)]
        onehot = jnp.where(lane == j, 1.0, 0.0).astype(BF16)
        k_aug = jnp.concatenate([kj, onehot], axis=1)
        s = lax.dot_general(qa_ref[c], k_aug, NT_DIMS, preferred_element_type=F32)
        return s, v_ref[g, pl.ds(start, BLK), _head_cols(hh)]

    start = pl.multiple_of(qb * BLK, BLK)
    for c, (hh, g) in enumerate(chains):
        s = lax.dot_general(q_ref[g, :, _head_cols(hh)], k_ref[g, pl.ds(start, BLK), _head_cols(hh)],
                            NT_DIMS, preferred_element_type=F32)
        _softmax_step(s + bias_d_ref[hh], v_ref[g, pl.ds(start, BLK), _head_cols(hh)],
                      m_ref.at[c], acc_ref.at[c], first=True)

    @pl.when(qb >= 1)
    def _():
        for c, (hh, g) in enumerate(chains):
            s, v = block_scores(c, qb - 1)
            _softmax_step(s + bias_p_ref[hh], v, m_ref.at[c], acc_ref.at[c], first=False)

    far_bias = [tab_ref[hg * heads + hh, N_BUCKETS - 1] * LOG2E for hh in range(heads)]

    def far_body(d, carry):
        for c, (hh, g) in enumerate(chains):
            s, v = block_scores(c, qb - d)
            _softmax_step(s, v, m_ref.at[c], acc_ref.at[c], first=False, shift=far_bias[hh])
        return carry

    lax.fori_loop(2, qb + 1, far_body, 0)

    for c, (hh, g) in enumerate(chains):
        o_ref[g, :, _head_cols(hh)] = _softmax_result(acc_ref.at[c]).astype(o_ref.dtype)


def moba_attention(proj, tab, bkt_diag, bkt_prev, *, group, heads, gate_fill):
    batch, seq, _ = proj.shape
    nqb = seq // BLK
    k_off = N_SELF_HEADS // heads
    width = heads * HEAD_DIM
    chains = heads * group
    kern = functools.partial(_moba_kernel, gate_fill=gate_fill, heads=heads)
    return pl.pallas_call(
        kern,
        grid=(N_SELF_HEADS // heads, batch // group, nqb),
        in_specs=[
            pl.BlockSpec(memory_space=pltpu.SMEM),
            pl.BlockSpec((BLK, BLK), lambda h, b, qb: (0, 0)),
            pl.BlockSpec((BLK, BLK), lambda h, b, qb: (0, 0)),
            pl.BlockSpec((group, BLK, width), lambda h, b, qb: (b, qb, h)),
            pl.BlockSpec((group, seq, width), lambda h, b, qb: (b, 0, k_off + h)),
            pl.BlockSpec((group, seq, width), lambda h, b, qb: (b, 0, 2 * k_off + h)),
        ],
        out_specs=pl.BlockSpec((group, BLK, width), lambda h, b, qb: (b, qb, h)),
        out_shape=jax.ShapeDtypeStruct((batch, seq, SELF_WIDTH), BF16),
        scratch_shapes=[
            pltpu.VMEM((heads, BLK, BLK), F32), pltpu.VMEM((heads, BLK, BLK), F32),
            pltpu.VMEM((chains, seq // BLK, HEAD_DIM), F32),
            pltpu.VMEM((chains, BLK, 2 * HEAD_DIM), BF16),
            pltpu.VMEM((chains, BLK, HEAD_DIM), F32), pltpu.VMEM((chains, BLK, 2 * HEAD_DIM), F32),
        ],
        compiler_params=_params(3),
        name="moba_attention",
    )(tab, bkt_diag, bkt_prev, proj, proj, proj)


def _diff_kernel(tab_ref, bd_ref, bp_ref, lq_ref, gs_ref, q_ref, k_ref, v_ref, o_ref,
                 bias_d_ref, bias_p_ref, q2_ref, m_ref, acc_ref, *, lam_init, heads):
    hg, bg, qb = pl.program_id(0), pl.program_id(1), pl.program_id(2)
    group = k_ref.shape[0]
    chains = [(hh, g) for hh in range(heads) for g in range(group)]

    @pl.when((bg == 0) & (qb == 0))
    def _():
        for hh in range(heads):
            h = hg * heads + hh
            bd = _causal_tile(_bias_tile(bd_ref[...], tab_ref, h))
            bp = _bias_tile(bp_ref[...], tab_ref, h)
            bias_d_ref[hh] = jnp.concatenate([bd, bd], axis=0)
            bias_p_ref[hh] = jnp.concatenate([bp, bp], axis=0)

    lane = lax.broadcasted_iota(jnp.int32, (BLK, HEAD_DIM), 1)
    for c, (hh, g) in enumerate(chains):
        q = q_ref[g, :, _head_cols(hh)]
        zero = jnp.zeros_like(q)
        q2_ref[c] = jnp.concatenate([jnp.where(lane < DIFF_HALF, q, zero),
                                     jnp.where(lane >= DIFF_HALF, q, zero)], axis=0)

    def block_scores(c, j):
        hh, g = chains[c]
        start = pl.multiple_of(j * BLK, BLK)
        s = lax.dot_general(q2_ref[c], k_ref[g, pl.ds(start, BLK), _head_cols(hh)], NT_DIMS,
                            preferred_element_type=F32)
        return s, v_ref[g, pl.ds(start, BLK), _head_cols(hh)]

    for c, (hh, g) in enumerate(chains):
        s, v = block_scores(c, qb)
        _softmax_step(s + bias_d_ref[hh], v, m_ref.at[c], acc_ref.at[c], first=True)

    @pl.when(qb >= 1)
    def _():
        for c, (hh, g) in enumerate(chains):
            s, v = block_scores(c, qb - 1)
            _softmax_step(s + bias_p_ref[hh], v, m_ref.at[c], acc_ref.at[c], first=False)

    far_bias = [tab_ref[hg * heads + hh, N_BUCKETS - 1] * LOG2E for hh in range(heads)]

    def far_body(d, carry):
        for c, (hh, g) in enumerate(chains):
            s, v = block_scores(c, qb - d)
            _softmax_step(s, v, m_ref.at[c], acc_ref.at[c], first=False, shift=far_bias[hh])
        return carry

    lax.fori_loop(2, qb + 1, far_body, 0)

    lq = lq_ref[...]
    lam = (jnp.exp(jnp.sum(lq[0:1] * lq[1:2], axis=-1, keepdims=True))
           - jnp.exp(jnp.sum(lq[2:3] * lq[3:4], axis=-1, keepdims=True)) + lam_init)
    for c, (hh, g) in enumerate(chains):
        o_all = _softmax_result(acc_ref.at[c])
        o = o_all[:BLK] - lam * o_all[BLK:]
        o_ref[g, :, _head_cols(hh)] = (
            _rmsnorm_rows(o, gs_ref[...]) * (1.0 - lam_init)).astype(o_ref.dtype)


def diff_attention(proj, kv, tab, bkt_diag, bkt_prev, lq, g_subln, *, group, heads, lam_init):
    batch, seq, _ = proj.shape
    nqb = seq // BLK
    v_off = N_SELF_HEADS // heads
    width = heads * HEAD_DIM
    chains = heads * group
    kern = functools.partial(_diff_kernel, lam_init=lam_init, heads=heads)
    return pl.pallas_call(
        kern,
        grid=(N_SELF_HEADS // heads, batch // group, nqb),
        in_specs=[
            pl.BlockSpec(memory_space=pltpu.SMEM),
            pl.BlockSpec((BLK, BLK), lambda h, b, qb: (0, 0)),
            pl.BlockSpec((BLK, BLK), lambda h, b, qb: (0, 0)),
            pl.BlockSpec((4, DIFF_HALF), lambda h, b, qb: (0, 0)),
            pl.BlockSpec((1, HEAD_DIM), lambda h, b, qb: (0, 0)),
            pl.BlockSpec((group, BLK, width), lambda h, b, qb: (b, qb, h)),
            pl.BlockSpec((group, seq, width), lambda h, b, qb: (b, 0, h)),
            pl.BlockSpec((group, seq, width), lambda h, b, qb: (b, 0, v_off + h)),
        ],
        out_specs=pl.BlockSpec((group, BLK, width), lambda h, b, qb: (b, qb, h)),
        out_shape=jax.ShapeDtypeStruct((batch, seq, SELF_WIDTH), BF16),
        scratch_shapes=[
            pltpu.VMEM((heads, 2 * BLK, BLK), F32), pltpu.VMEM((heads, 2 * BLK, BLK), F32),
            pltpu.VMEM((chains, 2 * BLK, HEAD_DIM), BF16),
            pltpu.VMEM((chains, 2 * BLK, HEAD_DIM), F32),
            pltpu.VMEM((chains, 2 * BLK, 2 * HEAD_DIM), F32),
        ],
        compiler_params=_params(3),
        name="diff_attention",
    )(tab, bkt_diag, bkt_prev, lq, g_subln.reshape(1, HEAD_DIM), proj, kv, kv)


def _mem_kernel(q_ref, mk_ref, mv_ref, o_ref):
    s = lax.dot_general(q_ref[...], mk_ref[...], NT_DIMS, preferred_element_type=F32)
    m = jnp.max(s, axis=-1, keepdims=True)
    p = jnp.exp2(s - m)
    l = jnp.sum(p, axis=-1, keepdims=True)
    o = jnp.dot(p.astype(BF16), mv_ref[...], preferred_element_type=F32)
    o_ref[...] = (o / l).astype(o_ref.dtype)


def mem_attention(proj, mkv, *, batch, seq, n_mem, q_col_block, tq):
    nsb = seq // tq
    return pl.pallas_call(
        _mem_kernel,
        grid=(N_MEM_HEADS, batch, nsb),
        in_specs=[
            pl.BlockSpec((tq, HEAD_DIM), lambda h, b, sb: (b * nsb + sb, q_col_block + h)),
            pl.BlockSpec((n_mem, HEAD_DIM), lambda h, b, sb: (b, h)),
            pl.BlockSpec((n_mem, HEAD_DIM), lambda h, b, sb: (b, N_MEM_HEADS + h)),
        ],
        out_specs=pl.BlockSpec((tq, HEAD_DIM), lambda h, b, sb: (b * nsb + sb, h)),
        out_shape=jax.ShapeDtypeStruct((batch * seq, MEM_WIDTH), BF16),
        compiler_params=_params(3),
        name="mem_attention",
    )(proj, mkv, mkv)


ATTN_GROUP = 4
ATTN_HEADS = 2


def kernel(x, mem, rel_bias, g_mix, w_in_a, w_in_b, g_mem, w_mem_kv, w_o, g_ffn, w_gate_up, w_down,
           g_kv, w_kv_shared, lambda_qk, g_subln, g_final):
    batch, seq, d = x.shape
    n_mem = mem.shape[1]
    depth = g_mix.shape[0]
    n_a = w_in_a.shape[0]
    tokens = batch * seq
    group = math.gcd(batch, ATTN_GROUP)

    w_in_a, w_in_b, w_mem_kv, w_o, w_gate_up, w_down = (
        w.astype(BF16) for w in (w_in_a, w_in_b, w_mem_kv, w_o, w_gate_up, w_down))
    w_kv_shared = w_kv_shared.astype(BF16)[None]

    tab = rel_bias.T.astype(F32)
    bkt_diag, bkt_prev = _bucket_tiles()

    moba_scale = HEAD_DIM ** -0.5 * LOG2E
    diff_scale = DIFF_HALF ** -0.5 * LOG2E
    mem_scale = HEAD_DIM ** -0.5 * LOG2E
    ones = functools.partial(jnp.ones, dtype=F32)
    cs_a = jnp.concatenate([jnp.full((SELF_WIDTH,), moba_scale, F32), ones((2 * SELF_WIDTH,)),
                            jnp.full((MEM_WIDTH,), mem_scale, F32)])
    cs_b = jnp.concatenate([jnp.full((SELF_WIDTH,), diff_scale, F32),
                            jnp.full((MEM_WIDTH,), mem_scale, F32)])

    h = x.reshape(tokens, d)
    mem2 = mem.reshape(batch * n_mem, d)
    kv = None
    for l in range(depth):
        mkv = norm_matmul(mem2, g_mem[l], w_mem_kv, l, ones((2 * MEM_WIDTH,)),
                          tm=batch * n_mem, tn=512)
        if l < n_a:
            proj = norm_matmul(h, g_mix[l], w_in_a, l, cs_a, tm=1024, tn=1024)
            y_self = moba_attention(proj.reshape(batch, seq, -1), tab, bkt_diag, bkt_prev,
                                    group=group, heads=ATTN_HEADS, gate_fill=NEG * moba_scale)
            q_col_block = 3 * SELF_WIDTH // HEAD_DIM
        else:
            j = l - n_a
            if kv is None:
                kv = norm_matmul(h, g_kv, w_kv_shared, 0, ones((2 * SELF_WIDTH,)), tm=1024, tn=1024)
                kv = kv.reshape(batch, seq, -1)
            proj = norm_matmul(h, g_mix[l], w_in_b, j, cs_b, tm=1024, tn=1024)
            lam_init = 0.8 - 0.6 * math.exp(-0.3 * l)
            y_self = diff_attention(proj.reshape(batch, seq, -1), kv, tab, bkt_diag, bkt_prev,
                                    lambda_qk[j], g_subln[j], group=group, heads=ATTN_HEADS,
                                    lam_init=lam_init)
            q_col_block = SELF_WIDTH // HEAD_DIM
        y_mem = mem_attention(proj, mkv, batch=batch, seq=seq, n_mem=n_mem,
                              q_col_block=q_col_block, tq=512)
        h = residual_matmul2(h, y_self.reshape(tokens, -1), y_mem, w_o, l, tm=2048, tn=512)
        act = ffn_up(h, g_ffn[l], w_gate_up, l, tm=1024, tn=512)
        h = residual_matmul(h, act, w_down, l, tm=1024, tn=512)
    return final_norm(h, g_final, tm=512).reshape(batch, seq, d)
```

```python
import functools
import math

import jax
import jax.numpy as jnp
import numpy as np
from jax import lax
from jax.experimental import pallas as pl
from jax.experimental.pallas import tpu as pltpu

D_MODEL = 2048
HEAD_DIM = 128
N_MEM_HEADS = 4
MEM_WIDTH = N_MEM_HEADS * HEAD_DIM
SELF_WIDTH = D_MODEL - MEM_WIDTH
N_SELF_HEADS = SELF_WIDTH // HEAD_DIM
DIFF_HALF = HEAD_DIM // 2
MOBA_BLOCK = 256
MOBA_TOPK = 3
N_BUCKETS = 32
MAX_DISTANCE = 128
RMS_EPS = 1e-6
NEG = -1e30
LOG2E = math.log2(math.e)

BLK = MOBA_BLOCK
V7X_VMEM_BYTES = 64 * 1024 * 1024
VMEM_LIMIT = 56 * 1024 * 1024

FFN_VMEM_LIMIT = 60 * 1024 * 1024

F32 = jnp.float32
BF16 = jnp.bfloat16
NT_DIMS = (((1,), (1,)), ((), ()))


def _params(n_grid_axes):
    return pltpu.CompilerParams(
        dimension_semantics=("arbitrary",) * n_grid_axes, vmem_limit_bytes=VMEM_LIMIT)


def _rmsnorm_rows(x, g):
    ms = jnp.mean(x * x, axis=-1, keepdims=True)
    return x * lax.rsqrt(ms + RMS_EPS) * g


def _norm_matmul_kernel(x_ref, g_ref, w_ref, cs_ref, o_ref, xn_ref):
    @pl.when(pl.program_id(1) == 0)
    def _():
        xn_ref[...] = _rmsnorm_rows(x_ref[...], g_ref[...]).astype(BF16)

    acc = jnp.dot(xn_ref[...], w_ref[...], preferred_element_type=F32)
    o_ref[...] = (acc * cs_ref[...]).astype(o_ref.dtype)


def norm_matmul(x, g, w, layer, col_scale, *, tm, tn):
    m, d = x.shape
    n = w.shape[-1]
    return pl.pallas_call(
        _norm_matmul_kernel,
        grid=(m // tm, n // tn),
        in_specs=[
            pl.BlockSpec((tm, d), lambda i, j: (i, 0)),
            pl.BlockSpec((1, d), lambda i, j: (0, 0)),
            pl.BlockSpec((None, d, tn), lambda i, j: (layer, 0, j)),
            pl.BlockSpec((1, tn), lambda i, j: (0, j)),
        ],
        out_specs=pl.BlockSpec((tm, tn), lambda i, j: (i, j)),
        out_shape=jax.ShapeDtypeStruct((m, n), BF16),
        scratch_shapes=[pltpu.VMEM((tm, d), BF16)],
        compiler_params=_params(2),
        name="norm_matmul",
    )(x, g.reshape(1, d), w, col_scale.reshape(1, n))


def _ffn_up_kernel(x_ref, g_ref, wg_ref, wu_ref, o_ref, xn_ref):
    @pl.when(pl.program_id(1) == 0)
    def _():
        xn_ref[...] = _rmsnorm_rows(x_ref[...], g_ref[...]).astype(BF16)

    xn = xn_ref[...]
    gate = jnp.dot(xn, wg_ref[...], preferred_element_type=F32)
    up = jnp.dot(xn, wu_ref[...], preferred_element_type=F32)
    o_ref[...] = (gate * jax.nn.sigmoid(gate) * up).astype(o_ref.dtype)


def ffn_up(x, g, w_gate_up, layer, *, tm, tn):
    m, d = x.shape
    d_ff = w_gate_up.shape[-1] // 2
    n_tiles = d_ff // tn
    return pl.pallas_call(
        _ffn_up_kernel,
        grid=(m // tm, n_tiles),
        in_specs=[
            pl.BlockSpec((tm, d), lambda i, j: (i, 0)),
            pl.BlockSpec((1, d), lambda i, j: (0, 0)),
            pl.BlockSpec((None, d, tn), lambda i, j: (layer, 0, j)),
            pl.BlockSpec((None, d, tn), lambda i, j: (layer, 0, j + n_tiles)),
        ],
        out_specs=pl.BlockSpec((tm, tn), lambda i, j: (i, j)),
        out_shape=jax.ShapeDtypeStruct((m, d_ff), BF16),
        scratch_shapes=[pltpu.VMEM((tm, d), BF16)],
        compiler_params=_params(2),
        name="ffn_up",
    )(x, g.reshape(1, d), w_gate_up, w_gate_up)


def _ffn_kernel(h_ref, g_ref, wg_ref, wu_ref, wd_ref, o_ref, xn_ref, *, n_slab):
    @pl.when(pl.program_id(1) == 0)
    def _():
        h = h_ref[...]
        xn_ref[...] = _rmsnorm_rows(h, g_ref[...]).astype(BF16)
        o_ref[...] = h

    xn = xn_ref[...]
    gate = jnp.dot(xn, wg_ref[...].astype(BF16), preferred_element_type=F32)
    up = jnp.dot(xn, wu_ref[...].astype(BF16), preferred_element_type=F32)
    act = (gate * jax.nn.sigmoid(gate) * up).astype(BF16)
    slab = o_ref.shape[1] // n_slab
    for n in range(n_slab):
        cols = slice(n * slab, (n + 1) * slab)
        o_ref[:, cols] += jnp.dot(act, wd_ref[:, cols].astype(BF16), preferred_element_type=F32)


def ffn_block(h, g, w_gate_up, w_down, layer, *, tm, tc):
    m, d = h.shape
    d_ff = w_down.shape[1]
    nc = d_ff // tc
    return pl.pallas_call(
        functools.partial(_ffn_kernel, n_slab=4),
        grid=(m // tm, nc),
        in_specs=[
            pl.BlockSpec((tm, d), lambda i, c: (i, 0)),
            pl.BlockSpec((1, d), lambda i, c: (0, 0)),
            pl.BlockSpec((None, d, tc), lambda i, c: (layer, 0, c)),
            pl.BlockSpec((None, d, tc), lambda i, c: (layer, 0, c + nc)),
            pl.BlockSpec((None, tc, d), lambda i, c: (layer, c, 0)),
        ],
        out_specs=pl.BlockSpec((tm, d), lambda i, c: (i, 0)),
        out_shape=jax.ShapeDtypeStruct((m, d), F32),
        scratch_shapes=[pltpu.VMEM((tm, d), BF16)],
        compiler_params=pltpu.CompilerParams(
            dimension_semantics=("arbitrary", "arbitrary"), vmem_limit_bytes=FFN_VMEM_LIMIT),
        name="ffn_block",
    )(h, g.reshape(1, d), w_gate_up, w_gate_up, w_down)


def _residual_matmul_kernel(h_ref, a_ref, w_ref, o_ref):
    o_ref[...] = h_ref[...] + jnp.dot(a_ref[...], w_ref[...], preferred_element_type=F32)


def residual_matmul(h, a, w, layer, *, tm, tn):
    m, n = h.shape
    k = a.shape[1]
    return pl.pallas_call(
        _residual_matmul_kernel,
        grid=(m // tm, n // tn),
        in_specs=[
            pl.BlockSpec((tm, tn), lambda i, j: (i, j)),
            pl.BlockSpec((tm, k), lambda i, j: (i, 0)),
            pl.BlockSpec((None, k, tn), lambda i, j: (layer, 0, j)),
        ],
        out_specs=pl.BlockSpec((tm, tn), lambda i, j: (i, j)),
        out_shape=jax.ShapeDtypeStruct((m, n), F32),
        compiler_params=_params(2),
        name="residual_matmul",
    )(h, a, w)


def _residual_matmul2_kernel(h_ref, a1_ref, a2_ref, w1_ref, w2_ref, o_ref):
    acc = jnp.dot(a1_ref[...], w1_ref[...], preferred_element_type=F32)
    acc = acc + jnp.dot(a2_ref[...], w2_ref[...], preferred_element_type=F32)
    o_ref[...] = h_ref[...] + acc


def residual_matmul2(h, a1, a2, w, layer, *, tm, tn):
    m, n = h.shape
    k1, k2 = a1.shape[1], a2.shape[1]
    assert k1 % k2 == 0
    return pl.pallas_call(
        _residual_matmul2_kernel,
        grid=(m // tm, n // tn),
        in_specs=[
            pl.BlockSpec((tm, tn), lambda i, j: (i, j)),
            pl.BlockSpec((tm, k1), lambda i, j: (i, 0)),
            pl.BlockSpec((tm, k2), lambda i, j: (i, 0)),
            pl.BlockSpec((None, k1, tn), lambda i, j: (layer, 0, j)),
            pl.BlockSpec((None, k2, tn), lambda i, j: (layer, k1 // k2, j)),
        ],
        out_specs=pl.BlockSpec((tm, tn), lambda i, j: (i, j)),
        out_shape=jax.ShapeDtypeStruct((m, n), F32),
        compiler_params=_params(2),
        name="residual_matmul2",
    )(h, a1, a2, w, w)


def _final_norm_kernel(x_ref, g_ref, o_ref):
    o_ref[...] = _rmsnorm_rows(x_ref[...], g_ref[...])


def final_norm(x, g, *, tm):
    m, d = x.shape
    return pl.pallas_call(
        _final_norm_kernel,
        grid=(m // tm,),
        in_specs=[pl.BlockSpec((tm, d), lambda i: (i, 0)), pl.BlockSpec((1, d), lambda i: (0, 0))],
        out_specs=pl.BlockSpec((tm, d), lambda i: (i, 0)),
        out_shape=jax.ShapeDtypeStruct((m, d), F32),
        compiler_params=_params(1),
        name="final_norm",
    )(x, g.reshape(1, d))


def _rel_bucket_np(rel):
    n = np.maximum(rel, 0)
    max_exact = N_BUCKETS // 2
    nf = np.maximum(n, 1).astype(np.float32)
    large = max_exact + (np.log(nf / np.float32(max_exact)) / np.float32(math.log(MAX_DISTANCE / max_exact))
                         * np.float32(N_BUCKETS - max_exact)).astype(np.int32)
    large = np.minimum(large, N_BUCKETS - 1)
    return np.where(n < max_exact, n, large).astype(np.int32)


def _bucket_tiles():
    qi = np.arange(BLK, dtype=np.int32)[:, None]
    ki = np.arange(BLK, dtype=np.int32)[None, :]
    diag = _rel_bucket_np(qi - ki)
    prev = _rel_bucket_np(qi - ki + BLK)
    assert _rel_bucket_np(np.array([BLK + 1]))[0] == N_BUCKETS - 1
    return jnp.asarray(diag), jnp.asarray(prev)


def _bias_tile(bkt, tab_ref, head):
    out = jnp.zeros(bkt.shape, F32)
    for b in range(N_BUCKETS):
        out = jnp.where(bkt == b, tab_ref[head, b] * LOG2E, out)
    return out


def _causal_tile(tile):
    qi = lax.broadcasted_iota(jnp.int32, tile.shape, 0)
    ki = lax.broadcasted_iota(jnp.int32, tile.shape, 1)
    return jnp.where(ki <= qi, tile, NEG)


def _wide(x):
    return jnp.tile(x, (1, BLK // HEAD_DIM))


def _softmax_step(s, v, m_ref, acc_ref, *, first, shift=None):
    v_aug = jnp.concatenate([v, jnp.ones(v.shape, v.dtype)], axis=1)
    m_cur = jnp.max(s, axis=-1, keepdims=True)
    if shift is not None:
        m_cur = m_cur + shift
    if first:
        m_new = jnp.broadcast_to(m_cur, m_ref.shape)
    else:
        m_old = m_ref[...]
        m_new = jnp.maximum(m_old, m_cur)
    p = jnp.exp2(s - _wide(m_new if shift is None else m_new - shift))
    pv = jnp.dot(p.astype(BF16), v_aug, preferred_element_type=F32)
    if first:
        acc_ref[...] = pv
    else:
        acc_ref[...] = _wide(jnp.exp2(m_old - m_new)) * acc_ref[...] + pv
    m_ref[...] = m_new


def _softmax_result(acc_ref):
    acc = acc_ref[...]
    return acc[:, :HEAD_DIM] / acc[:, HEAD_DIM:]


def _head_cols(hh):
    return slice(hh * HEAD_DIM, (hh + 1) * HEAD_DIM)


def _moba_kernel(tab_ref, bd_ref, bp_ref, q_ref, k_ref, v_ref, o_ref,
                 bias_d_ref, bias_p_ref, kmean_ref, qa_ref, m_ref, acc_ref, *, gate_fill, heads):
    hg, bg, qb = pl.program_id(0), pl.program_id(1), pl.program_id(2)
    group, seq, _ = k_ref.shape
    nb = seq // BLK
    chains = [(hh, g) for hh in range(heads) for g in range(group)]

    @pl.when((bg == 0) & (qb == 0))
    def _():
        for hh in range(heads):
            h = hg * heads + hh
            bias_d_ref[hh] = _causal_tile(_bias_tile(bd_ref[...], tab_ref, h))
            bias_p_ref[hh] = _bias_tile(bp_ref[...], tab_ref, h)

    @pl.when(qb == 0)
    def _():
        for c, (hh, g) in enumerate(chains):
            kf = k_ref[g, :, _head_cols(hh)].astype(F32).reshape(nb, BLK, HEAD_DIM)
            kmean_ref[c] = jnp.mean(kf, axis=1)

    row = lax.broadcasted_iota(jnp.int32, (nb, BLK), 0)
    past = row < qb
    n_sel = max(1, min(MOBA_TOPK, nb - 1))
    for c, (hh, g) in enumerate(chains):
        q = q_ref[g, :, _head_cols(hh)]
        gate = lax.dot_general(kmean_ref[c], q.astype(F32), NT_DIMS,
                               precision=lax.Precision.HIGHEST, preferred_element_type=F32)
        gate = jnp.where(past, gate, gate_fill)
        beaten_by = jnp.zeros(gate.shape, F32)
        for i in range(nb):
            gi = gate[i:i + 1, :]
            wins = (gi > gate) | ((gi == gate) & (i < row))
            beaten_by = beaten_by + jnp.where(wins, 1.0, 0.0)
        keep = (beaten_by < n_sel) & past
        sel_neg = jnp.where(keep, 0.0, NEG)
        sel_neg = jnp.concatenate([sel_neg, jnp.zeros((HEAD_DIM - nb, BLK), F32)], axis=0)
        qa_ref[c] = jnp.concatenate([q, sel_neg.T.astype(BF16)], axis=1)

    lane = lax.broadcasted_iota(jnp.int32, (BLK, HEAD_DIM), 1)

    def block_scores(c, j):
        hh, g = chains[c]
        start = pl.multiple_of(j * BLK, BLK)
        kj = k_ref[g, pl.ds(start, BLK), _head_cols(hh)]
        onehot = jnp.where(lane == j, 1.0, 0.0).astype(BF16)
        k_aug = jnp.concatenate([kj, onehot], axis=1)
        s = lax.dot_general(qa_ref[c], k_aug, NT_DIMS, preferred_element_type=F32)
        return s, v_ref[g, pl.ds(start, BLK), _head_cols(hh)]

    start = pl.multiple_of(qb * BLK, BLK)
    for c, (hh, g) in enumerate(chains):
        s = lax.dot_general(q_ref[g, :, _head_cols(hh)], k_ref[g, pl.ds(start, BLK), _head_cols(hh)],
                            NT_DIMS, preferred_element_type=F32)
        _softmax_step(s + bias_d_ref[hh], v_ref[g, pl.ds(start, BLK), _head_cols(hh)],
                      m_ref.at[c], acc_ref.at[c], first=True)

    @pl.when(qb >= 1)
    def _():
        for c, (hh, g) in enumerate(chains):
            s, v = block_scores(c, qb - 1)
            _softmax_step(s + bias_p_ref[hh], v, m_ref.at[c], acc_ref.at[c], first=False)

    far_bias = [tab_ref[hg * heads + hh, N_BUCKETS - 1] * LOG2E for hh in range(heads)]

    def far_body(d, carry):
        for c, (hh, g) in enumerate(chains):
            s, v = block_scores(c, qb - d)
            _softmax_step(s, v, m_ref.at[c], acc_ref.at[c], first=False, shift=far_bias[hh])
        return carry

    lax.fori_loop(2, qb + 1, far_body, 0)

    for c, (hh, g) in enumerate(chains):
        o_ref[g, :, _head_cols(hh)] = _softmax_result(acc_ref.at[c]).astype(o_ref.dtype)


def moba_attention(proj, tab, bkt_diag, bkt_prev, *, group, heads, gate_fill):
    batch, seq, _ = proj.shape
    nqb = seq // BLK
    k_off = N_SELF_HEADS // heads
    width = heads * HEAD_DIM
    chains = heads * group
    kern = functools.partial(_moba_kernel, gate_fill=gate_fill, heads=heads)
    return pl.pallas_call(
        kern,
        grid=(N_SELF_HEADS // heads, batch // group, nqb),
        in_specs=[
            pl.BlockSpec(memory_space=pltpu.SMEM),
            pl.BlockSpec((BLK, BLK), lambda h, b, qb: (0, 0)),
            pl.BlockSpec((BLK, BLK), lambda h, b, qb: (0, 0)),
            pl.BlockSpec((group, BLK, width), lambda h, b, qb: (b, qb, h)),
            pl.BlockSpec((group, seq, width), lambda h, b, qb: (b, 0, k_off + h)),
            pl.BlockSpec((group, seq, width), lambda h, b, qb: (b, 0, 2 * k_off + h)),
        ],
        out_specs=pl.BlockSpec((group, BLK, width), lambda h, b, qb: (b, qb, h)),
        out_shape=jax.ShapeDtypeStruct((batch, seq, SELF_WIDTH), BF16),
        scratch_shapes=[
            pltpu.VMEM((heads, BLK, BLK), F32), pltpu.VMEM((heads, BLK, BLK), F32),
            pltpu.VMEM((chains, seq // BLK, HEAD_DIM), F32),
            pltpu.VMEM((chains, BLK, 2 * HEAD_DIM), BF16),
            pltpu.VMEM((chains, BLK, HEAD_DIM), F32), pltpu.VMEM((chains, BLK, 2 * HEAD_DIM), F32),
        ],
        compiler_params=_params(3),
        name="moba_attention",
    )(tab, bkt_diag, bkt_prev, proj, proj, proj)


def _diff_kernel(tab_ref, bd_ref, bp_ref, lq_ref, gs_ref, q_ref, k_ref, v_ref, o_ref,
                 bias_d_ref, bias_p_ref, q2_ref, m_ref, acc_ref, *, lam_init, heads):
    hg, bg, qb = pl.program_id(0), pl.program_id(1), pl.program_id(2)
    group = k_ref.shape[0]
    chains = [(hh, g) for hh in range(heads) for g in range(group)]

    @pl.when((bg == 0) & (qb == 0))
    def _():
        for hh in range(heads):
            h = hg * heads + hh
            bd = _causal_tile(_bias_tile(bd_ref[...], tab_ref, h))
            bp = _bias_tile(bp_ref[...], tab_ref, h)
            bias_d_ref[hh] = jnp.concatenate([bd, bd], axis=0)
            bias_p_ref[hh] = jnp.concatenate([bp, bp], axis=0)

    lane = lax.broadcasted_iota(jnp.int32, (BLK, HEAD_DIM), 1)
    for c, (hh, g) in enumerate(chains):
        q = q_ref[g, :, _head_cols(hh)]
        zero = jnp.zeros_like(q)
        q2_ref[c] = jnp.concatenate([jnp.where(lane < DIFF_HALF, q, zero),
                                     jnp.where(lane >= DIFF_HALF, q, zero)], axis=0)

    def block_scores(c, j):
        hh, g = chains[c]
        start = pl.multiple_of(j * BLK, BLK)
        s = lax.dot_general(q2_ref[c], k_ref[g, pl.ds(start, BLK), _head_cols(hh)], NT_DIMS,
                            preferred_element_type=F32)
        return s, v_ref[g, pl.ds(start, BLK), _head_cols(hh)]

    for c, (hh, g) in enumerate(chains):
        s, v = block_scores(c, qb)
        _softmax_step(s + bias_d_ref[hh], v, m_ref.at[c], acc_ref.at[c], first=True)

    @pl.when(qb >= 1)
    def _():
        for c, (hh, g) in enumerate(chains):
            s, v = block_scores(c, qb - 1)
            _softmax_step(s + bias_p_ref[hh], v, m_ref.at[c], acc_ref.at[c], first=False)

    far_bias = [tab_ref[hg * heads + hh, N_BUCKETS - 1] * LOG2E for hh in range(heads)]

    def far_body(d, carry):
        for c, (hh, g) in enumerate(chains):
            s, v = block_scores(c, qb - d)
            _softmax_step(s, v, m_ref.at[c], acc_ref.at[c], first=False, shift=far_bias[hh])
        return carry

    lax.fori_loop(2, qb + 1, far_body, 0)

    lq = lq_ref[...]
    lam = (jnp.exp(jnp.sum(lq[0:1] * lq[1:2], axis=-1, keepdims=True))
           - jnp.exp(jnp.sum(lq[2:3] * lq[3:4], axis=-1, keepdims=True)) + lam_init)
    for c, (hh, g) in enumerate(chains):
        o_all = _softmax_result(acc_ref.at[c])
        o = o_all[:BLK] - lam * o_all[BLK:]
        o_ref[g, :, _head_cols(hh)] = (
            _rmsnorm_rows(o, gs_ref[...]) * (1.0 - lam_init)).astype(o_ref.dtype)


def diff_attention(proj, kv, tab, bkt_diag, bkt_prev, lq, g_subln, *, group, heads, lam_init):
    batch, seq, _ = proj.shape
    nqb = seq // BLK
    v_off = N_SELF_HEADS // heads
    width = heads * HEAD_DIM
    chains = heads * group
    kern = functools.partial(_diff_kernel, lam_init=lam_init, heads=heads)
    return pl.pallas_call(
        kern,
        grid=(N_SELF_HEADS // heads, batch // group, nqb),
        in_specs=[
            pl.BlockSpec(memory_space=pltpu.SMEM),
            pl.BlockSpec((BLK, BLK), lambda h, b, qb: (0, 0)),
            pl.BlockSpec((BLK, BLK), lambda h, b, qb: (0, 0)),
            pl.BlockSpec((4, DIFF_HALF), lambda h, b, qb: (0, 0)),
            pl.BlockSpec((1, HEAD_DIM), lambda h, b, qb: (0, 0)),
            pl.BlockSpec((group, BLK, width), lambda h, b, qb: (b, qb, h)),
            pl.BlockSpec((group, seq, width), lambda h, b, qb: (b, 0, h)),
            pl.BlockSpec((group, seq, width), lambda h, b, qb: (b, 0, v_off + h)),
        ],
        out_specs=pl.BlockSpec((group, BLK, width), lambda h, b, qb: (b, qb, h)),
        out_shape=jax.ShapeDtypeStruct((batch, seq, SELF_WIDTH), BF16),
        scratch_shapes=[
            pltpu.VMEM((heads, 2 * BLK, BLK), F32), pltpu.VMEM((heads, 2 * BLK, BLK), F32),
            pltpu.VMEM((chains, 2 * BLK, HEAD_DIM), BF16),
            pltpu.VMEM((chains, 2 * BLK, HEAD_DIM), F32),
            pltpu.VMEM((chains, 2 * BLK, 2 * HEAD_DIM), F32),
        ],
        compiler_params=_params(3),
        name="diff_attention",
    )(tab, bkt_diag, bkt_prev, lq, g_subln.reshape(1, HEAD_DIM), proj, kv, kv)


def _mem_kernel(q_ref, mk_ref, mv_ref, o_ref):
    for hh in range(N_MEM_HEADS):
        cols = _head_cols(hh)
        s = lax.dot_general(q_ref[:, cols], mk_ref[:, cols], NT_DIMS, preferred_element_type=F32)
        p = jnp.exp2(s - jnp.max(s, axis=-1, keepdims=True))
        mv = mv_ref[:, cols]
        pv = jnp.dot(p.astype(BF16), jnp.concatenate([mv, jnp.ones(mv.shape, mv.dtype)], axis=1),
                     preferred_element_type=F32)
        o_ref[:, cols] = (pv[:, :HEAD_DIM] / pv[:, HEAD_DIM:]).astype(o_ref.dtype)


def mem_attention(proj, mkv, *, batch, seq, n_mem, q_col, tq):
    nsb = seq // tq
    q_blk = q_col // MEM_WIDTH
    return pl.pallas_call(
        _mem_kernel,
        grid=(batch, nsb),
        in_specs=[
            pl.BlockSpec((tq, MEM_WIDTH), lambda b, sb: (b * nsb + sb, q_blk)),
            pl.BlockSpec((n_mem, MEM_WIDTH), lambda b, sb: (b, 0)),
            pl.BlockSpec((n_mem, MEM_WIDTH), lambda b, sb: (b, 1)),
        ],
        out_specs=pl.BlockSpec((tq, MEM_WIDTH), lambda b, sb: (b * nsb + sb, 0)),
        out_shape=jax.ShapeDtypeStruct((batch * seq, MEM_WIDTH), BF16),
        compiler_params=_params(2),
        name="mem_attention",
    )(proj, mkv, mkv)


ATTN_GROUP = 4
ATTN_HEADS = 2


def kernel(x, mem, rel_bias, g_mix, w_in_a, w_in_b, g_mem, w_mem_kv, w_o, g_ffn, w_gate_up, w_down,
           g_kv, w_kv_shared, lambda_qk, g_subln, g_final):
    batch, seq, d = x.shape
    n_mem = mem.shape[1]
    depth = g_mix.shape[0]
    n_a = w_in_a.shape[0]
    tokens = batch * seq
    group = math.gcd(batch, ATTN_GROUP)

    w_in_a, w_in_b, w_mem_kv, w_o = (w.astype(BF16) for w in (w_in_a, w_in_b, w_mem_kv, w_o))
    w_kv_shared = w_kv_shared.astype(BF16)[None]

    tab = rel_bias.T.astype(F32)
    bkt_diag, bkt_prev = _bucket_tiles()

    moba_scale = HEAD_DIM ** -0.5 * LOG2E
    diff_scale = DIFF_HALF ** -0.5 * LOG2E
    mem_scale = HEAD_DIM ** -0.5 * LOG2E
    ones = functools.partial(jnp.ones, dtype=F32)
    cs_a = jnp.concatenate([jnp.full((SELF_WIDTH,), moba_scale, F32), ones((2 * SELF_WIDTH,)),
                            jnp.full((MEM_WIDTH,), mem_scale, F32)])
    cs_b = jnp.concatenate([jnp.full((SELF_WIDTH,), diff_scale, F32),
                            jnp.full((MEM_WIDTH,), mem_scale, F32)])

    h = x.reshape(tokens, d)
    mem2 = mem.reshape(batch * n_mem, d)
    kv = None
    for l in range(depth):
        mkv = norm_matmul(mem2, g_mem[l], w_mem_kv, l, ones((2 * MEM_WIDTH,)),
                          tm=batch * n_mem, tn=512)
        if l < n_a:
            proj = norm_matmul(h, g_mix[l], w_in_a, l, cs_a, tm=1024, tn=1024)
            y_self = moba_attention(proj.reshape(batch, seq, -1), tab, bkt_diag, bkt_prev,
                                    group=group, heads=ATTN_HEADS, gate_fill=NEG * moba_scale)
            q_col = 3 * SELF_WIDTH
        else:
            j = l - n_a
            if kv is None:
                kv = norm_matmul(h, g_kv, w_kv_shared, 0, ones((2 * SELF_WIDTH,)), tm=1024, tn=1024)
                kv = kv.reshape(batch, seq, -1)
            proj = norm_matmul(h, g_mix[l], w_in_b, j, cs_b, tm=1024, tn=1024)
            lam_init = 0.8 - 0.6 * math.exp(-0.3 * l)
            y_self = diff_attention(proj.reshape(batch, seq, -1), kv, tab, bkt_diag, bkt_prev,
                                    lambda_qk[j], g_subln[j], group=group, heads=ATTN_HEADS,
                                    lam_init=lam_init)
            q_col = SELF_WIDTH
        y_mem = mem_attention(proj, mkv, batch=batch, seq=seq, n_mem=n_mem,
                              q_col=q_col, tq=1024)
        h = residual_matmul2(h, y_self.reshape(tokens, -1), y_mem, w_o, l, tm=2048, tn=512)
        h = ffn_block(h, g_ffn[l], w_gate_up, w_down, l, tm=1024, tc=256)
    return final_norm(h, g_final, tm=512).reshape(batch, seq, d)
```

```python
import functools
import math

import jax
import jax.numpy as jnp
import numpy as np
from jax import lax
from jax.experimental import pallas as pl
from jax.experimental.pallas import tpu as pltpu

D_MODEL = 2048
HEAD_DIM = 128
N_MEM_HEADS = 4
MEM_WIDTH = N_MEM_HEADS * HEAD_DIM
SELF_WIDTH = D_MODEL - MEM_WIDTH
N_SELF_HEADS = SELF_WIDTH // HEAD_DIM
DIFF_HALF = HEAD_DIM // 2
MOBA_BLOCK = 256
MOBA_TOPK = 3
N_BUCKETS = 32
MAX_DISTANCE = 128
RMS_EPS = 1e-6
NEG = -1e30
LOG2E = math.log2(math.e)

BLK = MOBA_BLOCK
V7X_VMEM_BYTES = 64 * 1024 * 1024
VMEM_LIMIT = 56 * 1024 * 1024

FFN_VMEM_LIMIT = 60 * 1024 * 1024

F32 = jnp.float32
BF16 = jnp.bfloat16
NT_DIMS = (((1,), (1,)), ((), ()))


def _params(n_grid_axes):
    return pltpu.CompilerParams(
        dimension_semantics=("arbitrary",) * n_grid_axes, vmem_limit_bytes=VMEM_LIMIT)


def _rmsnorm_rows(x, g):
    ms = jnp.mean(x * x, axis=-1, keepdims=True)
    return x * lax.rsqrt(ms + RMS_EPS) * g


def _norm_matmul_kernel(x_ref, g_ref, w_ref, cs_ref, o_ref, xn_ref):
    @pl.when(pl.program_id(1) == 0)
    def _():
        xn_ref[...] = _rmsnorm_rows(x_ref[...], g_ref[...]).astype(BF16)

    acc = jnp.dot(xn_ref[...], w_ref[...].astype(BF16), preferred_element_type=F32)
    o_ref[...] = (acc * cs_ref[...]).astype(o_ref.dtype)


def norm_matmul(x, g, w, layer, col_scale, *, tm, tn):
    m, d = x.shape
    n = w.shape[-1]
    return pl.pallas_call(
        _norm_matmul_kernel,
        grid=(m // tm, n // tn),
        in_specs=[
            pl.BlockSpec((tm, d), lambda i, j: (i, 0)),
            pl.BlockSpec((1, d), lambda i, j: (0, 0)),
            pl.BlockSpec((None, d, tn), lambda i, j: (layer, 0, j)),
            pl.BlockSpec((1, tn), lambda i, j: (0, j)),
        ],
        out_specs=pl.BlockSpec((tm, tn), lambda i, j: (i, j)),
        out_shape=jax.ShapeDtypeStruct((m, n), BF16),
        scratch_shapes=[pltpu.VMEM((tm, d), BF16)],
        compiler_params=_params(2),
        name="norm_matmul",
    )(x, g.reshape(1, d), w, col_scale.reshape(1, n))


def _ffn_kernel(h_ref, g_ref, gf_ref, wg_ref, wu_ref, wd_ref, o_ref, xn_ref, *, n_slab, final_norm):
    c = pl.program_id(1)

    @pl.when(c == 0)
    def _():
        h = h_ref[...]
        xn_ref[...] = _rmsnorm_rows(h, g_ref[...]).astype(BF16)
        o_ref[...] = h

    xn = xn_ref[...]
    gate = jnp.dot(xn, wg_ref[...].astype(BF16), preferred_element_type=F32)
    up = jnp.dot(xn, wu_ref[...].astype(BF16), preferred_element_type=F32)
    act = (gate * jax.nn.sigmoid(gate) * up).astype(BF16)
    slab = o_ref.shape[1] // n_slab
    for n in range(n_slab):
        cols = slice(n * slab, (n + 1) * slab)
        o_ref[:, cols] += jnp.dot(act, wd_ref[:, cols].astype(BF16), preferred_element_type=F32)

    if final_norm:
        @pl.when(c == pl.num_programs(1) - 1)
        def _():
            o_ref[...] = _rmsnorm_rows(o_ref[...], gf_ref[...])


def ffn_block(h, g, w_gate_up, w_down, layer, *, tm, tc, final_gain=None):
    m, d = h.shape
    d_ff = w_down.shape[1]
    nc = d_ff // tc
    final_norm = final_gain is not None
    gf = final_gain if final_norm else g
    return pl.pallas_call(
        functools.partial(_ffn_kernel, n_slab=4, final_norm=final_norm),
        grid=(m // tm, nc),
        in_specs=[
            pl.BlockSpec((tm, d), lambda i, c: (i, 0)),
            pl.BlockSpec((1, d), lambda i, c: (0, 0)),
            pl.BlockSpec((1, d), lambda i, c: (0, 0)),
            pl.BlockSpec((None, d, tc), lambda i, c: (layer, 0, c)),
            pl.BlockSpec((None, d, tc), lambda i, c: (layer, 0, c + nc)),
            pl.BlockSpec((None, tc, d), lambda i, c: (layer, c, 0)),
        ],
        out_specs=pl.BlockSpec((tm, d), lambda i, c: (i, 0)),
        out_shape=jax.ShapeDtypeStruct((m, d), F32),
        scratch_shapes=[pltpu.VMEM((tm, d), BF16)],
        compiler_params=pltpu.CompilerParams(
            dimension_semantics=("arbitrary", "arbitrary"), vmem_limit_bytes=FFN_VMEM_LIMIT),
        name="ffn_block",
    )(h, g.reshape(1, d), gf.reshape(1, d), w_gate_up, w_gate_up, w_down)


def _residual_matmul2_kernel(h_ref, a1_ref, a2_ref, w1_ref, w2_ref, o_ref):
    acc = jnp.dot(a1_ref[...], w1_ref[...].astype(BF16), preferred_element_type=F32)
    acc = acc + jnp.dot(a2_ref[...], w2_ref[...].astype(BF16), preferred_element_type=F32)
    o_ref[...] = h_ref[...] + acc


def residual_matmul2(h, a1, a2, w, layer, *, tm, tn):
    m, n = h.shape
    k1, k2 = a1.shape[1], a2.shape[1]
    assert k1 % k2 == 0
    return pl.pallas_call(
        _residual_matmul2_kernel,
        grid=(m // tm, n // tn),
        in_specs=[
            pl.BlockSpec((tm, tn), lambda i, j: (i, j)),
            pl.BlockSpec((tm, k1), lambda i, j: (i, 0)),
            pl.BlockSpec((tm, k2), lambda i, j: (i, 0)),
            pl.BlockSpec((None, k1, tn), lambda i, j: (layer, 0, j)),
            pl.BlockSpec((None, k2, tn), lambda i, j: (layer, k1 // k2, j)),
        ],
        out_specs=pl.BlockSpec((tm, tn), lambda i, j: (i, j)),
        out_shape=jax.ShapeDtypeStruct((m, n), F32),
        compiler_params=_params(2),
        name="residual_matmul2",
    )(h, a1, a2, w, w)


def _rel_bucket_np(rel):
    n = np.maximum(rel, 0)
    max_exact = N_BUCKETS // 2
    nf = np.maximum(n, 1).astype(np.float32)
    large = max_exact + (np.log(nf / np.float32(max_exact)) / np.float32(math.log(MAX_DISTANCE / max_exact))
                         * np.float32(N_BUCKETS - max_exact)).astype(np.int32)
    large = np.minimum(large, N_BUCKETS - 1)
    return np.where(n < max_exact, n, large).astype(np.int32)


def _bucket_tiles():
    qi = np.arange(BLK, dtype=np.int32)[:, None]
    ki = np.arange(BLK, dtype=np.int32)[None, :]
    diag = _rel_bucket_np(qi - ki)
    prev = _rel_bucket_np(qi - ki + BLK)
    assert _rel_bucket_np(np.array([BLK + 1]))[0] == N_BUCKETS - 1
    return jnp.asarray(diag), jnp.asarray(prev)


def _bias_tile(bkt, tab_ref, head):
    out = jnp.zeros(bkt.shape, F32)
    for b in range(N_BUCKETS):
        out = jnp.where(bkt == b, tab_ref[head, b] * LOG2E, out)
    return out


def _causal_tile(tile):
    qi = lax.broadcasted_iota(jnp.int32, tile.shape, 0)
    ki = lax.broadcasted_iota(jnp.int32, tile.shape, 1)
    return jnp.where(ki <= qi, tile, NEG)


def _wide(x):
    return jnp.tile(x, (1, BLK // HEAD_DIM))


def _softmax_step(s, v, m_ref, acc_ref, *, first, shift=None):
    v_aug = jnp.concatenate([v, jnp.ones(v.shape, v.dtype)], axis=1)
    m_cur = jnp.max(s, axis=-1, keepdims=True)
    if shift is not None:
        m_cur = m_cur + shift
    if first:
        m_new = jnp.broadcast_to(m_cur, m_ref.shape)
    else:
        m_old = m_ref[...]
        m_new = jnp.maximum(m_old, m_cur)
    p = jnp.exp2(s - _wide(m_new if shift is None else m_new - shift))
    pv = jnp.dot(p.astype(BF16), v_aug, preferred_element_type=F32)
    if first:
        acc_ref[...] = pv
    else:
        acc_ref[...] = _wide(jnp.exp2(m_old - m_new)) * acc_ref[...] + pv
    m_ref[...] = m_new


def _softmax_result(acc_ref):
    acc = acc_ref[...]
    return acc[:, :HEAD_DIM] / acc[:, HEAD_DIM:]


def _head_cols(hh):
    return slice(hh * HEAD_DIM, (hh + 1) * HEAD_DIM)


def _moba_kernel(tab_ref, bd_ref, bp_ref, q_ref, k_ref, v_ref, o_ref,
                 bias_d_ref, bias_p_ref, kmean_ref, qa_ref, m_ref, acc_ref, *, gate_fill, heads):
    hg, bg, qb = pl.program_id(0), pl.program_id(1), pl.program_id(2)
    group, seq, _ = k_ref.shape
    nb = seq // BLK
    chains = [(hh, g) for hh in range(heads) for g in range(group)]

    @pl.when((bg == 0) & (qb == 0))
    def _():
        for hh in range(heads):
            h = hg * heads + hh
            bias_d_ref[hh] = _causal_tile(_bias_tile(bd_ref[...], tab_ref, h))
            bias_p_ref[hh] = _bias_tile(bp_ref[...], tab_ref, h)

    @pl.when(qb == 0)
    def _():
        for c, (hh, g) in enumerate(chains):
            kf = k_ref[g, :, _head_cols(hh)].astype(F32).reshape(nb, BLK, HEAD_DIM)
            kmean_ref[c] = jnp.mean(kf, axis=1)

    row = lax.broadcasted_iota(jnp.int32, (nb, BLK), 0)
    past = row < qb
    n_sel = max(1, min(MOBA_TOPK, nb - 1))
    for c, (hh, g) in enumerate(chains):
        q = q_ref[g, :, _head_cols(hh)]
        gate = lax.dot_general(kmean_ref[c], q.astype(F32), NT_DIMS,
                               precision=lax.Precision.HIGHEST, preferred_element_type=F32)
        gate = jnp.where(past, gate, gate_fill)
        beaten_by = jnp.zeros(gate.shape, F32)
        for i in range(nb):
            gi = gate[i:i + 1, :]
            wins = (gi > gate) | ((gi == gate) & (i < row))
            beaten_by = beaten_by + jnp.where(wins, 1.0, 0.0)
        keep = (beaten_by < n_sel) & past
        sel_neg = jnp.where(keep, 0.0, NEG)
        sel_neg = jnp.concatenate([sel_neg, jnp.zeros((HEAD_DIM - nb, BLK), F32)], axis=0)
        qa_ref[c] = jnp.concatenate([q, sel_neg.T.astype(BF16)], axis=1)

    lane = lax.broadcasted_iota(jnp.int32, (BLK, HEAD_DIM), 1)

    def block_scores(c, j):
        hh, g = chains[c]
        start = pl.multiple_of(j * BLK, BLK)
        kj = k_ref[g, pl.ds(start, BLK), _head_cols(hh)]
        onehot = jnp.where(lane == j, 1.0, 0.0).astype(BF16)
        k_aug = jnp.concatenate([kj, onehot], axis=1)
        s = lax.dot_general(qa_ref[c], k_aug, NT_DIMS, preferred_element_type=F32)
        return s, v_ref[g, pl.ds(start, BLK), _head_cols(hh)]

    start = pl.multiple_of(qb * BLK, BLK)
    for c, (hh, g) in enumerate(chains):
        s = lax.dot_general(q_ref[g, :, _head_cols(hh)], k_ref[g, pl.ds(start, BLK), _head_cols(hh)],
                            NT_DIMS, preferred_element_type=F32)
        _softmax_step(s + bias_d_ref[hh], v_ref[g, pl.ds(start, BLK), _head_cols(hh)],
                      m_ref.at[c], acc_ref.at[c], first=True)

    @pl.when(qb >= 1)
    def _():
        for c, (hh, g) in enumerate(chains):
            s, v = block_scores(c, qb - 1)
            _softmax_step(s + bias_p_ref[hh], v, m_ref.at[c], acc_ref.at[c], first=False)

    far_bias = [tab_ref[hg * heads + hh, N_BUCKETS - 1] * LOG2E for hh in range(heads)]

    def far_body(d, carry):
        for c, (hh, g) in enumerate(chains):
            s, v = block_scores(c, qb - d)
            _softmax_step(s, v, m_ref.at[c], acc_ref.at[c], first=False, shift=far_bias[hh])
        return carry

    lax.fori_loop(2, qb + 1, far_body, 0)

    for c, (hh, g) in enumerate(chains):
        o_ref[g, :, _head_cols(hh)] = _softmax_result(acc_ref.at[c]).astype(o_ref.dtype)


def moba_attention(proj, tab, bkt_diag, bkt_prev, *, group, heads, gate_fill):
    batch, seq, _ = proj.shape
    nqb = seq // BLK
    k_off = N_SELF_HEADS // heads
    width = heads * HEAD_DIM
    chains = heads * group
    kern = functools.partial(_moba_kernel, gate_fill=gate_fill, heads=heads)
    return pl.pallas_call(
        kern,
        grid=(N_SELF_HEADS // heads, batch // group, nqb),
        in_specs=[
            pl.BlockSpec(memory_space=pltpu.SMEM),
            pl.BlockSpec((BLK, BLK), lambda h, b, qb: (0, 0)),
            pl.BlockSpec((BLK, BLK), lambda h, b, qb: (0, 0)),
            pl.BlockSpec((group, BLK, width), lambda h, b, qb: (b, qb, h)),
            pl.BlockSpec((group, seq, width), lambda h, b, qb: (b, 0, k_off + h)),
            pl.BlockSpec((group, seq, width), lambda h, b, qb: (b, 0, 2 * k_off + h)),
        ],
        out_specs=pl.BlockSpec((group, BLK, width), lambda h, b, qb: (b, qb, h)),
        out_shape=jax.ShapeDtypeStruct((batch, seq, SELF_WIDTH), BF16),
        scratch_shapes=[
            pltpu.VMEM((heads, BLK, BLK), F32), pltpu.VMEM((heads, BLK, BLK), F32),
            pltpu.VMEM((chains, seq // BLK, HEAD_DIM), F32),
            pltpu.VMEM((chains, BLK, 2 * HEAD_DIM), BF16),
            pltpu.VMEM((chains, BLK, HEAD_DIM), F32), pltpu.VMEM((chains, BLK, 2 * HEAD_DIM), F32),
        ],
        compiler_params=_params(3),
        name="moba_attention",
    )(tab, bkt_diag, bkt_prev, proj, proj, proj)


def _diff_kernel(tab_ref, bd_ref, bp_ref, lq_ref, gs_ref, q_ref, k_ref, v_ref, o_ref,
                 bias_d_ref, bias_p_ref, q2_ref, m_ref, acc_ref, *, lam_init, heads):
    hg, bg, qb = pl.program_id(0), pl.program_id(1), pl.program_id(2)
    group = k_ref.shape[0]
    chains = [(hh, g) for hh in range(heads) for g in range(group)]

    @pl.when((bg == 0) & (qb == 0))
    def _():
        for hh in range(heads):
            h = hg * heads + hh
            bd = _causal_tile(_bias_tile(bd_ref[...], tab_ref, h))
            bp = _bias_tile(bp_ref[...], tab_ref, h)
            bias_d_ref[hh] = jnp.concatenate([bd, bd], axis=0)
            bias_p_ref[hh] = jnp.concatenate([bp, bp], axis=0)

    lane = lax.broadcasted_iota(jnp.int32, (BLK, HEAD_DIM), 1)
    for c, (hh, g) in enumerate(chains):
        q = q_ref[g, :, _head_cols(hh)]
        zero = jnp.zeros_like(q)
        q2_ref[c] = jnp.concatenate([jnp.where(lane < DIFF_HALF, q, zero),
                                     jnp.where(lane >= DIFF_HALF, q, zero)], axis=0)

    def block_scores(c, j):
        hh, g = chains[c]
        start = pl.multiple_of(j * BLK, BLK)
        s = lax.dot_general(q2_ref[c], k_ref[g, pl.ds(start, BLK), _head_cols(hh)], NT_DIMS,
                            preferred_element_type=F32)
        return s, v_ref[g, pl.ds(start, BLK), _head_cols(hh)]

    for c, (hh, g) in enumerate(chains):
        s, v = block_scores(c, qb)
        _softmax_step(s + bias_d_ref[hh], v, m_ref.at[c], acc_ref.at[c], first=True)

    @pl.when(qb >= 1)
    def _():
        for c, (hh, g) in enumerate(chains):
            s, v = block_scores(c, qb - 1)
            _softmax_step(s + bias_p_ref[hh], v, m_ref.at[c], acc_ref.at[c], first=False)

    far_bias = [tab_ref[hg * heads + hh, N_BUCKETS - 1] * LOG2E for hh in range(heads)]

    def far_body(d, carry):
        for c, (hh, g) in enumerate(chains):
            s, v = block_scores(c, qb - d)
            _softmax_step(s, v, m_ref.at[c], acc_ref.at[c], first=False, shift=far_bias[hh])
        return carry

    lax.fori_loop(2, qb + 1, far_body, 0)

    lq = lq_ref[...]
    lam = (jnp.exp(jnp.sum(lq[0:1] * lq[1:2], axis=-1, keepdims=True))
           - jnp.exp(jnp.sum(lq[2:3] * lq[3:4], axis=-1, keepdims=True)) + lam_init)
    for c, (hh, g) in enumerate(chains):
        o_all = _softmax_result(acc_ref.at[c])
        o = o_all[:BLK] - lam * o_all[BLK:]
        o_ref[g, :, _head_cols(hh)] = (
            _rmsnorm_rows(o, gs_ref[...]) * (1.0 - lam_init)).astype(o_ref.dtype)


def diff_attention(proj, kv, tab, bkt_diag, bkt_prev, lq, g_subln, *, group, heads, lam_init):
    batch, seq, _ = proj.shape
    nqb = seq // BLK
    v_off = N_SELF_HEADS // heads
    width = heads * HEAD_DIM
    chains = heads * group
    kern = functools.partial(_diff_kernel, lam_init=lam_init, heads=heads)
    return pl.pallas_call(
        kern,
        grid=(N_SELF_HEADS // heads, batch // group, nqb),
        in_specs=[
            pl.BlockSpec(memory_space=pltpu.SMEM),
            pl.BlockSpec((BLK, BLK), lambda h, b, qb: (0, 0)),
            pl.BlockSpec((BLK, BLK), lambda h, b, qb: (0, 0)),
            pl.BlockSpec((4, DIFF_HALF), lambda h, b, qb: (0, 0)),
            pl.BlockSpec((1, HEAD_DIM), lambda h, b, qb: (0, 0)),
            pl.BlockSpec((group, BLK, width), lambda h, b, qb: (b, qb, h)),
            pl.BlockSpec((group, seq, width), lambda h, b, qb: (b, 0, h)),
            pl.BlockSpec((group, seq, width), lambda h, b, qb: (b, 0, v_off + h)),
        ],
        out_specs=pl.BlockSpec((group, BLK, width), lambda h, b, qb: (b, qb, h)),
        out_shape=jax.ShapeDtypeStruct((batch, seq, SELF_WIDTH), BF16),
        scratch_shapes=[
            pltpu.VMEM((heads, 2 * BLK, BLK), F32), pltpu.VMEM((heads, 2 * BLK, BLK), F32),
            pltpu.VMEM((chains, 2 * BLK, HEAD_DIM), BF16),
            pltpu.VMEM((chains, 2 * BLK, HEAD_DIM), F32),
            pltpu.VMEM((chains, 2 * BLK, 2 * HEAD_DIM), F32),
        ],
        compiler_params=_params(3),
        name="diff_attention",
    )(tab, bkt_diag, bkt_prev, lq, g_subln.reshape(1, HEAD_DIM), proj, kv, kv)


def _mem_kernel(q_ref, mk_ref, mv_ref, o_ref):
    for hh in range(N_MEM_HEADS):
        cols = _head_cols(hh)
        s = lax.dot_general(q_ref[:, cols], mk_ref[:, cols], NT_DIMS, preferred_element_type=F32)
        p = jnp.exp2(s - jnp.max(s, axis=-1, keepdims=True))
        mv = mv_ref[:, cols]
        pv = jnp.dot(p.astype(BF16), jnp.concatenate([mv, jnp.ones(mv.shape, mv.dtype)], axis=1),
                     preferred_element_type=F32)
        o_ref[:, cols] = (pv[:, :HEAD_DIM] / pv[:, HEAD_DIM:]).astype(o_ref.dtype)


def mem_attention(proj, mkv, *, batch, seq, n_mem, q_col, tq):
    nsb = seq // tq
    q_blk = q_col // MEM_WIDTH
    return pl.pallas_call(
        _mem_kernel,
        grid=(batch, nsb),
        in_specs=[
            pl.BlockSpec((tq, MEM_WIDTH), lambda b, sb: (b * nsb + sb, q_blk)),
            pl.BlockSpec((n_mem, MEM_WIDTH), lambda b, sb: (b, 0)),
            pl.BlockSpec((n_mem, MEM_WIDTH), lambda b, sb: (b, 1)),
        ],
        out_specs=pl.BlockSpec((tq, MEM_WIDTH), lambda b, sb: (b * nsb + sb, 0)),
        out_shape=jax.ShapeDtypeStruct((batch * seq, MEM_WIDTH), BF16),
        compiler_params=_params(2),
        name="mem_attention",
    )(proj, mkv, mkv)


ATTN_GROUP = 4
ATTN_HEADS = 2


def kernel(x, mem, rel_bias, g_mix, w_in_a, w_in_b, g_mem, w_mem_kv, w_o, g_ffn, w_gate_up, w_down,
           g_kv, w_kv_shared, lambda_qk, g_subln, g_final):
    batch, seq, d = x.shape
    n_mem = mem.shape[1]
    depth = g_mix.shape[0]
    n_a = w_in_a.shape[0]
    tokens = batch * seq
    group = math.gcd(batch, ATTN_GROUP)

    w_kv_shared = w_kv_shared[None]

    tab = rel_bias.T.astype(F32)
    bkt_diag, bkt_prev = _bucket_tiles()

    moba_scale = HEAD_DIM ** -0.5 * LOG2E
    diff_scale = DIFF_HALF ** -0.5 * LOG2E
    mem_scale = HEAD_DIM ** -0.5 * LOG2E
    ones = functools.partial(jnp.ones, dtype=F32)
    cs_a = jnp.concatenate([jnp.full((SELF_WIDTH,), moba_scale, F32), ones((2 * SELF_WIDTH,)),
                            jnp.full((MEM_WIDTH,), mem_scale, F32)])
    cs_b = jnp.concatenate([jnp.full((SELF_WIDTH,), diff_scale, F32),
                            jnp.full((MEM_WIDTH,), mem_scale, F32)])

    h = x.reshape(tokens, d)
    mem2 = mem.reshape(batch * n_mem, d)
    kv = None
    for l in range(depth):
        mkv = norm_matmul(mem2, g_mem[l], w_mem_kv, l, ones((2 * MEM_WIDTH,)),
                          tm=batch * n_mem, tn=512)
        if l < n_a:
            proj = norm_matmul(h, g_mix[l], w_in_a, l, cs_a, tm=1024, tn=1024)
            y_self = moba_attention(proj.reshape(batch, seq, -1), tab, bkt_diag, bkt_prev,
                                    group=group, heads=ATTN_HEADS, gate_fill=NEG * moba_scale)
            q_col = 3 * SELF_WIDTH
        else:
            j = l - n_a
            if kv is None:
                kv = norm_matmul(h, g_kv, w_kv_shared, 0, ones((2 * SELF_WIDTH,)), tm=1024, tn=1024)
                kv = kv.reshape(batch, seq, -1)
            proj = norm_matmul(h, g_mix[l], w_in_b, j, cs_b, tm=1024, tn=1024)
            lam_init = 0.8 - 0.6 * math.exp(-0.3 * l)
            y_self = diff_attention(proj.reshape(batch, seq, -1), kv, tab, bkt_diag, bkt_prev,
                                    lambda_qk[j], g_subln[j], group=group, heads=ATTN_HEADS,
                                    lam_init=lam_init)
            q_col = SELF_WIDTH
        y_mem = mem_attention(proj, mkv, batch=batch, seq=seq, n_mem=n_mem,
                              q_col=q_col, tq=1024)
        h = residual_matmul2(h, y_self.reshape(tokens, -1), y_mem, w_o, l, tm=2048, tn=512)
        h = ffn_block(h, g_ffn[l], w_gate_up, w_down, l, tm=1024, tc=256,
                      final_gain=g_final if l == depth - 1 else None)
    return h.reshape(batch, seq, d)
```

```python
import functools
import math

import jax
import jax.numpy as jnp
import numpy as np
from jax import lax
from jax.experimental import pallas as pl
from jax.experimental.pallas import tpu as pltpu

D_MODEL = 2048
HEAD_DIM = 128
N_MEM_HEADS = 4
MEM_WIDTH = N_MEM_HEADS * HEAD_DIM
SELF_WIDTH = D_MODEL - MEM_WIDTH
N_SELF_HEADS = SELF_WIDTH // HEAD_DIM
DIFF_HALF = HEAD_DIM // 2
MOBA_BLOCK = 256
MOBA_TOPK = 3
N_BUCKETS = 32
MAX_DISTANCE = 128
RMS_EPS = 1e-6
NEG = -1e30
LOG2E = math.log2(math.e)

BLK = MOBA_BLOCK
V7X_VMEM_BYTES = 64 * 1024 * 1024
VMEM_LIMIT = 56 * 1024 * 1024

FFN_VMEM_LIMIT = 60 * 1024 * 1024

F32 = jnp.float32
BF16 = jnp.bfloat16
NT_DIMS = (((1,), (1,)), ((), ()))


def _params(n_grid_axes):
    return pltpu.CompilerParams(
        dimension_semantics=("arbitrary",) * n_grid_axes, vmem_limit_bytes=VMEM_LIMIT)


def _rmsnorm_rows(x, g):
    ms = jnp.mean(x * x, axis=-1, keepdims=True)
    return x * lax.rsqrt(ms + RMS_EPS) * g


def _norm_matmul_kernel(x_ref, g_ref, w_ref, cs_ref, o_ref, xn_ref):
    @pl.when(pl.program_id(1) == 0)
    def _():
        xn_ref[...] = _rmsnorm_rows(x_ref[...], g_ref[...]).astype(BF16)

    acc = jnp.dot(xn_ref[...], w_ref[...].astype(BF16), preferred_element_type=F32)
    o_ref[...] = (acc * cs_ref[...]).astype(o_ref.dtype)


def norm_matmul(x, g, w, layer, col_scale, *, tm, tn):
    m, d = x.shape
    n = w.shape[-1]
    return pl.pallas_call(
        _norm_matmul_kernel,
        grid=(m // tm, n // tn),
        in_specs=[
            pl.BlockSpec((tm, d), lambda i, j: (i, 0)),
            pl.BlockSpec((1, d), lambda i, j: (0, 0)),
            pl.BlockSpec((None, d, tn), lambda i, j: (layer, 0, j)),
            pl.BlockSpec((1, tn), lambda i, j: (0, j)),
        ],
        out_specs=pl.BlockSpec((tm, tn), lambda i, j: (i, j)),
        out_shape=jax.ShapeDtypeStruct((m, n), BF16),
        scratch_shapes=[pltpu.VMEM((tm, d), BF16)],
        compiler_params=_params(2),
        name="norm_matmul",
    )(x, g.reshape(1, d), w, col_scale.reshape(1, n))


def _ffn_kernel(h_ref, g_ref, gf_ref, wg_ref, wu_ref, wd_ref, o_ref, xn_ref, *, n_slab, final_norm):
    c = pl.program_id(1)

    @pl.when(c == 0)
    def _():
        h = h_ref[...]
        xn_ref[...] = _rmsnorm_rows(h, g_ref[...]).astype(BF16)
        o_ref[...] = h

    xn = xn_ref[...]
    gate = jnp.dot(xn, wg_ref[...].astype(BF16), preferred_element_type=F32)
    up = jnp.dot(xn, wu_ref[...].astype(BF16), preferred_element_type=F32)
    act = (gate * jax.nn.sigmoid(gate) * up).astype(BF16)
    slab = o_ref.shape[1] // n_slab
    for n in range(n_slab):
        cols = slice(n * slab, (n + 1) * slab)
        o_ref[:, cols] += jnp.dot(act, wd_ref[:, cols].astype(BF16), preferred_element_type=F32)

    if final_norm:
        @pl.when(c == pl.num_programs(1) - 1)
        def _():
            o_ref[...] = _rmsnorm_rows(o_ref[...], gf_ref[...])


def ffn_block(h, g, w_gate_up, w_down, layer, *, tm, tc, final_gain=None):
    m, d = h.shape
    d_ff = w_down.shape[1]
    nc = d_ff // tc
    final_norm = final_gain is not None
    gf = final_gain if final_norm else g
    return pl.pallas_call(
        functools.partial(_ffn_kernel, n_slab=4, final_norm=final_norm),
        grid=(m // tm, nc),
        in_specs=[
            pl.BlockSpec((tm, d), lambda i, c: (i, 0)),
            pl.BlockSpec((1, d), lambda i, c: (0, 0)),
            pl.BlockSpec((1, d), lambda i, c: (0, 0)),
            pl.BlockSpec((None, d, tc), lambda i, c: (layer, 0, c)),
            pl.BlockSpec((None, d, tc), lambda i, c: (layer, 0, c + nc)),
            pl.BlockSpec((None, tc, d), lambda i, c: (layer, c, 0)),
        ],
        out_specs=pl.BlockSpec((tm, d), lambda i, c: (i, 0)),
        out_shape=jax.ShapeDtypeStruct((m, d), F32),
        scratch_shapes=[pltpu.VMEM((tm, d), BF16)],
        compiler_params=pltpu.CompilerParams(
            dimension_semantics=("arbitrary", "arbitrary"), vmem_limit_bytes=FFN_VMEM_LIMIT),
        name="ffn_block",
    )(h, g.reshape(1, d), gf.reshape(1, d), w_gate_up, w_gate_up, w_down)


def _residual_matmul2_kernel(h_ref, a1_ref, a2_ref, w1_ref, w2_ref, o_ref):
    acc = jnp.dot(a1_ref[...], w1_ref[...].astype(BF16), preferred_element_type=F32)
    acc = acc + jnp.dot(a2_ref[...], w2_ref[...].astype(BF16), preferred_element_type=F32)
    o_ref[...] = h_ref[...] + acc


def residual_matmul2(h, a1, a2, w, layer, *, tm, tn):
    m, n = h.shape
    k1, k2 = a1.shape[1], a2.shape[1]
    assert k1 % k2 == 0
    return pl.pallas_call(
        _residual_matmul2_kernel,
        grid=(m // tm, n // tn),
        in_specs=[
            pl.BlockSpec((tm, tn), lambda i, j: (i, j)),
            pl.BlockSpec((tm, k1), lambda i, j: (i, 0)),
            pl.BlockSpec((tm, k2), lambda i, j: (i, 0)),
            pl.BlockSpec((None, k1, tn), lambda i, j: (layer, 0, j)),
            pl.BlockSpec((None, k2, tn), lambda i, j: (layer, k1 // k2, j)),
        ],
        out_specs=pl.BlockSpec((tm, tn), lambda i, j: (i, j)),
        out_shape=jax.ShapeDtypeStruct((m, n), F32),
        compiler_params=_params(2),
        name="residual_matmul2",
    )(h, a1, a2, w, w)


def _rel_bucket_np(rel):
    n = np.maximum(rel, 0)
    max_exact = N_BUCKETS // 2
    nf = np.maximum(n, 1).astype(np.float32)
    large = max_exact + (np.log(nf / np.float32(max_exact)) / np.float32(math.log(MAX_DISTANCE / max_exact))
                         * np.float32(N_BUCKETS - max_exact)).astype(np.int32)
    large = np.minimum(large, N_BUCKETS - 1)
    return np.where(n < max_exact, n, large).astype(np.int32)


def _bucket_tiles():
    qi = np.arange(BLK, dtype=np.int32)[:, None]
    ki = np.arange(BLK, dtype=np.int32)[None, :]
    diag = _rel_bucket_np(qi - ki)
    prev = _rel_bucket_np(qi - ki + BLK)
    assert _rel_bucket_np(np.array([BLK + 1]))[0] == N_BUCKETS - 1
    return jnp.asarray(diag), jnp.asarray(prev)


def _bias_tile(bkt, tab_ref, head):
    out = jnp.zeros(bkt.shape, F32)
    for b in range(N_BUCKETS):
        out = jnp.where(bkt == b, tab_ref[head, b] * LOG2E, out)
    return out


def _causal_tile(tile):
    qi = lax.broadcasted_iota(jnp.int32, tile.shape, 0)
    ki = lax.broadcasted_iota(jnp.int32, tile.shape, 1)
    return jnp.where(ki <= qi, tile, NEG)


def _wide(x):
    return jnp.tile(x, (1, BLK // HEAD_DIM))


def _softmax_step(s, v, m_ref, acc_ref, *, first, shift=None):
    v_aug = jnp.concatenate([v, jnp.ones(v.shape, v.dtype)], axis=1)
    m_cur = jnp.max(s, axis=-1, keepdims=True)
    if shift is not None:
        m_cur = m_cur + shift
    if first:
        m_new = jnp.broadcast_to(m_cur, m_ref.shape)
    else:
        m_old = m_ref[...]
        m_new = jnp.maximum(m_old, m_cur)
    p = jnp.exp2(s - _wide(m_new if shift is None else m_new - shift))
    pv = jnp.dot(p.astype(BF16), v_aug, preferred_element_type=F32)
    if first:
        acc_ref[...] = pv
    else:
        acc_ref[...] = _wide(jnp.exp2(m_old - m_new)) * acc_ref[...] + pv
    m_ref[...] = m_new


def _softmax_result(acc_ref):
    acc = acc_ref[...]
    return acc[:, :HEAD_DIM] / acc[:, HEAD_DIM:]


def _head_cols(hh):
    return slice(hh * HEAD_DIM, (hh + 1) * HEAD_DIM)


def _moba_kernel(tab_ref, bd_ref, bp_ref, q_ref, k_ref, v_ref, o_ref,
                 bias_d_ref, bias_p_ref, kmean_ref, qa_ref, m_ref, acc_ref, *, gate_fill, heads):
    hg, bg, qb = pl.program_id(0), pl.program_id(1), pl.program_id(2)
    group, seq, _ = k_ref.shape
    nb = seq // BLK
    chains = [(hh, g) for hh in range(heads) for g in range(group)]

    @pl.when((bg == 0) & (qb == 0))
    def _():
        for hh in range(heads):
            h = hg * heads + hh
            bias_d_ref[hh] = _causal_tile(_bias_tile(bd_ref[...], tab_ref, h))
            bias_p_ref[hh] = _bias_tile(bp_ref[...], tab_ref, h)

    @pl.when(qb == 0)
    def _():
        for c, (hh, g) in enumerate(chains):
            kf = k_ref[g, :, _head_cols(hh)].astype(F32).reshape(nb, BLK, HEAD_DIM)
            kmean_ref[c] = jnp.mean(kf, axis=1)

    row = lax.broadcasted_iota(jnp.int32, (nb, BLK), 0)
    past = row < qb
    n_sel = max(1, min(MOBA_TOPK, nb - 1))
    for c, (hh, g) in enumerate(chains):
        q = q_ref[g, :, _head_cols(hh)]
        gate = lax.dot_general(kmean_ref[c], q.astype(F32), NT_DIMS,
                               precision=lax.Precision.HIGHEST, preferred_element_type=F32)
        gate = jnp.where(past, gate, gate_fill)
        beaten_by = jnp.zeros(gate.shape, F32)
        for i in range(nb):
            gi = gate[i:i + 1, :]
            wins = (gi > gate) | ((gi == gate) & (i < row))
            beaten_by = beaten_by + jnp.where(wins, 1.0, 0.0)
        keep = (beaten_by < n_sel) & past
        sel_neg = jnp.where(keep, 0.0, NEG)
        sel_neg = jnp.concatenate([sel_neg, jnp.zeros((HEAD_DIM - nb, BLK), F32)], axis=0)
        qa_ref[c] = jnp.concatenate([q, sel_neg.T.astype(BF16)], axis=1)

    lane = lax.broadcasted_iota(jnp.int32, (BLK, HEAD_DIM), 1)

    def block_scores(c, j):
        hh, g = chains[c]
        start = pl.multiple_of(j * BLK, BLK)
        kj = k_ref[g, pl.ds(start, BLK), _head_cols(hh)]
        onehot = jnp.where(lane == j, 1.0, 0.0).astype(BF16)
        k_aug = jnp.concatenate([kj, onehot], axis=1)
        s = lax.dot_general(qa_ref[c], k_aug, NT_DIMS, preferred_element_type=F32)
        return s, v_ref[g, pl.ds(start, BLK), _head_cols(hh)]

    start = pl.multiple_of(qb * BLK, BLK)
    for c, (hh, g) in enumerate(chains):
        s = lax.dot_general(q_ref[g, :, _head_cols(hh)], k_ref[g, pl.ds(start, BLK), _head_cols(hh)],
                            NT_DIMS, preferred_element_type=F32)
        _softmax_step(s + bias_d_ref[hh], v_ref[g, pl.ds(start, BLK), _head_cols(hh)],
                      m_ref.at[c], acc_ref.at[c], first=True)

    @pl.when(qb >= 1)
    def _():
        for c, (hh, g) in enumerate(chains):
            s, v = block_scores(c, qb - 1)
            _softmax_step(s + bias_p_ref[hh], v, m_ref.at[c], acc_ref.at[c], first=False)

    far_bias = [tab_ref[hg * heads + hh, N_BUCKETS - 1] * LOG2E for hh in range(heads)]

    def far_body(d, carry):
        for c, (hh, g) in enumerate(chains):
            s, v = block_scores(c, qb - d)
            _softmax_step(s, v, m_ref.at[c], acc_ref.at[c], first=False, shift=far_bias[hh])
        return carry

    lax.fori_loop(2, qb + 1, far_body, 0)

    for c, (hh, g) in enumerate(chains):
        o_ref[g, :, _head_cols(hh)] = _softmax_result(acc_ref.at[c]).astype(o_ref.dtype)


def moba_attention(proj, tab, bkt_diag, bkt_prev, *, group, heads, gate_fill):
    batch, seq, _ = proj.shape
    nqb = seq // BLK
    k_off = N_SELF_HEADS // heads
    width = heads * HEAD_DIM
    chains = heads * group
    kern = functools.partial(_moba_kernel, gate_fill=gate_fill, heads=heads)
    return pl.pallas_call(
        kern,
        grid=(N_SELF_HEADS // heads, batch // group, nqb),
        in_specs=[
            pl.BlockSpec(memory_space=pltpu.SMEM),
            pl.BlockSpec((BLK, BLK), lambda h, b, qb: (0, 0)),
            pl.BlockSpec((BLK, BLK), lambda h, b, qb: (0, 0)),
            pl.BlockSpec((group, BLK, width), lambda h, b, qb: (b, qb, h)),
            pl.BlockSpec((group, seq, width), lambda h, b, qb: (b, 0, k_off + h)),
            pl.BlockSpec((group, seq, width), lambda h, b, qb: (b, 0, 2 * k_off + h)),
        ],
        out_specs=pl.BlockSpec((group, BLK, width), lambda h, b, qb: (b, qb, h)),
        out_shape=jax.ShapeDtypeStruct((batch, seq, SELF_WIDTH), BF16),
        scratch_shapes=[
            pltpu.VMEM((heads, BLK, BLK), F32), pltpu.VMEM((heads, BLK, BLK), F32),
            pltpu.VMEM((chains, seq // BLK, HEAD_DIM), F32),
            pltpu.VMEM((chains, BLK, 2 * HEAD_DIM), BF16),
            pltpu.VMEM((chains, BLK, HEAD_DIM), F32), pltpu.VMEM((chains, BLK, 2 * HEAD_DIM), F32),
        ],
        compiler_params=_params(3),
        name="moba_attention",
    )(tab, bkt_diag, bkt_prev, proj, proj, proj)


def _diff_kernel(tab_ref, bd_ref, bp_ref, lq_ref, gs_ref, q_ref, k_ref, v_ref, o_ref,
                 bias_d_ref, bias_p_ref, q2_ref, m_ref, acc_ref, *, lam_init, heads):
    hg, bg, qb = pl.program_id(0), pl.program_id(1), pl.program_id(2)
    group = k_ref.shape[0]
    chains = [(hh, g) for hh in range(heads) for g in range(group)]

    @pl.when((bg == 0) & (qb == 0))
    def _():
        for hh in range(heads):
            h = hg * heads + hh
            bd = _causal_tile(_bias_tile(bd_ref[...], tab_ref, h))
            bp = _bias_tile(bp_ref[...], tab_ref, h)
            bias_d_ref[hh] = jnp.concatenate([bd, bd], axis=0)
            bias_p_ref[hh] = jnp.concatenate([bp, bp], axis=0)

    lane = lax.broadcasted_iota(jnp.int32, (BLK, HEAD_DIM), 1)
    for c, (hh, g) in enumerate(chains):
        q = q_ref[g, :, _head_cols(hh)]
        zero = jnp.zeros_like(q)
        q2_ref[c] = jnp.concatenate([jnp.where(lane < DIFF_HALF, q, zero),
                                     jnp.where(lane >= DIFF_HALF, q, zero)], axis=0)

    def block_scores(c, j):
        hh, g = chains[c]
        start = pl.multiple_of(j * BLK, BLK)
        s = lax.dot_general(q2_ref[c], k_ref[g, pl.ds(start, BLK), _head_cols(hh)], NT_DIMS,
                            preferred_element_type=F32)
        return s, v_ref[g, pl.ds(start, BLK), _head_cols(hh)]

    for c, (hh, g) in enumerate(chains):
        s, v = block_scores(c, qb)
        _softmax_step(s + bias_d_ref[hh], v, m_ref.at[c], acc_ref.at[c], first=True)

    @pl.when(qb >= 1)
    def _():
        for c, (hh, g) in enumerate(chains):
            s, v = block_scores(c, qb - 1)
            _softmax_step(s + bias_p_ref[hh], v, m_ref.at[c], acc_ref.at[c], first=False)

    far_bias = [tab_ref[hg * heads + hh, N_BUCKETS - 1] * LOG2E for hh in range(heads)]

    def far_body(d, carry):
        for c, (hh, g) in enumerate(chains):
            s, v = block_scores(c, qb - d)
            _softmax_step(s, v, m_ref.at[c], acc_ref.at[c], first=False, shift=far_bias[hh])
        return carry

    lax.fori_loop(2, qb + 1, far_body, 0)

    lq = lq_ref[...]
    lam = (jnp.exp(jnp.sum(lq[0:1] * lq[1:2], axis=-1, keepdims=True))
           - jnp.exp(jnp.sum(lq[2:3] * lq[3:4], axis=-1, keepdims=True)) + lam_init)
    for c, (hh, g) in enumerate(chains):
        o_all = _softmax_result(acc_ref.at[c])
        o = o_all[:BLK] - lam * o_all[BLK:]
        o_ref[g, :, _head_cols(hh)] = (
            _rmsnorm_rows(o, gs_ref[...]) * (1.0 - lam_init)).astype(o_ref.dtype)


def diff_attention(proj, kv, tab, bkt_diag, bkt_prev, lq, g_subln, *, group, heads, lam_init):
    batch, seq, _ = proj.shape
    nqb = seq // BLK
    v_off = N_SELF_HEADS // heads
    width = heads * HEAD_DIM
    chains = heads * group
    kern = functools.partial(_diff_kernel, lam_init=lam_init, heads=heads)
    return pl.pallas_call(
        kern,
        grid=(N_SELF_HEADS // heads, batch // group, nqb),
        in_specs=[
            pl.BlockSpec(memory_space=pltpu.SMEM),
            pl.BlockSpec((BLK, BLK), lambda h, b, qb: (0, 0)),
            pl.BlockSpec((BLK, BLK), lambda h, b, qb: (0, 0)),
            pl.BlockSpec((4, DIFF_HALF), lambda h, b, qb: (0, 0)),
            pl.BlockSpec((1, HEAD_DIM), lambda h, b, qb: (0, 0)),
            pl.BlockSpec((group, BLK, width), lambda h, b, qb: (b, qb, h)),
            pl.BlockSpec((group, seq, width), lambda h, b, qb: (b, 0, h)),
            pl.BlockSpec((group, seq, width), lambda h, b, qb: (b, 0, v_off + h)),
        ],
        out_specs=pl.BlockSpec((group, BLK, width), lambda h, b, qb: (b, qb, h)),
        out_shape=jax.ShapeDtypeStruct((batch, seq, SELF_WIDTH), BF16),
        scratch_shapes=[
            pltpu.VMEM((heads, 2 * BLK, BLK), F32), pltpu.VMEM((heads, 2 * BLK, BLK), F32),
            pltpu.VMEM((chains, 2 * BLK, HEAD_DIM), BF16),
            pltpu.VMEM((chains, 2 * BLK, HEAD_DIM), F32),
            pltpu.VMEM((chains, 2 * BLK, 2 * HEAD_DIM), F32),
        ],
        compiler_params=_params(3),
        name="diff_attention",
    )(tab, bkt_diag, bkt_prev, lq, g_subln.reshape(1, HEAD_DIM), proj, kv, kv)


def _mem_kernel(q_ref, mk_ref, mv_ref, o_ref):
    for hh in range(N_MEM_HEADS):
        cols = _head_cols(hh)
        s = lax.dot_general(q_ref[:, cols], mk_ref[:, cols], NT_DIMS, preferred_element_type=F32)
        p = jnp.exp2(s - jnp.max(s, axis=-1, keepdims=True))
        mv = mv_ref[:, cols]
        pv = jnp.dot(p.astype(BF16), jnp.concatenate([mv, jnp.ones(mv.shape, mv.dtype)], axis=1),
                     preferred_element_type=F32)
        o_ref[:, cols] = (pv[:, :HEAD_DIM] / pv[:, HEAD_DIM:]).astype(o_ref.dtype)


def mem_attention(proj, mkv, *, batch, seq, n_mem, q_col, tq):
    nsb = seq // tq
    q_blk = q_col // MEM_WIDTH
    return pl.pallas_call(
        _mem_kernel,
        grid=(batch, nsb),
        in_specs=[
            pl.BlockSpec((tq, MEM_WIDTH), lambda b, sb: (b * nsb + sb, q_blk)),
            pl.BlockSpec((n_mem, MEM_WIDTH), lambda b, sb: (b, 0)),
            pl.BlockSpec((n_mem, MEM_WIDTH), lambda b, sb: (b, 1)),
        ],
        out_specs=pl.BlockSpec((tq, MEM_WIDTH), lambda b, sb: (b * nsb + sb, 0)),
        out_shape=jax.ShapeDtypeStruct((batch * seq, MEM_WIDTH), BF16),
        compiler_params=_params(2),
        name="mem_attention",
    )(proj, mkv, mkv)


ATTN_GROUP = 4
ATTN_HEADS = 4


def kernel(x, mem, rel_bias, g_mix, w_in_a, w_in_b, g_mem, w_mem_kv, w_o, g_ffn, w_gate_up, w_down,
           g_kv, w_kv_shared, lambda_qk, g_subln, g_final):
    batch, seq, d = x.shape
    n_mem = mem.shape[1]
    depth = g_mix.shape[0]
    n_a = w_in_a.shape[0]
    tokens = batch * seq
    group = math.gcd(batch, ATTN_GROUP)

    w_kv_shared = w_kv_shared[None]

    tab = rel_bias.T.astype(F32)
    bkt_diag, bkt_prev = _bucket_tiles()

    moba_scale = HEAD_DIM ** -0.5 * LOG2E
    diff_scale = DIFF_HALF ** -0.5 * LOG2E
    mem_scale = HEAD_DIM ** -0.5 * LOG2E
    ones = functools.partial(jnp.ones, dtype=F32)
    cs_a = jnp.concatenate([jnp.full((SELF_WIDTH,), moba_scale, F32), ones((2 * SELF_WIDTH,)),
                            jnp.full((MEM_WIDTH,), mem_scale, F32)])
    cs_b = jnp.concatenate([jnp.full((SELF_WIDTH,), diff_scale, F32),
                            jnp.full((MEM_WIDTH,), mem_scale, F32)])

    h = x.reshape(tokens, d)
    mem2 = mem.reshape(batch * n_mem, d)
    kv = None
    for l in range(depth):
        mkv = norm_matmul(mem2, g_mem[l], w_mem_kv, l, ones((2 * MEM_WIDTH,)),
                          tm=batch * n_mem, tn=512)
        if l < n_a:
            proj = norm_matmul(h, g_mix[l], w_in_a, l, cs_a, tm=1024, tn=1024)
            y_self = moba_attention(proj.reshape(batch, seq, -1), tab, bkt_diag, bkt_prev,
                                    group=group, heads=ATTN_HEADS, gate_fill=NEG * moba_scale)
            q_col = 3 * SELF_WIDTH
        else:
            j = l - n_a
            if kv is None:
                kv = norm_matmul(h, g_kv, w_kv_shared, 0, ones((2 * SELF_WIDTH,)), tm=1024, tn=1024)
                kv = kv.reshape(batch, seq, -1)
            proj = norm_matmul(h, g_mix[l], w_in_b, j, cs_b, tm=1024, tn=1024)
            lam_init = 0.8 - 0.6 * math.exp(-0.3 * l)
            y_self = diff_attention(proj.reshape(batch, seq, -1), kv, tab, bkt_diag, bkt_prev,
                                    lambda_qk[j], g_subln[j], group=group, heads=ATTN_HEADS,
                                    lam_init=lam_init)
            q_col = SELF_WIDTH
        y_mem = mem_attention(proj, mkv, batch=batch, seq=seq, n_mem=n_mem,
                              q_col=q_col, tq=1024)
        h = residual_matmul2(h, y_self.reshape(tokens, -1), y_mem, w_o, l, tm=2048, tn=512)
        h = ffn_block(h, g_ffn[l], w_gate_up, w_down, l, tm=1024, tc=256,
                      final_gain=g_final if l == depth - 1 else None)
    return h.reshape(batch, seq, d)
```

```python
import functools
import math

import jax
import jax.numpy as jnp
import numpy as np
from jax import lax
from jax.experimental import pallas as pl
from jax.experimental.pallas import tpu as pltpu

D_MODEL = 2048
HEAD_DIM = 128
N_MEM_HEADS = 4
MEM_WIDTH = N_MEM_HEADS * HEAD_DIM
SELF_WIDTH = D_MODEL - MEM_WIDTH
N_SELF_HEADS = SELF_WIDTH // HEAD_DIM
DIFF_HALF = HEAD_DIM // 2
MOBA_BLOCK = 256
MOBA_TOPK = 3
N_BUCKETS = 32
MAX_DISTANCE = 128
RMS_EPS = 1e-6
NEG = -1e30
LOG2E = math.log2(math.e)

BLK = MOBA_BLOCK
V7X_VMEM_BYTES = 64 * 1024 * 1024
VMEM_LIMIT = 56 * 1024 * 1024

FFN_VMEM_LIMIT = 60 * 1024 * 1024

F32 = jnp.float32
BF16 = jnp.bfloat16
NT_DIMS = (((1,), (1,)), ((), ()))


def _params(n_grid_axes):
    return pltpu.CompilerParams(
        dimension_semantics=("arbitrary",) * n_grid_axes, vmem_limit_bytes=VMEM_LIMIT)


def _rmsnorm_rows(x, g):
    ms = jnp.mean(x * x, axis=-1, keepdims=True)
    return x * lax.rsqrt(ms + RMS_EPS) * g


ROW_PARTS = 4


def _staggered_row_specs(tm, d, n_tiles, n_steps):
    part = tm // ROW_PARTS
    last = n_tiles * ROW_PARTS - 1
    specs = []
    for q in range(ROW_PARTS):
        s_q = 1 + q * n_steps // ROW_PARTS

        def index(i, c, q=q, s_q=s_q):
            ahead = jnp.where(c >= s_q, 1, 0) if s_q < n_steps else 0
            return (jnp.minimum(ROW_PARTS * (i + ahead) + q, last), 0)

        specs.append(pl.BlockSpec((part, d), index))
    return specs


def _row_part(ref, q):
    part = ref.shape[0] // ROW_PARTS
    return slice(q * part, (q + 1) * part)


def _norm_matmul_kernel(*refs):
    x_parts = refs[:ROW_PARTS]
    g_ref, w_ref, cs_ref, o_ref, xn_ref = refs[ROW_PARTS:]

    @pl.when(pl.program_id(1) == 0)
    def _():
        for q, x_ref in enumerate(x_parts):
            xn_ref[_row_part(xn_ref, q), :] = _rmsnorm_rows(x_ref[...], g_ref[...]).astype(BF16)

    acc = jnp.dot(xn_ref[...], w_ref[...].astype(BF16), preferred_element_type=F32)
    o_ref[...] = (acc * cs_ref[...]).astype(o_ref.dtype)


def norm_matmul(x, g, w, layer, col_scale, *, tm, tn):
    m, d = x.shape
    n = w.shape[-1]
    return pl.pallas_call(
        _norm_matmul_kernel,
        grid=(m // tm, n // tn),
        in_specs=_staggered_row_specs(tm, d, m // tm, n // tn) + [
            pl.BlockSpec((1, d), lambda i, j: (0, 0)),
            pl.BlockSpec((None, d, tn), lambda i, j: (layer, 0, j)),
            pl.BlockSpec((1, tn), lambda i, j: (0, j)),
        ],
        out_specs=pl.BlockSpec((tm, tn), lambda i, j: (i, j)),
        out_shape=jax.ShapeDtypeStruct((m, n), BF16),
        scratch_shapes=[pltpu.VMEM((tm, d), BF16)],
        compiler_params=_params(2),
        name="norm_matmul",
    )(*([x] * ROW_PARTS), g.reshape(1, d), w, col_scale.reshape(1, n))


def _ffn_kernel(*refs, n_slab, final_norm):
    h_parts = refs[:ROW_PARTS]
    g_ref, gf_ref, wg_ref, wu_ref, wd_ref, o_ref, xn_ref = refs[ROW_PARTS:]
    c = pl.program_id(1)

    @pl.when(c == 0)
    def _():
        for q, h_ref in enumerate(h_parts):
            h = h_ref[...]
            xn_ref[_row_part(xn_ref, q), :] = _rmsnorm_rows(h, g_ref[...]).astype(BF16)
            o_ref[_row_part(o_ref, q), :] = h

    xn = xn_ref[...]
    gate = jnp.dot(xn, wg_ref[...].astype(BF16), preferred_element_type=F32)
    up = jnp.dot(xn, wu_ref[...].astype(BF16), preferred_element_type=F32)
    act = (gate * jax.nn.sigmoid(gate) * up).astype(BF16)
    slab = o_ref.shape[1] // n_slab
    for n in range(n_slab):
        cols = slice(n * slab, (n + 1) * slab)
        o_ref[:, cols] += jnp.dot(act, wd_ref[:, cols].astype(BF16), preferred_element_type=F32)

    if final_norm:
        @pl.when(c == pl.num_programs(1) - 1)
        def _():
            o_ref[...] = _rmsnorm_rows(o_ref[...], gf_ref[...])


def ffn_block(h, g, w_gate_up, w_down, layer, *, tm, tc, final_gain=None):
    m, d = h.shape
    d_ff = w_down.shape[1]
    nc = d_ff // tc
    final_norm = final_gain is not None
    gf = final_gain if final_norm else g
    return pl.pallas_call(
        functools.partial(_ffn_kernel, n_slab=4, final_norm=final_norm),
        grid=(m // tm, nc),
        in_specs=_staggered_row_specs(tm, d, m // tm, nc) + [
            pl.BlockSpec((1, d), lambda i, c: (0, 0)),
            pl.BlockSpec((1, d), lambda i, c: (0, 0)),
            pl.BlockSpec((None, d, tc), lambda i, c: (layer, 0, c)),
            pl.BlockSpec((None, d, tc), lambda i, c: (layer, 0, c + nc)),
            pl.BlockSpec((None, tc, d), lambda i, c: (layer, c, 0)),
        ],
        out_specs=pl.BlockSpec((tm, d), lambda i, c: (i, 0)),
        out_shape=jax.ShapeDtypeStruct((m, d), F32),
        scratch_shapes=[pltpu.VMEM((tm, d), BF16)],
        compiler_params=pltpu.CompilerParams(
            dimension_semantics=("arbitrary", "arbitrary"), vmem_limit_bytes=FFN_VMEM_LIMIT),
        name="ffn_block",
    )(*([h] * ROW_PARTS), g.reshape(1, d), gf.reshape(1, d), w_gate_up, w_gate_up, w_down)


def _residual_matmul2_kernel(h_ref, a1_ref, a2_ref, w1_ref, w2_ref, o_ref):
    acc = jnp.dot(a1_ref[...], w1_ref[...].astype(BF16), preferred_element_type=F32)
    acc = acc + jnp.dot(a2_ref[...], w2_ref[...].astype(BF16), preferred_element_type=F32)
    o_ref[...] = h_ref[...] + acc


def residual_matmul2(h, a1, a2, w, layer, *, tm, tn):
    m, n = h.shape
    k1, k2 = a1.shape[1], a2.shape[1]
    assert k1 % k2 == 0
    return pl.pallas_call(
        _residual_matmul2_kernel,
        grid=(m // tm, n // tn),
        in_specs=[
            pl.BlockSpec((tm, tn), lambda i, j: (i, j)),
            pl.BlockSpec((tm, k1), lambda i, j: (i, 0)),
            pl.BlockSpec((tm, k2), lambda i, j: (i, 0)),
            pl.BlockSpec((None, k1, tn), lambda i, j: (layer, 0, j)),
            pl.BlockSpec((None, k2, tn), lambda i, j: (layer, k1 // k2, j)),
        ],
        out_specs=pl.BlockSpec((tm, tn), lambda i, j: (i, j)),
        out_shape=jax.ShapeDtypeStruct((m, n), F32),
        compiler_params=_params(2),
        name="residual_matmul2",
    )(h, a1, a2, w, w)


def _rel_bucket_np(rel):
    n = np.maximum(rel, 0)
    max_exact = N_BUCKETS // 2
    nf = np.maximum(n, 1).astype(np.float32)
    large = max_exact + (np.log(nf / np.float32(max_exact)) / np.float32(math.log(MAX_DISTANCE / max_exact))
                         * np.float32(N_BUCKETS - max_exact)).astype(np.int32)
    large = np.minimum(large, N_BUCKETS - 1)
    return np.where(n < max_exact, n, large).astype(np.int32)


def _bucket_tiles():
    qi = np.arange(BLK, dtype=np.int32)[:, None]
    ki = np.arange(BLK, dtype=np.int32)[None, :]
    diag = _rel_bucket_np(qi - ki)
    prev = _rel_bucket_np(qi - ki + BLK)
    assert _rel_bucket_np(np.array([BLK + 1]))[0] == N_BUCKETS - 1
    return jnp.asarray(diag), jnp.asarray(prev)


def _bias_tile(bkt, tab_ref, head):
    out = jnp.zeros(bkt.shape, F32)
    for b in range(N_BUCKETS):
        out = jnp.where(bkt == b, tab_ref[head, b] * LOG2E, out)
    return out


def _causal_tile(tile):
    qi = lax.broadcasted_iota(jnp.int32, tile.shape, 0)
    ki = lax.broadcasted_iota(jnp.int32, tile.shape, 1)
    return jnp.where(ki <= qi, tile, NEG)


def _wide(x):
    return jnp.tile(x, (1, BLK // HEAD_DIM))


def _softmax_step(s, v, m_ref, acc_ref, *, first, shift=None):
    v_aug = jnp.concatenate([v, jnp.ones(v.shape, v.dtype)], axis=1)
    m_cur = jnp.max(s, axis=-1, keepdims=True)
    if shift is not None:
        m_cur = m_cur + shift
    if first:
        m_new = jnp.broadcast_to(m_cur, m_ref.shape)
    else:
        m_old = m_ref[...]
        m_new = jnp.maximum(m_old, m_cur)
    p = jnp.exp2(s - _wide(m_new if shift is None else m_new - shift))
    pv = jnp.dot(p.astype(BF16), v_aug, preferred_element_type=F32)
    if first:
        acc_ref[...] = pv
    else:
        acc_ref[...] = _wide(jnp.exp2(m_old - m_new)) * acc_ref[...] + pv
    m_ref[...] = m_new


def _softmax_result(acc_ref):
    acc = acc_ref[...]
    return acc[:, :HEAD_DIM] / acc[:, HEAD_DIM:]


def _head_cols(hh):
    return slice(hh * HEAD_DIM, (hh + 1) * HEAD_DIM)


def _moba_kernel(tab_ref, bd_ref, bp_ref, q_ref, k_ref, v_ref, o_ref,
                 bias_d_ref, bias_p_ref, kmean_ref, qa_ref, m_ref, acc_ref, *, gate_fill, heads):
    hg, bg, qb = pl.program_id(0), pl.program_id(1), pl.program_id(2)
    group, seq, _ = k_ref.shape
    nb = seq // BLK
    chains = [(hh, g) for hh in range(heads) for g in range(group)]

    @pl.when((bg == 0) & (qb == 0))
    def _():
        for hh in range(heads):
            h = hg * heads + hh
            bias_d_ref[hh] = _causal_tile(_bias_tile(bd_ref[...], tab_ref, h))
            bias_p_ref[hh] = _bias_tile(bp_ref[...], tab_ref, h)

    @pl.when(qb == 0)
    def _():
        for c, (hh, g) in enumerate(chains):
            kf = k_ref[g, :, _head_cols(hh)].astype(F32).reshape(nb, BLK, HEAD_DIM)
            kmean_ref[c] = jnp.mean(kf, axis=1)

    row = lax.broadcasted_iota(jnp.int32, (nb, BLK), 0)
    past = row < qb
    n_sel = max(1, min(MOBA_TOPK, nb - 1))
    for c, (hh, g) in enumerate(chains):
        q = q_ref[g, :, _head_cols(hh)]
        gate = lax.dot_general(kmean_ref[c], q.astype(F32), NT_DIMS,
                               precision=lax.Precision.HIGHEST, preferred_element_type=F32)
        gate = jnp.where(past, gate, gate_fill)
        beaten_by = jnp.zeros(gate.shape, F32)
        for i in range(nb):
            gi = gate[i:i + 1, :]
            wins = (gi > gate) | ((gi == gate) & (i < row))
            beaten_by = beaten_by + jnp.where(wins, 1.0, 0.0)
        keep = (beaten_by < n_sel) & past
        sel_neg = jnp.where(keep, 0.0, NEG)
        sel_neg = jnp.concatenate([sel_neg, jnp.zeros((HEAD_DIM - nb, BLK), F32)], axis=0)
        qa_ref[c] = jnp.concatenate([q, sel_neg.T.astype(BF16)], axis=1)

    lane = lax.broadcasted_iota(jnp.int32, (BLK, HEAD_DIM), 1)

    def block_scores(c, j):
        hh, g = chains[c]
        start = pl.multiple_of(j * BLK, BLK)
        kj = k_ref[g, pl.ds(start, BLK), _head_cols(hh)]
        onehot = jnp.where(lane == j, 1.0, 0.0).astype(BF16)
        k_aug = jnp.concatenate([kj, onehot], axis=1)
        s = lax.dot_general(qa_ref[c], k_aug, NT_DIMS, preferred_element_type=F32)
        return s, v_ref[g, pl.ds(start, BLK), _head_cols(hh)]

    start = pl.multiple_of(qb * BLK, BLK)
    for c, (hh, g) in enumerate(chains):
        s = lax.dot_general(q_ref[g, :, _head_cols(hh)], k_ref[g, pl.ds(start, BLK), _head_cols(hh)],
                            NT_DIMS, preferred_element_type=F32)
        _softmax_step(s + bias_d_ref[hh], v_ref[g, pl.ds(start, BLK), _head_cols(hh)],
                      m_ref.at[c], acc_ref.at[c], first=True)

    @pl.when(qb >= 1)
    def _():
        for c, (hh, g) in enumerate(chains):
            s, v = block_scores(c, qb - 1)
            _softmax_step(s + bias_p_ref[hh], v, m_ref.at[c], acc_ref.at[c], first=False)

    far_bias = [tab_ref[hg * heads + hh, N_BUCKETS - 1] * LOG2E for hh in range(heads)]

    def far_body(d, carry):
        for c, (hh, g) in enumerate(chains):
            s, v = block_scores(c, qb - d)
            _softmax_step(s, v, m_ref.at[c], acc_ref.at[c], first=False, shift=far_bias[hh])
        return carry

    lax.fori_loop(2, qb + 1, far_body, 0)

    for c, (hh, g) in enumerate(chains):
        o_ref[g, :, _head_cols(hh)] = _softmax_result(acc_ref.at[c]).astype(o_ref.dtype)


def moba_attention(proj, tab, bkt_diag, bkt_prev, *, group, heads, gate_fill):
    batch, seq, _ = proj.shape
    nqb = seq // BLK
    k_off = N_SELF_HEADS // heads
    width = heads * HEAD_DIM
    chains = heads * group
    kern = functools.partial(_moba_kernel, gate_fill=gate_fill, heads=heads)
    return pl.pallas_call(
        kern,
        grid=(N_SELF_HEADS // heads, batch // group, nqb),
        in_specs=[
            pl.BlockSpec(memory_space=pltpu.SMEM),
            pl.BlockSpec((BLK, BLK), lambda h, b, qb: (0, 0)),
            pl.BlockSpec((BLK, BLK), lambda h, b, qb: (0, 0)),
            pl.BlockSpec((group, BLK, width), lambda h, b, qb: (b, qb, h)),
            pl.BlockSpec((group, seq, width), lambda h, b, qb: (b, 0, k_off + h)),
            pl.BlockSpec((group, seq, width), lambda h, b, qb: (b, 0, 2 * k_off + h)),
        ],
        out_specs=pl.BlockSpec((group, BLK, width), lambda h, b, qb: (b, qb, h)),
        out_shape=jax.ShapeDtypeStruct((batch, seq, SELF_WIDTH), BF16),
        scratch_shapes=[
            pltpu.VMEM((heads, BLK, BLK), F32), pltpu.VMEM((heads, BLK, BLK), F32),
            pltpu.VMEM((chains, seq // BLK, HEAD_DIM), F32),
            pltpu.VMEM((chains, BLK, 2 * HEAD_DIM), BF16),
            pltpu.VMEM((chains, BLK, HEAD_DIM), F32), pltpu.VMEM((chains, BLK, 2 * HEAD_DIM), F32),
        ],
        compiler_params=_params(3),
        name="moba_attention",
    )(tab, bkt_diag, bkt_prev, proj, proj, proj)


def _diff_kernel(tab_ref, bd_ref, bp_ref, lq_ref, gs_ref, q_ref, k_ref, v_ref, o_ref,
                 bias_d_ref, bias_p_ref, q2_ref, m_ref, acc_ref, *, lam_init, heads):
    hg, bg, qb = pl.program_id(0), pl.program_id(1), pl.program_id(2)
    group = k_ref.shape[0]
    chains = [(hh, g) for hh in range(heads) for g in range(group)]

    @pl.when((bg == 0) & (qb == 0))
    def _():
        for hh in range(heads):
            h = hg * heads + hh
            bd = _causal_tile(_bias_tile(bd_ref[...], tab_ref, h))
            bp = _bias_tile(bp_ref[...], tab_ref, h)
            bias_d_ref[hh] = jnp.concatenate([bd, bd], axis=0)
            bias_p_ref[hh] = jnp.concatenate([bp, bp], axis=0)

    lane = lax.broadcasted_iota(jnp.int32, (BLK, HEAD_DIM), 1)
    for c, (hh, g) in enumerate(chains):
        q = q_ref[g, :, _head_cols(hh)]
        zero = jnp.zeros_like(q)
        q2_ref[c] = jnp.concatenate([jnp.where(lane < DIFF_HALF, q, zero),
                                     jnp.where(lane >= DIFF_HALF, q, zero)], axis=0)

    def block_scores(c, j):
        hh, g = chains[c]
        start = pl.multiple_of(j * BLK, BLK)
        s = lax.dot_general(q2_ref[c], k_ref[g, pl.ds(start, BLK), _head_cols(hh)], NT_DIMS,
                            preferred_element_type=F32)
        return s, v_ref[g, pl.ds(start, BLK), _head_cols(hh)]

    for c, (hh, g) in enumerate(chains):
        s, v = block_scores(c, qb)
        _softmax_step(s + bias_d_ref[hh], v, m_ref.at[c], acc_ref.at[c], first=True)

    @pl.when(qb >= 1)
    def _():
        for c, (hh, g) in enumerate(chains):
            s, v = block_scores(c, qb - 1)
            _softmax_step(s + bias_p_ref[hh], v, m_ref.at[c], acc_ref.at[c], first=False)

    far_bias = [tab_ref[hg * heads + hh, N_BUCKETS - 1] * LOG2E for hh in range(heads)]

    def far_body(d, carry):
        for c, (hh, g) in enumerate(chains):
            s, v = block_scores(c, qb - d)
            _softmax_step(s, v, m_ref.at[c], acc_ref.at[c], first=False, shift=far_bias[hh])
        return carry

    lax.fori_loop(2, qb + 1, far_body, 0)

    lq = lq_ref[...]
    lam = (jnp.exp(jnp.sum(lq[0:1] * lq[1:2], axis=-1, keepdims=True))
           - jnp.exp(jnp.sum(lq[2:3] * lq[3:4], axis=-1, keepdims=True)) + lam_init)
    for c, (hh, g) in enumerate(chains):
        o_all = _softmax_result(acc_ref.at[c])
        o = o_all[:BLK] - lam * o_all[BLK:]
        o_ref[g, :, _head_cols(hh)] = (
            _rmsnorm_rows(o, gs_ref[...]) * (1.0 - lam_init)).astype(o_ref.dtype)


def diff_attention(proj, kv, tab, bkt_diag, bkt_prev, lq, g_subln, *, group, heads, lam_init):
    batch, seq, _ = proj.shape
    nqb = seq // BLK
    v_off = N_SELF_HEADS // heads
    width = heads * HEAD_DIM
    chains = heads * group
    kern = functools.partial(_diff_kernel, lam_init=lam_init, heads=heads)
    return pl.pallas_call(
        kern,
        grid=(N_SELF_HEADS // heads, batch // group, nqb),
        in_specs=[
            pl.BlockSpec(memory_space=pltpu.SMEM),
            pl.BlockSpec((BLK, BLK), lambda h, b, qb: (0, 0)),
            pl.BlockSpec((BLK, BLK), lambda h, b, qb: (0, 0)),
            pl.BlockSpec((4, DIFF_HALF), lambda h, b, qb: (0, 0)),
            pl.BlockSpec((1, HEAD_DIM), lambda h, b, qb: (0, 0)),
            pl.BlockSpec((group, BLK, width), lambda h, b, qb: (b, qb, h)),
            pl.BlockSpec((group, seq, width), lambda h, b, qb: (b, 0, h)),
            pl.BlockSpec((group, seq, width), lambda h, b, qb: (b, 0, v_off + h)),
        ],
        out_specs=pl.BlockSpec((group, BLK, width), lambda h, b, qb: (b, qb, h)),
        out_shape=jax.ShapeDtypeStruct((batch, seq, SELF_WIDTH), BF16),
        scratch_shapes=[
            pltpu.VMEM((heads, 2 * BLK, BLK), F32), pltpu.VMEM((heads, 2 * BLK, BLK), F32),
            pltpu.VMEM((chains, 2 * BLK, HEAD_DIM), BF16),
            pltpu.VMEM((chains, 2 * BLK, HEAD_DIM), F32),
            pltpu.VMEM((chains, 2 * BLK, 2 * HEAD_DIM), F32),
        ],
        compiler_params=_params(3),
        name="diff_attention",
    )(tab, bkt_diag, bkt_prev, lq, g_subln.reshape(1, HEAD_DIM), proj, kv, kv)


def _mem_kernel(q_ref, mk_ref, mv_ref, o_ref):
    for hh in range(N_MEM_HEADS):
        cols = _head_cols(hh)
        s = lax.dot_general(q_ref[:, cols], mk_ref[:, cols], NT_DIMS, preferred_element_type=F32)
        p = jnp.exp2(s - jnp.max(s, axis=-1, keepdims=True))
        mv = mv_ref[:, cols]
        pv = jnp.dot(p.astype(BF16), jnp.concatenate([mv, jnp.ones(mv.shape, mv.dtype)], axis=1),
                     preferred_element_type=F32)
        o_ref[:, cols] = (pv[:, :HEAD_DIM] / pv[:, HEAD_DIM:]).astype(o_ref.dtype)


def mem_attention(proj, mkv, *, batch, seq, n_mem, q_col, tq):
    nsb = seq // tq
    q_blk = q_col // MEM_WIDTH
    return pl.pallas_call(
        _mem_kernel,
        grid=(batch, nsb),
        in_specs=[
            pl.BlockSpec((tq, MEM_WIDTH), lambda b, sb: (b * nsb + sb, q_blk)),
            pl.BlockSpec((n_mem, MEM_WIDTH), lambda b, sb: (b, 0)),
            pl.BlockSpec((n_mem, MEM_WIDTH), lambda b, sb: (b, 1)),
        ],
        out_specs=pl.BlockSpec((tq, MEM_WIDTH), lambda b, sb: (b * nsb + sb, 0)),
        out_shape=jax.ShapeDtypeStruct((batch * seq, MEM_WIDTH), BF16),
        compiler_params=_params(2),
        name="mem_attention",
    )(proj, mkv, mkv)


ATTN_GROUP = 4
ATTN_HEADS = 4


def kernel(x, mem, rel_bias, g_mix, w_in_a, w_in_b, g_mem, w_mem_kv, w_o, g_ffn, w_gate_up, w_down,
           g_kv, w_kv_shared, lambda_qk, g_subln, g_final):
    batch, seq, d = x.shape
    n_mem = mem.shape[1]
    depth = g_mix.shape[0]
    n_a = w_in_a.shape[0]
    tokens = batch * seq
    group = math.gcd(batch, ATTN_GROUP)

    w_kv_shared = w_kv_shared[None]

    tab = rel_bias.T.astype(F32)
    bkt_diag, bkt_prev = _bucket_tiles()

    moba_scale = HEAD_DIM ** -0.5 * LOG2E
    diff_scale = DIFF_HALF ** -0.5 * LOG2E
    mem_scale = HEAD_DIM ** -0.5 * LOG2E
    ones = functools.partial(jnp.ones, dtype=F32)
    cs_a = jnp.concatenate([jnp.full((SELF_WIDTH,), moba_scale, F32), ones((2 * SELF_WIDTH,)),
                            jnp.full((MEM_WIDTH,), mem_scale, F32)])
    cs_b = jnp.concatenate([jnp.full((SELF_WIDTH,), diff_scale, F32),
                            jnp.full((MEM_WIDTH,), mem_scale, F32)])

    h = x.reshape(tokens, d)
    mem2 = mem.reshape(batch * n_mem, d)
    kv = None
    for l in range(depth):
        mkv = norm_matmul(mem2, g_mem[l], w_mem_kv, l, ones((2 * MEM_WIDTH,)),
                          tm=batch * n_mem, tn=512)
        if l < n_a:
            proj = norm_matmul(h, g_mix[l], w_in_a, l, cs_a, tm=1024, tn=1024)
            y_self = moba_attention(proj.reshape(batch, seq, -1), tab, bkt_diag, bkt_prev,
                                    group=group, heads=ATTN_HEADS, gate_fill=NEG * moba_scale)
            q_col = 3 * SELF_WIDTH
        else:
            j = l - n_a
            if kv is None:
                kv = norm_matmul(h, g_kv, w_kv_shared, 0, ones((2 * SELF_WIDTH,)), tm=1024, tn=1024)
                kv = kv.reshape(batch, seq, -1)
            proj = norm_matmul(h, g_mix[l], w_in_b, j, cs_b, tm=1024, tn=1024)
            lam_init = 0.8 - 0.6 * math.exp(-0.3 * l)
            y_self = diff_attention(proj.reshape(batch, seq, -1), kv, tab, bkt_diag, bkt_prev,
                                    lambda_qk[j], g_subln[j], group=group, heads=ATTN_HEADS,
                                    lam_init=lam_init)
            q_col = SELF_WIDTH
        y_mem = mem_attention(proj, mkv, batch=batch, seq=seq, n_mem=n_mem,
                              q_col=q_col, tq=1024)
        h = residual_matmul2(h, y_self.reshape(tokens, -1), y_mem, w_o, l, tm=2048, tn=512)
        h = ffn_block(h, g_ffn[l], w_gate_up, w_down, l, tm=1024, tc=256,
                      final_gain=g_final if l == depth - 1 else None)
    return h.reshape(batch, seq, d)
```

```python
import functools
import math

import jax
import jax.numpy as jnp
import numpy as np
from jax import lax
from jax.experimental import pallas as pl
from jax.experimental.pallas import tpu as pltpu

D_MODEL = 2048
HEAD_DIM = 128
N_MEM_HEADS = 4
MEM_WIDTH = N_MEM_HEADS * HEAD_DIM
SELF_WIDTH = D_MODEL - MEM_WIDTH
N_SELF_HEADS = SELF_WIDTH // HEAD_DIM
DIFF_HALF = HEAD_DIM // 2
MOBA_BLOCK = 256
MOBA_TOPK = 3
N_BUCKETS = 32
MAX_DISTANCE = 128
RMS_EPS = 1e-6
NEG = -1e30
LOG2E = math.log2(math.e)

BLK = MOBA_BLOCK
V7X_VMEM_BYTES = 64 * 1024 * 1024
VMEM_LIMIT = V7X_VMEM_BYTES - 4 * 1024 * 1024

F32 = jnp.float32
BF16 = jnp.bfloat16
NT_DIMS = (((1,), (1,)), ((), ()))


def _params(n_grid_axes):
    return pltpu.CompilerParams(
        dimension_semantics=("arbitrary",) * n_grid_axes, vmem_limit_bytes=VMEM_LIMIT)


def _rmsnorm_rows(x, g):
    ms = jnp.mean(x * x, axis=-1, keepdims=True)
    return x * lax.rsqrt(ms + RMS_EPS) * g


ROW_PARTS = 4


def _staggered_row_specs(tm, d, n_tiles, n_steps):
    part = tm // ROW_PARTS
    last = n_tiles * ROW_PARTS - 1
    specs = []
    for q in range(ROW_PARTS):
        s_q = 1 + q * n_steps // ROW_PARTS

        def index(i, c, q=q, s_q=s_q):
            ahead = jnp.where(c >= s_q, 1, 0) if s_q < n_steps else 0
            return (jnp.minimum(ROW_PARTS * (i + ahead) + q, last), 0)

        specs.append(pl.BlockSpec((part, d), index))
    return specs


def _row_part(ref, q):
    part = ref.shape[0] // ROW_PARTS
    return slice(q * part, (q + 1) * part)


def _norm_matmul_kernel(*refs):
    x_parts = refs[:ROW_PARTS]
    g_ref, w_ref, cs_ref, o_ref, xn_ref = refs[ROW_PARTS:]

    @pl.when(pl.program_id(1) == 0)
    def _():
        for q, x_ref in enumerate(x_parts):
            xn_ref[_row_part(xn_ref, q), :] = _rmsnorm_rows(x_ref[...], g_ref[...]).astype(BF16)

    acc = jnp.dot(xn_ref[...], w_ref[...].astype(BF16), preferred_element_type=F32)
    o_ref[...] = (acc * cs_ref[...]).astype(o_ref.dtype)


def norm_matmul(x, g, w, layer, col_scale, *, tm, tn):
    m, d = x.shape
    n = w.shape[-1]
    return pl.pallas_call(
        _norm_matmul_kernel,
        grid=(m // tm, n // tn),
        in_specs=_staggered_row_specs(tm, d, m // tm, n // tn) + [
            pl.BlockSpec((1, d), lambda i, j: (0, 0)),
            pl.BlockSpec((None, d, tn), lambda i, j: (layer, 0, j)),
            pl.BlockSpec((1, tn), lambda i, j: (0, j)),
        ],
        out_specs=pl.BlockSpec((tm, tn), lambda i, j: (i, j)),
        out_shape=jax.ShapeDtypeStruct((m, n), BF16),
        scratch_shapes=[pltpu.VMEM((tm, d), BF16)],
        compiler_params=_params(2),
        name="norm_matmul",
    )(*([x] * ROW_PARTS), g.reshape(1, d), w, col_scale.reshape(1, n))


def _ffn_kernel(*refs, n_slab, final_norm):
    h_parts = refs[:ROW_PARTS]
    g_ref, gf_ref, wg_ref, wu_ref, wd_ref, o_ref, xn_ref = refs[ROW_PARTS:]
    c = pl.program_id(1)

    @pl.when(c == 0)
    def _():
        for q, h_ref in enumerate(h_parts):
            h = h_ref[...]
            xn_ref[_row_part(xn_ref, q), :] = _rmsnorm_rows(h, g_ref[...]).astype(BF16)
            o_ref[_row_part(o_ref, q), :] = h

    xn = xn_ref[...]
    gate = jnp.dot(xn, wg_ref[...].astype(BF16), preferred_element_type=F32)
    up = jnp.dot(xn, wu_ref[...].astype(BF16), preferred_element_type=F32)
    act = (gate * jax.nn.sigmoid(gate) * up).astype(BF16)
    slab = o_ref.shape[1] // n_slab
    for n in range(n_slab):
        cols = slice(n * slab, (n + 1) * slab)
        o_ref[:, cols] += jnp.dot(act, wd_ref[:, cols].astype(BF16), preferred_element_type=F32)

    if final_norm:
        @pl.when(c == pl.num_programs(1) - 1)
        def _():
            o_ref[...] = _rmsnorm_rows(o_ref[...], gf_ref[...])


def ffn_block(h, g, w_gate_up, w_down, layer, *, tm, tc, final_gain=None):
    m, d = h.shape
    d_ff = w_down.shape[1]
    nc = d_ff // tc
    final_norm = final_gain is not None
    gf = final_gain if final_norm else g
    return pl.pallas_call(
        functools.partial(_ffn_kernel, n_slab=4, final_norm=final_norm),
        grid=(m // tm, nc),
        in_specs=_staggered_row_specs(tm, d, m // tm, nc) + [
            pl.BlockSpec((1, d), lambda i, c: (0, 0)),
            pl.BlockSpec((1, d), lambda i, c: (0, 0)),
            pl.BlockSpec((None, d, tc), lambda i, c: (layer, 0, c)),
            pl.BlockSpec((None, d, tc), lambda i, c: (layer, 0, c + nc)),
            pl.BlockSpec((None, tc, d), lambda i, c: (layer, c, 0)),
        ],
        out_specs=pl.BlockSpec((tm, d), lambda i, c: (i, 0)),
        out_shape=jax.ShapeDtypeStruct((m, d), F32),
        scratch_shapes=[pltpu.VMEM((tm, d), BF16)],
        compiler_params=_params(2),
        name="ffn_block",
    )(*([h] * ROW_PARTS), g.reshape(1, d), gf.reshape(1, d), w_gate_up, w_gate_up, w_down)


def _residual_matmul2_kernel(h_ref, a1_ref, a2_ref, w1_ref, w2_ref, o_ref):
    acc = jnp.dot(a1_ref[...], w1_ref[...].astype(BF16), preferred_element_type=F32)
    acc = acc + jnp.dot(a2_ref[...], w2_ref[...].astype(BF16), preferred_element_type=F32)
    o_ref[...] = h_ref[...] + acc


def residual_matmul2(h, a1, a2, w, layer, *, tm, tn):
    m, n = h.shape
    k1, k2 = a1.shape[1], a2.shape[1]
    assert k1 % k2 == 0
    return pl.pallas_call(
        _residual_matmul2_kernel,
        grid=(m // tm, n // tn),
        in_specs=[
            pl.BlockSpec((tm, tn), lambda i, j: (i, j)),
            pl.BlockSpec((tm, k1), lambda i, j: (i, 0)),
            pl.BlockSpec((tm, k2), lambda i, j: (i, 0)),
            pl.BlockSpec((None, k1, tn), lambda i, j: (layer, 0, j)),
            pl.BlockSpec((None, k2, tn), lambda i, j: (layer, k1 // k2, j)),
        ],
        out_specs=pl.BlockSpec((tm, tn), lambda i, j: (i, j)),
        out_shape=jax.ShapeDtypeStruct((m, n), F32),
        compiler_params=_params(2),
        name="residual_matmul2",
    )(h, a1, a2, w, w)


def _rel_bucket_np(rel):
    n = np.maximum(rel, 0)
    max_exact = N_BUCKETS // 2
    nf = np.maximum(n, 1).astype(np.float32)
    large = max_exact + (np.log(nf / np.float32(max_exact)) / np.float32(math.log(MAX_DISTANCE / max_exact))
                         * np.float32(N_BUCKETS - max_exact)).astype(np.int32)
    large = np.minimum(large, N_BUCKETS - 1)
    return np.where(n < max_exact, n, large).astype(np.int32)


def _bucket_tiles():
    qi = np.arange(BLK, dtype=np.int32)[:, None]
    ki = np.arange(BLK, dtype=np.int32)[None, :]
    diag = _rel_bucket_np(qi - ki)
    prev = _rel_bucket_np(qi - ki + BLK)
    assert _rel_bucket_np(np.array([BLK + 1]))[0] == N_BUCKETS - 1
    return jnp.asarray(diag), jnp.asarray(prev)


def _bias_tile(bkt, tab_ref, head):
    out = jnp.zeros(bkt.shape, F32)
    for b in range(N_BUCKETS):
        out = jnp.where(bkt == b, tab_ref[head, b] * LOG2E, out)
    return out


def _causal_tile(tile):
    qi = lax.broadcasted_iota(jnp.int32, tile.shape, 0)
    ki = lax.broadcasted_iota(jnp.int32, tile.shape, 1)
    return jnp.where(ki <= qi, tile, NEG)


def _bias_kernel(tab_ref, bd_ref, bp_ref, o_ref):
    h = pl.program_id(0)
    o_ref[...] = jnp.concatenate(
        [_bias_tile(bp_ref[...], tab_ref, h),
         _causal_tile(_bias_tile(bd_ref[...], tab_ref, h))], axis=1)


def bias_tiles(tab, bkt_diag, bkt_prev):
    n_heads = tab.shape[0]
    return pl.pallas_call(
        _bias_kernel,
        grid=(n_heads,),
        in_specs=[
            pl.BlockSpec(memory_space=pltpu.SMEM),
            pl.BlockSpec((BLK, BLK), lambda h: (0, 0)),
            pl.BlockSpec((BLK, BLK), lambda h: (0, 0)),
        ],
        out_specs=pl.BlockSpec((None, BLK, 2 * BLK), lambda h: (h, 0, 0)),
        out_shape=jax.ShapeDtypeStruct((n_heads, BLK, 2 * BLK), F32),
        compiler_params=_params(1),
        name="bias_tiles",
    )(tab, bkt_diag, bkt_prev)


def _wide(x, width):
    return jnp.tile(x, (1, width // HEAD_DIM))


def _softmax_step(s, v, m_ref, acc_ref, *, first, shift=None):
    v_aug = jnp.concatenate([v, jnp.ones(v.shape, v.dtype)], axis=1)
    m_cur = jnp.max(s, axis=-1, keepdims=True)
    if shift is not None:
        m_cur = m_cur + shift
    if first:
        m_new = jnp.broadcast_to(m_cur, m_ref.shape)
    else:
        m_old = m_ref[...]
        m_new = jnp.maximum(m_old, m_cur)
    p = jnp.exp2(s - _wide(m_new if shift is None else m_new - shift, s.shape[1]))
    pv = jnp.dot(p.astype(BF16), v_aug, preferred_element_type=F32)
    if first:
        acc_ref[...] = pv
    else:
        acc_ref[...] = _wide(jnp.exp2(m_old - m_new), acc_ref.shape[1]) * acc_ref[...] + pv
    m_ref[...] = m_new


def _softmax_result(acc_ref):
    acc = acc_ref[...]
    return acc[:, :HEAD_DIM] / acc[:, HEAD_DIM:]


def _head_cols(hh):
    return slice(hh * HEAD_DIM, (hh + 1) * HEAD_DIM)


def _moba_kernel(tab_ref, bias_ref, q_ref, k_ref, v_ref, o_ref,
                 kmean_ref, qa_ref, m_ref, acc_ref, *, gate_fill, heads):
    hg, qb = pl.program_id(0), pl.program_id(2)
    group, seq, _ = k_ref.shape
    nb = seq // BLK
    chains = [(hh, g) for hh in range(heads) for g in range(group)]

    @pl.when(qb == 0)
    def _():
        for c, (hh, g) in enumerate(chains):
            kf = k_ref[g, :, _head_cols(hh)].astype(F32).reshape(nb, BLK, HEAD_DIM)
            kmean_ref[c] = jnp.mean(kf, axis=1)

    row = lax.broadcasted_iota(jnp.int32, (nb, BLK), 0)
    past = row < qb
    n_sel = max(1, min(MOBA_TOPK, nb - 1))
    for c, (hh, g) in enumerate(chains):
        q = q_ref[g, :, _head_cols(hh)]
        gate = lax.dot_general(kmean_ref[c], q.astype(F32), NT_DIMS,
                               precision=lax.Precision.HIGHEST, preferred_element_type=F32)
        gate = jnp.where(past, gate, gate_fill)
        beaten_by = jnp.zeros(gate.shape, F32)
        for i in range(nb):
            gi = gate[i:i + 1, :]
            wins = (gi > gate) | ((gi == gate) & (i < row))
            beaten_by = beaten_by + jnp.where(wins, 1.0, 0.0)
        keep = ((beaten_by < n_sel) & past) | (row == qb)
        sel_neg = jnp.where(keep, 0.0, NEG)
        sel_neg = jnp.concatenate([sel_neg, jnp.zeros((HEAD_DIM - nb, BLK), F32)], axis=0)
        qa_ref[c] = jnp.concatenate([q, sel_neg.T.astype(BF16)], axis=1)


    def span_scores(c, j, n):
        hh, g = chains[c]
        rows = pl.ds(pl.multiple_of(j * BLK, BLK), n * BLK)
        kj = k_ref[g, rows, _head_cols(hh)]
        lane = lax.broadcasted_iota(jnp.int32, (n * BLK, HEAD_DIM), 1)
        blk = j + lax.broadcasted_iota(jnp.int32, (n * BLK, HEAD_DIM), 0) // BLK
        onehot = jnp.where(lane == blk, 1.0, 0.0).astype(BF16)
        k_aug = jnp.concatenate([kj, onehot], axis=1)
        s = lax.dot_general(qa_ref[c], k_aug, NT_DIMS, preferred_element_type=F32)
        return s, v_ref[g, rows, _head_cols(hh)]

    @pl.when(qb == 0)
    def _():
        for c, (hh, g) in enumerate(chains):
            s, v = span_scores(c, qb, 1)
            _softmax_step(s + bias_ref[hh, :, BLK:], v, m_ref.at[c], acc_ref.at[c], first=True)

    @pl.when(qb >= 1)
    def _():
        for c, (hh, g) in enumerate(chains):
            s, v = span_scores(c, qb - 1, 2)
            _softmax_step(s + bias_ref[hh], v, m_ref.at[c], acc_ref.at[c], first=True)

    far_bias = [tab_ref[hg * heads + hh, N_BUCKETS - 1] * LOG2E for hh in range(heads)]

    def far_pair(p, carry):
        for c, (hh, g) in enumerate(chains):
            s, v = span_scores(c, 2 * p, 2)
            _softmax_step(s, v, m_ref.at[c], acc_ref.at[c], first=False, shift=far_bias[hh])
        return carry

    lax.fori_loop(0, jnp.maximum(qb - 1, 0) // 2, far_pair, 0)

    @pl.when((qb >= 2) & (qb % 2 == 0))
    def _():
        for c, (hh, g) in enumerate(chains):
            s, v = span_scores(c, qb - 2, 1)
            _softmax_step(s, v, m_ref.at[c], acc_ref.at[c], first=False, shift=far_bias[hh])

    for c, (hh, g) in enumerate(chains):
        o_ref[g, :, _head_cols(hh)] = _softmax_result(acc_ref.at[c]).astype(o_ref.dtype)


def moba_attention(proj, tab, bias, *, group, heads, gate_fill):
    batch, seq, _ = proj.shape
    nqb = seq // BLK
    k_off = N_SELF_HEADS // heads
    width = heads * HEAD_DIM
    chains = heads * group
    kern = functools.partial(_moba_kernel, gate_fill=gate_fill, heads=heads)
    return pl.pallas_call(
        kern,
        grid=(N_SELF_HEADS // heads, batch // group, nqb),
        in_specs=[
            pl.BlockSpec(memory_space=pltpu.SMEM),
            pl.BlockSpec((heads, BLK, 2 * BLK), lambda h, b, qb: (h, 0, 0)),
            pl.BlockSpec((group, BLK, width), lambda h, b, qb: (b, qb, h)),
            pl.BlockSpec((group, seq, width), lambda h, b, qb: (b, 0, k_off + h)),
            pl.BlockSpec((group, seq, width), lambda h, b, qb: (b, 0, 2 * k_off + h)),
        ],
        out_specs=pl.BlockSpec((group, BLK, width), lambda h, b, qb: (b, qb, h)),
        out_shape=jax.ShapeDtypeStruct((batch, seq, SELF_WIDTH), BF16),
        scratch_shapes=[
            pltpu.VMEM((chains, seq // BLK, HEAD_DIM), F32),
            pltpu.VMEM((chains, BLK, 2 * HEAD_DIM), BF16),
            pltpu.VMEM((chains, BLK, HEAD_DIM), F32), pltpu.VMEM((chains, BLK, 2 * HEAD_DIM), F32),
        ],
        compiler_params=_params(3),
        name="moba_attention",
    )(tab, bias, proj, proj, proj)


def _twice(tile):
    return jnp.concatenate([tile, tile], axis=0)


def _diff_kernel(tab_ref, bias_ref, lq_ref, gs_ref, q_ref, k_ref, v_ref, o_ref,
                 q2_ref, m_ref, acc_ref, *, lam_init, heads):
    hg, qb = pl.program_id(0), pl.program_id(2)
    group = k_ref.shape[0]
    chains = [(hh, g) for hh in range(heads) for g in range(group)]

    lane = lax.broadcasted_iota(jnp.int32, (BLK, HEAD_DIM), 1)
    for c, (hh, g) in enumerate(chains):
        q = q_ref[g, :, _head_cols(hh)]
        zero = jnp.zeros_like(q)
        q2_ref[c] = jnp.concatenate([jnp.where(lane < DIFF_HALF, q, zero),
                                     jnp.where(lane >= DIFF_HALF, q, zero)], axis=0)

    def span_scores(c, j, n):
        hh, g = chains[c]
        rows = pl.ds(pl.multiple_of(j * BLK, BLK), n * BLK)
        s = lax.dot_general(q2_ref[c], k_ref[g, rows, _head_cols(hh)], NT_DIMS,
                            preferred_element_type=F32)
        return s, v_ref[g, rows, _head_cols(hh)]

    @pl.when(qb == 0)
    def _():
        for c, (hh, g) in enumerate(chains):
            s, v = span_scores(c, qb, 1)
            _softmax_step(s + _twice(bias_ref[hh, :, BLK:]), v, m_ref.at[c], acc_ref.at[c],
                          first=True)

    @pl.when(qb >= 1)
    def _():
        for c, (hh, g) in enumerate(chains):
            s, v = span_scores(c, qb - 1, 2)
            _softmax_step(s + _twice(bias_ref[hh]), v, m_ref.at[c], acc_ref.at[c], first=True)

    far_bias = [tab_ref[hg * heads + hh, N_BUCKETS - 1] * LOG2E for hh in range(heads)]

    def far_pair(p, carry):
        for c, (hh, g) in enumerate(chains):
            s, v = span_scores(c, 2 * p, 2)
            _softmax_step(s, v, m_ref.at[c], acc_ref.at[c], first=False, shift=far_bias[hh])
        return carry

    lax.fori_loop(0, jnp.maximum(qb - 1, 0) // 2, far_pair, 0)

    @pl.when((qb >= 2) & (qb % 2 == 0))
    def _():
        for c, (hh, g) in enumerate(chains):
            s, v = span_scores(c, qb - 2, 1)
            _softmax_step(s, v, m_ref.at[c], acc_ref.at[c], first=False, shift=far_bias[hh])

    lq = lq_ref[...]
    lam = (jnp.exp(jnp.sum(lq[0:1] * lq[1:2], axis=-1, keepdims=True))
           - jnp.exp(jnp.sum(lq[2:3] * lq[3:4], axis=-1, keepdims=True)) + lam_init)
    for c, (hh, g) in enumerate(chains):
        o_all = _softmax_result(acc_ref.at[c])
        o = o_all[:BLK] - lam * o_all[BLK:]
        o_ref[g, :, _head_cols(hh)] = (
            _rmsnorm_rows(o, gs_ref[...]) * (1.0 - lam_init)).astype(o_ref.dtype)


def diff_attention(proj, kv, tab, bias, lq, g_subln, *, group, heads, lam_init):
    batch, seq, _ = proj.shape
    nqb = seq // BLK
    v_off = N_SELF_HEADS // heads
    width = heads * HEAD_DIM
    chains = heads * group
    kern = functools.partial(_diff_kernel, lam_init=lam_init, heads=heads)
    return pl.pallas_call(
        kern,
        grid=(N_SELF_HEADS // heads, batch // group, nqb),
        in_specs=[
            pl.BlockSpec(memory_space=pltpu.SMEM),
            pl.BlockSpec((heads, BLK, 2 * BLK), lambda h, b, qb: (h, 0, 0)),
            pl.BlockSpec((4, DIFF_HALF), lambda h, b, qb: (0, 0)),
            pl.BlockSpec((1, HEAD_DIM), lambda h, b, qb: (0, 0)),
            pl.BlockSpec((group, BLK, width), lambda h, b, qb: (b, qb, h)),
            pl.BlockSpec((group, seq, width), lambda h, b, qb: (b, 0, h)),
            pl.BlockSpec((group, seq, width), lambda h, b, qb: (b, 0, v_off + h)),
        ],
        out_specs=pl.BlockSpec((group, BLK, width), lambda h, b, qb: (b, qb, h)),
        out_shape=jax.ShapeDtypeStruct((batch, seq, SELF_WIDTH), BF16),
        scratch_shapes=[
            pltpu.VMEM((chains, 2 * BLK, HEAD_DIM), BF16),
            pltpu.VMEM((chains, 2 * BLK, HEAD_DIM), F32),
            pltpu.VMEM((chains, 2 * BLK, 2 * HEAD_DIM), F32),
        ],
        compiler_params=_params(3),
        name="diff_attention",
    )(tab, bias, lq, g_subln.reshape(1, HEAD_DIM), proj, kv, kv)


def _mem_kernel(q_ref, mk_ref, mv_ref, o_ref):
    for hh in range(N_MEM_HEADS):
        cols = _head_cols(hh)
        s = lax.dot_general(q_ref[:, cols], mk_ref[:, cols], NT_DIMS, preferred_element_type=F32)
        p = jnp.exp2(s - jnp.max(s, axis=-1, keepdims=True))
        mv = mv_ref[:, cols]
        pv = jnp.dot(p.astype(BF16), jnp.concatenate([mv, jnp.ones(mv.shape, mv.dtype)], axis=1),
                     preferred_element_type=F32)
        o_ref[:, cols] = (pv[:, :HEAD_DIM] / pv[:, HEAD_DIM:]).astype(o_ref.dtype)


def mem_attention(proj, mkv, *, batch, seq, n_mem, q_col, tq):
    nsb = seq // tq
    q_blk = q_col // MEM_WIDTH
    return pl.pallas_call(
        _mem_kernel,
        grid=(batch, nsb),
        in_specs=[
            pl.BlockSpec((tq, MEM_WIDTH), lambda b, sb: (b * nsb + sb, q_blk)),
            pl.BlockSpec((n_mem, MEM_WIDTH), lambda b, sb: (b, 0)),
            pl.BlockSpec((n_mem, MEM_WIDTH), lambda b, sb: (b, 1)),
        ],
        out_specs=pl.BlockSpec((tq, MEM_WIDTH), lambda b, sb: (b * nsb + sb, 0)),
        out_shape=jax.ShapeDtypeStruct((batch * seq, MEM_WIDTH), BF16),
        compiler_params=_params(2),
        name="mem_attention",
    )(proj, mkv, mkv)


ATTN_GROUP = 4
ATTN_HEADS = 4


def kernel(x, mem, rel_bias, g_mix, w_in_a, w_in_b, g_mem, w_mem_kv, w_o, g_ffn, w_gate_up, w_down,
           g_kv, w_kv_shared, lambda_qk, g_subln, g_final):
    batch, seq, d = x.shape
    n_mem = mem.shape[1]
    depth = g_mix.shape[0]
    n_a = w_in_a.shape[0]
    tokens = batch * seq
    group = math.gcd(batch, ATTN_GROUP)

    w_kv_shared = w_kv_shared[None]

    tab = rel_bias.T.astype(F32)
    bias = bias_tiles(tab, *_bucket_tiles())

    moba_scale = HEAD_DIM ** -0.5 * LOG2E
    diff_scale = DIFF_HALF ** -0.5 * LOG2E
    mem_scale = HEAD_DIM ** -0.5 * LOG2E
    ones = functools.partial(jnp.ones, dtype=F32)
    cs_a = jnp.concatenate([jnp.full((SELF_WIDTH,), moba_scale, F32), ones((2 * SELF_WIDTH,)),
                            jnp.full((MEM_WIDTH,), mem_scale, F32)])
    cs_b = jnp.concatenate([jnp.full((SELF_WIDTH,), diff_scale, F32),
                            jnp.full((MEM_WIDTH,), mem_scale, F32)])

    h = x.reshape(tokens, d)
    mem2 = mem.reshape(batch * n_mem, d)
    kv = None
    for l in range(depth):
        mkv = norm_matmul(mem2, g_mem[l], w_mem_kv, l, ones((2 * MEM_WIDTH,)),
                          tm=batch * n_mem, tn=512)
        if l < n_a:
            proj = norm_matmul(h, g_mix[l], w_in_a, l, cs_a, tm=1024, tn=1024)
            y_self = moba_attention(proj.reshape(batch, seq, -1), tab, bias,
                                    group=group, heads=ATTN_HEADS, gate_fill=NEG * moba_scale)
            q_col = 3 * SELF_WIDTH
        else:
            j = l - n_a
            if kv is None:
                kv = norm_matmul(h, g_kv, w_kv_shared, 0, ones((2 * SELF_WIDTH,)), tm=1024, tn=1024)
                kv = kv.reshape(batch, seq, -1)
            proj = norm_matmul(h, g_mix[l], w_in_b, j, cs_b, tm=1024, tn=1024)
            lam_init = 0.8 - 0.6 * math.exp(-0.3 * l)
            y_self = diff_attention(proj.reshape(batch, seq, -1), kv, tab, bias,
                                    lambda_qk[j], g_subln[j], group=group, heads=ATTN_HEADS,
                                    lam_init=lam_init)
            q_col = SELF_WIDTH
        y_mem = mem_attention(proj, mkv, batch=batch, seq=seq, n_mem=n_mem,
                              q_col=q_col, tq=1024)
        h = residual_matmul2(h, y_self.reshape(tokens, -1), y_mem, w_o, l, tm=2048, tn=512)
        h = ffn_block(h, g_ffn[l], w_gate_up, w_down, l, tm=1024, tc=256,
                      final_gain=g_final if l == depth - 1 else None)
    return h.reshape(batch, seq, d)
```

```python
import functools
import math

import jax
import jax.numpy as jnp
import numpy as np
from jax import lax
from jax.experimental import pallas as pl
from jax.experimental.pallas import tpu as pltpu

D_MODEL = 2048
HEAD_DIM = 128
N_MEM_HEADS = 4
MEM_WIDTH = N_MEM_HEADS * HEAD_DIM
SELF_WIDTH = D_MODEL - MEM_WIDTH
N_SELF_HEADS = SELF_WIDTH // HEAD_DIM
DIFF_HALF = HEAD_DIM // 2
MOBA_BLOCK = 256
MOBA_TOPK = 3
N_BUCKETS = 32
MAX_DISTANCE = 128
RMS_EPS = 1e-6
NEG = -1e30
LOG2E = math.log2(math.e)

BLK = MOBA_BLOCK
V7X_VMEM_BYTES = 64 * 1024 * 1024
VMEM_LIMIT = V7X_VMEM_BYTES - 4 * 1024 * 1024

F32 = jnp.float32
BF16 = jnp.bfloat16
NT_DIMS = (((1,), (1,)), ((), ()))


def _params(n_grid_axes):
    return pltpu.CompilerParams(
        dimension_semantics=("arbitrary",) * n_grid_axes, vmem_limit_bytes=VMEM_LIMIT)


def _rmsnorm_rows(x, g):
    ms = jnp.mean(x * x, axis=-1, keepdims=True)
    return x * lax.rsqrt(ms + RMS_EPS) * g


ROW_PARTS = 4


def _staggered_row_specs(tm, d, n_tiles, n_steps):
    part = tm // ROW_PARTS
    last = n_tiles * ROW_PARTS - 1
    specs = []
    for q in range(ROW_PARTS):
        s_q = 1 + q * n_steps // ROW_PARTS

        def index(i, c, q=q, s_q=s_q):
            ahead = jnp.where(c >= s_q, 1, 0) if s_q < n_steps else 0
            return (jnp.minimum(ROW_PARTS * (i + ahead) + q, last), 0)

        specs.append(pl.BlockSpec((part, d), index))
    return specs


def _row_part(ref, q):
    part = ref.shape[0] // ROW_PARTS
    return slice(q * part, (q + 1) * part)


def _norm_matmul_kernel(*refs):
    x_parts = refs[:ROW_PARTS]
    g_ref, w_ref, cs_ref, o_ref, xn_ref = refs[ROW_PARTS:]

    @pl.when(pl.program_id(1) == 0)
    def _():
        for q, x_ref in enumerate(x_parts):
            xn_ref[_row_part(xn_ref, q), :] = _rmsnorm_rows(x_ref[...], g_ref[...]).astype(BF16)

    acc = jnp.dot(xn_ref[...], w_ref[...].astype(BF16), preferred_element_type=F32)
    o_ref[...] = (acc * cs_ref[...]).astype(o_ref.dtype)


def norm_matmul(x, g, w, layer, col_scale, *, tm, tn):
    m, d = x.shape
    n = w.shape[-1]
    return pl.pallas_call(
        _norm_matmul_kernel,
        grid=(m // tm, n // tn),
        in_specs=_staggered_row_specs(tm, d, m // tm, n // tn) + [
            pl.BlockSpec((1, d), lambda i, j: (0, 0)),
            pl.BlockSpec((None, d, tn), lambda i, j: (layer, 0, j)),
            pl.BlockSpec((1, tn), lambda i, j: (0, j)),
        ],
        out_specs=pl.BlockSpec((tm, tn), lambda i, j: (i, j)),
        out_shape=jax.ShapeDtypeStruct((m, n), BF16),
        scratch_shapes=[pltpu.VMEM((tm, d), BF16)],
        compiler_params=_params(2),
        name="norm_matmul",
    )(*([x] * ROW_PARTS), g.reshape(1, d), w, col_scale.reshape(1, n))


def _ffn_kernel(*refs, n_slab, final_norm):
    h_parts = refs[:ROW_PARTS]
    g_ref, gf_ref, wg_ref, wu_ref, wd_ref, o_ref, xn_ref = refs[ROW_PARTS:]
    c = pl.program_id(1)

    @pl.when(c == 0)
    def _():
        for q, h_ref in enumerate(h_parts):
            h = h_ref[...]
            xn_ref[_row_part(xn_ref, q), :] = _rmsnorm_rows(h, g_ref[...]).astype(BF16)
            o_ref[_row_part(o_ref, q), :] = h

    xn = xn_ref[...]
    gate = jnp.dot(xn, wg_ref[...].astype(BF16), preferred_element_type=F32)
    up = jnp.dot(xn, wu_ref[...].astype(BF16), preferred_element_type=F32)
    act = (gate * jax.nn.sigmoid(gate) * up).astype(BF16)
    slab = o_ref.shape[1] // n_slab
    for n in range(n_slab):
        cols = slice(n * slab, (n + 1) * slab)
        o_ref[:, cols] += jnp.dot(act, wd_ref[:, cols].astype(BF16), preferred_element_type=F32)

    if final_norm:
        @pl.when(c == pl.num_programs(1) - 1)
        def _():
            o_ref[...] = _rmsnorm_rows(o_ref[...], gf_ref[...])


def ffn_block(h, g, w_gate_up, w_down, layer, *, tm, tc, final_gain=None):
    m, d = h.shape
    d_ff = w_down.shape[1]
    nc = d_ff // tc
    final_norm = final_gain is not None
    gf = final_gain if final_norm else g
    return pl.pallas_call(
        functools.partial(_ffn_kernel, n_slab=4, final_norm=final_norm),
        grid=(m // tm, nc),
        in_specs=_staggered_row_specs(tm, d, m // tm, nc) + [
            pl.BlockSpec((1, d), lambda i, c: (0, 0)),
            pl.BlockSpec((1, d), lambda i, c: (0, 0)),
            pl.BlockSpec((None, d, tc), lambda i, c: (layer, 0, c)),
            pl.BlockSpec((None, d, tc), lambda i, c: (layer, 0, c + nc)),
            pl.BlockSpec((None, tc, d), lambda i, c: (layer, c, 0)),
        ],
        out_specs=pl.BlockSpec((tm, d), lambda i, c: (i, 0)),
        out_shape=jax.ShapeDtypeStruct((m, d), F32),
        scratch_shapes=[pltpu.VMEM((tm, d), BF16)],
        compiler_params=_params(2),
        name="ffn_block",
    )(*([h] * ROW_PARTS), g.reshape(1, d), gf.reshape(1, d), w_gate_up, w_gate_up, w_down)


def _residual_matmul2_kernel(h_ref, a1_ref, a2_ref, w1_ref, w2_ref, o_ref):
    acc = jnp.dot(a1_ref[...], w1_ref[...].astype(BF16), preferred_element_type=F32)
    acc = acc + jnp.dot(a2_ref[...], w2_ref[...].astype(BF16), preferred_element_type=F32)
    o_ref[...] = h_ref[...] + acc


def residual_matmul2(h, a1, a2, w, layer, *, tm, tn):
    m, n = h.shape
    k1, k2 = a1.shape[1], a2.shape[1]
    assert k1 % k2 == 0
    return pl.pallas_call(
        _residual_matmul2_kernel,
        grid=(m // tm, n // tn),
        in_specs=[
            pl.BlockSpec((tm, tn), lambda i, j: (i, j)),
            pl.BlockSpec((tm, k1), lambda i, j: (i, 0)),
            pl.BlockSpec((tm, k2), lambda i, j: (i, 0)),
            pl.BlockSpec((None, k1, tn), lambda i, j: (layer, 0, j)),
            pl.BlockSpec((None, k2, tn), lambda i, j: (layer, k1 // k2, j)),
        ],
        out_specs=pl.BlockSpec((tm, tn), lambda i, j: (i, j)),
        out_shape=jax.ShapeDtypeStruct((m, n), F32),
        compiler_params=_params(2),
        name="residual_matmul2",
    )(h, a1, a2, w, w)


def _rel_bucket_np(rel):
    n = np.maximum(rel, 0)
    max_exact = N_BUCKETS // 2
    nf = np.maximum(n, 1).astype(np.float32)
    large = max_exact + (np.log(nf / np.float32(max_exact)) / np.float32(math.log(MAX_DISTANCE / max_exact))
                         * np.float32(N_BUCKETS - max_exact)).astype(np.int32)
    large = np.minimum(large, N_BUCKETS - 1)
    return np.where(n < max_exact, n, large).astype(np.int32)


def _bucket_tiles():
    qi = np.arange(BLK, dtype=np.int32)[:, None]
    ki = np.arange(BLK, dtype=np.int32)[None, :]
    diag = _rel_bucket_np(qi - ki)
    prev = _rel_bucket_np(qi - ki + BLK)
    assert _rel_bucket_np(np.array([BLK + 1]))[0] == N_BUCKETS - 1
    return jnp.asarray(diag), jnp.asarray(prev)


def _bias_tile(bkt, tab_ref, head):
    out = jnp.zeros(bkt.shape, F32)
    for b in range(N_BUCKETS):
        out = jnp.where(bkt == b, tab_ref[head, b] * LOG2E, out)
    return out


def _causal_tile(tile):
    qi = lax.broadcasted_iota(jnp.int32, tile.shape, 0)
    ki = lax.broadcasted_iota(jnp.int32, tile.shape, 1)
    return jnp.where(ki <= qi, tile, NEG)


def _bias_kernel(tab_ref, bd_ref, bp_ref, o_ref):
    h = pl.program_id(0)
    o_ref[...] = jnp.concatenate(
        [_bias_tile(bp_ref[...], tab_ref, h),
         _causal_tile(_bias_tile(bd_ref[...], tab_ref, h))], axis=1)


def bias_tiles(tab, bkt_diag, bkt_prev):
    n_heads = tab.shape[0]
    return pl.pallas_call(
        _bias_kernel,
        grid=(n_heads,),
        in_specs=[
            pl.BlockSpec(memory_space=pltpu.SMEM),
            pl.BlockSpec((BLK, BLK), lambda h: (0, 0)),
            pl.BlockSpec((BLK, BLK), lambda h: (0, 0)),
        ],
        out_specs=pl.BlockSpec((None, BLK, 2 * BLK), lambda h: (h, 0, 0)),
        out_shape=jax.ShapeDtypeStruct((n_heads, BLK, 2 * BLK), F32),
        compiler_params=_params(1),
        name="bias_tiles",
    )(tab, bkt_diag, bkt_prev)


def _wide(x, width):
    return jnp.tile(x, (1, width // HEAD_DIM))


def _softmax_step(s, v, m_ref, acc_ref, *, first, shift=None):
    v_aug = jnp.concatenate([v, jnp.ones(v.shape, v.dtype)], axis=1)
    m_cur = jnp.max(s, axis=-1, keepdims=True)
    if shift is not None:
        m_cur = m_cur + shift
    if first:
        m_new = jnp.broadcast_to(m_cur, m_ref.shape)
    else:
        m_old = m_ref[...]
        m_new = jnp.maximum(m_old, m_cur)
    p = jnp.exp2(s - _wide(m_new if shift is None else m_new - shift, s.shape[1]))
    pv = jnp.dot(p.astype(BF16), v_aug, preferred_element_type=F32)
    if first:
        acc_ref[...] = pv
    else:
        acc_ref[...] = _wide(jnp.exp2(m_old - m_new), acc_ref.shape[1]) * acc_ref[...] + pv
    m_ref[...] = m_new


def _softmax_result(acc_ref):
    acc = acc_ref[...]
    return acc[:, :HEAD_DIM] / acc[:, HEAD_DIM:]


def _head_cols(hh):
    return slice(hh * HEAD_DIM, (hh + 1) * HEAD_DIM)


def _moba_kernel(tab_ref, bias_ref, q_ref, k_ref, v_ref, o_ref,
                 kmean_ref, qa_ref, m_ref, acc_ref, *, gate_fill, heads):
    hg, qb = pl.program_id(0), pl.program_id(2)
    group, seq, _ = k_ref.shape
    nb = seq // BLK
    chains = [(hh, g) for hh in range(heads) for g in range(group)]

    @pl.when(qb == 0)
    def _():
        for c, (hh, g) in enumerate(chains):
            kf = k_ref[g, :, _head_cols(hh)].astype(F32).reshape(nb, BLK, HEAD_DIM)
            km = jnp.mean(kf, axis=1)
            km_hi = km.astype(BF16)
            km_lo = (km - km_hi.astype(F32)).astype(BF16)
            kmean_ref[c] = jnp.concatenate([km_hi, km_lo], axis=0)

    row = lax.broadcasted_iota(jnp.int32, (nb, BLK), 0)
    past = row < qb
    n_sel = max(1, min(MOBA_TOPK, nb - 1))
    for c, (hh, g) in enumerate(chains):
        q = q_ref[g, :, _head_cols(hh)]
        parts = lax.dot_general(kmean_ref[c], q, NT_DIMS, preferred_element_type=F32)
        gate = jnp.where(past, parts[:nb] + parts[nb:], gate_fill)
        beaten_by = jnp.zeros(gate.shape, F32)
        for i in range(nb):
            gi = gate[i:i + 1, :]
            wins = (gi > gate) | ((gi == gate) & (i < row))
            beaten_by = beaten_by + jnp.where(wins, 1.0, 0.0)
        keep = ((beaten_by < n_sel) & past) | (row == qb)
        sel_neg = jnp.where(keep, 0.0, NEG)
        sel_neg = jnp.concatenate([sel_neg, jnp.zeros((HEAD_DIM - nb, BLK), F32)], axis=0)
        qa_ref[c] = jnp.concatenate([q, sel_neg.T.astype(BF16)], axis=1)


    def span_scores(c, j, n):
        hh, g = chains[c]
        rows = pl.ds(pl.multiple_of(j * BLK, BLK), n * BLK)
        kj = k_ref[g, rows, _head_cols(hh)]
        lane = lax.broadcasted_iota(jnp.int32, (n * BLK, HEAD_DIM), 1)
        blk = j + lax.broadcasted_iota(jnp.int32, (n * BLK, HEAD_DIM), 0) // BLK
        onehot = jnp.where(lane == blk, 1.0, 0.0).astype(BF16)
        k_aug = jnp.concatenate([kj, onehot], axis=1)
        s = lax.dot_general(qa_ref[c], k_aug, NT_DIMS, preferred_element_type=F32)
        return s, v_ref[g, rows, _head_cols(hh)]

    @pl.when(qb == 0)
    def _():
        for c, (hh, g) in enumerate(chains):
            s, v = span_scores(c, qb, 1)
            _softmax_step(s + bias_ref[hh, :, BLK:], v, m_ref.at[c], acc_ref.at[c], first=True)

    @pl.when(qb >= 1)
    def _():
        for c, (hh, g) in enumerate(chains):
            s, v = span_scores(c, qb - 1, 2)
            _softmax_step(s + bias_ref[hh], v, m_ref.at[c], acc_ref.at[c], first=True)

    far_bias = [tab_ref[hg * heads + hh, N_BUCKETS - 1] * LOG2E for hh in range(heads)]

    def far_pair(p, carry):
        for c, (hh, g) in enumerate(chains):
            s, v = span_scores(c, 2 * p, 2)
            _softmax_step(s, v, m_ref.at[c], acc_ref.at[c], first=False, shift=far_bias[hh])
        return carry

    lax.fori_loop(0, jnp.maximum(qb - 1, 0) // 2, far_pair, 0)

    @pl.when((qb >= 2) & (qb % 2 == 0))
    def _():
        for c, (hh, g) in enumerate(chains):
            s, v = span_scores(c, qb - 2, 1)
            _softmax_step(s, v, m_ref.at[c], acc_ref.at[c], first=False, shift=far_bias[hh])

    for c, (hh, g) in enumerate(chains):
        o_ref[g, :, _head_cols(hh)] = _softmax_result(acc_ref.at[c]).astype(o_ref.dtype)


def moba_attention(proj, tab, bias, *, group, heads, gate_fill):
    batch, seq, _ = proj.shape
    nqb = seq // BLK
    k_off = N_SELF_HEADS // heads
    width = heads * HEAD_DIM
    chains = heads * group
    kern = functools.partial(_moba_kernel, gate_fill=gate_fill, heads=heads)
    return pl.pallas_call(
        kern,
        grid=(N_SELF_HEADS // heads, batch // group, nqb),
        in_specs=[
            pl.BlockSpec(memory_space=pltpu.SMEM),
            pl.BlockSpec((heads, BLK, 2 * BLK), lambda h, b, qb: (h, 0, 0)),
            pl.BlockSpec((group, BLK, width), lambda h, b, qb: (b, qb, h)),
            pl.BlockSpec((group, seq, width), lambda h, b, qb: (b, 0, k_off + h)),
            pl.BlockSpec((group, seq, width), lambda h, b, qb: (b, 0, 2 * k_off + h)),
        ],
        out_specs=pl.BlockSpec((group, BLK, width), lambda h, b, qb: (b, qb, h)),
        out_shape=jax.ShapeDtypeStruct((batch, seq, SELF_WIDTH), BF16),
        scratch_shapes=[
            pltpu.VMEM((chains, 2 * (seq // BLK), HEAD_DIM), BF16),
            pltpu.VMEM((chains, BLK, 2 * HEAD_DIM), BF16),
            pltpu.VMEM((chains, BLK, HEAD_DIM), F32), pltpu.VMEM((chains, BLK, 2 * HEAD_DIM), F32),
        ],
        compiler_params=_params(3),
        name="moba_attention",
    )(tab, bias, proj, proj, proj)


def _twice(tile):
    return jnp.concatenate([tile, tile], axis=0)


def _diff_kernel(tab_ref, bias_ref, lq_ref, gs_ref, q_ref, k_ref, v_ref, o_ref,
                 q2_ref, m_ref, acc_ref, *, lam_init, heads):
    hg, qb = pl.program_id(0), pl.program_id(2)
    group = k_ref.shape[0]
    chains = [(hh, g) for hh in range(heads) for g in range(group)]

    lane = lax.broadcasted_iota(jnp.int32, (BLK, HEAD_DIM), 1)
    for c, (hh, g) in enumerate(chains):
        q = q_ref[g, :, _head_cols(hh)]
        zero = jnp.zeros_like(q)
        q2_ref[c] = jnp.concatenate([jnp.where(lane < DIFF_HALF, q, zero),
                                     jnp.where(lane >= DIFF_HALF, q, zero)], axis=0)

    def span_scores(c, j, n):
        hh, g = chains[c]
        rows = pl.ds(pl.multiple_of(j * BLK, BLK), n * BLK)
        s = lax.dot_general(q2_ref[c], k_ref[g, rows, _head_cols(hh)], NT_DIMS,
                            preferred_element_type=F32)
        return s, v_ref[g, rows, _head_cols(hh)]

    @pl.when(qb == 0)
    def _():
        for c, (hh, g) in enumerate(chains):
            s, v = span_scores(c, qb, 1)
            _softmax_step(s + _twice(bias_ref[hh, :, BLK:]), v, m_ref.at[c], acc_ref.at[c],
                          first=True)

    @pl.when(qb >= 1)
    def _():
        for c, (hh, g) in enumerate(chains):
            s, v = span_scores(c, qb - 1, 2)
            _softmax_step(s + _twice(bias_ref[hh]), v, m_ref.at[c], acc_ref.at[c], first=True)

    far_bias = [tab_ref[hg * heads + hh, N_BUCKETS - 1] * LOG2E for hh in range(heads)]

    def far_pair(p, carry):
        for c, (hh, g) in enumerate(chains):
            s, v = span_scores(c, 2 * p, 2)
            _softmax_step(s, v, m_ref.at[c], acc_ref.at[c], first=False, shift=far_bias[hh])
        return carry

    lax.fori_loop(0, jnp.maximum(qb - 1, 0) // 2, far_pair, 0)

    @pl.when((qb >= 2) & (qb % 2 == 0))
    def _():
        for c, (hh, g) in enumerate(chains):
            s, v = span_scores(c, qb - 2, 1)
            _softmax_step(s, v, m_ref.at[c], acc_ref.at[c], first=False, shift=far_bias[hh])

    lq = lq_ref[...]
    lam = (jnp.exp(jnp.sum(lq[0:1] * lq[1:2], axis=-1, keepdims=True))
           - jnp.exp(jnp.sum(lq[2:3] * lq[3:4], axis=-1, keepdims=True)) + lam_init)
    for c, (hh, g) in enumerate(chains):
        o_all = _softmax_result(acc_ref.at[c])
        o = o_all[:BLK] - lam * o_all[BLK:]
        o_ref[g, :, _head_cols(hh)] = (
            _rmsnorm_rows(o, gs_ref[...]) * (1.0 - lam_init)).astype(o_ref.dtype)


def diff_attention(proj, kv, tab, bias, lq, g_subln, *, group, heads, lam_init):
    batch, seq, _ = proj.shape
    nqb = seq // BLK
    v_off = N_SELF_HEADS // heads
    width = heads * HEAD_DIM
    chains = heads * group
    kern = functools.partial(_diff_kernel, lam_init=lam_init, heads=heads)
    return pl.pallas_call(
        kern,
        grid=(N_SELF_HEADS // heads, batch // group, nqb),
        in_specs=[
            pl.BlockSpec(memory_space=pltpu.SMEM),
            pl.BlockSpec((heads, BLK, 2 * BLK), lambda h, b, qb: (h, 0, 0)),
            pl.BlockSpec((4, DIFF_HALF), lambda h, b, qb: (0, 0)),
            pl.BlockSpec((1, HEAD_DIM), lambda h, b, qb: (0, 0)),
            pl.BlockSpec((group, BLK, width), lambda h, b, qb: (b, qb, h)),
            pl.BlockSpec((group, seq, width), lambda h, b, qb: (b, 0, h)),
            pl.BlockSpec((group, seq, width), lambda h, b, qb: (b, 0, v_off + h)),
        ],
        out_specs=pl.BlockSpec((group, BLK, width), lambda h, b, qb: (b, qb, h)),
        out_shape=jax.ShapeDtypeStruct((batch, seq, SELF_WIDTH), BF16),
        scratch_shapes=[
            pltpu.VMEM((chains, 2 * BLK, HEAD_DIM), BF16),
            pltpu.VMEM((chains, 2 * BLK, HEAD_DIM), F32),
            pltpu.VMEM((chains, 2 * BLK, 2 * HEAD_DIM), F32),
        ],
        compiler_params=_params(3),
        name="diff_attention",
    )(tab, bias, lq, g_subln.reshape(1, HEAD_DIM), proj, kv, kv)


def _mem_kernel(q_ref, mk_ref, mv_ref, o_ref):
    for hh in range(N_MEM_HEADS):
        cols = _head_cols(hh)
        s = lax.dot_general(q_ref[:, cols], mk_ref[:, cols], NT_DIMS, preferred_element_type=F32)
        p = jnp.exp2(s - jnp.max(s, axis=-1, keepdims=True))
        mv = mv_ref[:, cols]
        pv = jnp.dot(p.astype(BF16), jnp.concatenate([mv, jnp.ones(mv.shape, mv.dtype)], axis=1),
                     preferred_element_type=F32)
        o_ref[:, cols] = (pv[:, :HEAD_DIM] / pv[:, HEAD_DIM:]).astype(o_ref.dtype)


def _mem_kv_kernel(x_ref, g_ref, w_ref, o_ref, xn_ref):
    @pl.when(pl.program_id(1) == 0)
    def _():
        xn_ref[...] = _rmsnorm_rows(x_ref[...], g_ref[...]).astype(BF16)

    o_ref[...] = jnp.dot(xn_ref[...], w_ref[...].astype(BF16),
                         preferred_element_type=F32).astype(o_ref.dtype)


def mem_kv(mem, g_mem, w_mem_kv, *, tn):
    m, d = mem.shape
    depth, _, n = w_mem_kv.shape
    return pl.pallas_call(
        _mem_kv_kernel,
        grid=(depth, n // tn),
        in_specs=[
            pl.BlockSpec((m, d), lambda l, j: (0, 0)),
            pl.BlockSpec((None, 1, d), lambda l, j: (l, 0, 0)),
            pl.BlockSpec((None, d, tn), lambda l, j: (l, 0, j)),
        ],
        out_specs=pl.BlockSpec((None, m, tn), lambda l, j: (l, 0, j)),
        out_shape=jax.ShapeDtypeStruct((depth, m, n), BF16),
        scratch_shapes=[pltpu.VMEM((m, d), BF16)],
        compiler_params=_params(2),
        name="mem_kv",
    )(mem, g_mem.reshape(depth, 1, d), w_mem_kv)


def mem_attention(proj, mkv, layer, *, batch, seq, n_mem, q_col, tq):
    nsb = seq // tq
    q_blk = q_col // MEM_WIDTH
    return pl.pallas_call(
        _mem_kernel,
        grid=(batch, nsb),
        in_specs=[
            pl.BlockSpec((tq, MEM_WIDTH), lambda b, sb: (b * nsb + sb, q_blk)),
            pl.BlockSpec((None, n_mem, MEM_WIDTH), lambda b, sb: (layer, b, 0)),
            pl.BlockSpec((None, n_mem, MEM_WIDTH), lambda b, sb: (layer, b, 1)),
        ],
        out_specs=pl.BlockSpec((tq, MEM_WIDTH), lambda b, sb: (b * nsb + sb, 0)),
        out_shape=jax.ShapeDtypeStruct((batch * seq, MEM_WIDTH), BF16),
        compiler_params=_params(2),
        name="mem_attention",
    )(proj, mkv, mkv)


ATTN_GROUP = 4
ATTN_HEADS = 4


def kernel(x, mem, rel_bias, g_mix, w_in_a, w_in_b, g_mem, w_mem_kv, w_o, g_ffn, w_gate_up, w_down,
           g_kv, w_kv_shared, lambda_qk, g_subln, g_final):
    batch, seq, d = x.shape
    n_mem = mem.shape[1]
    depth = g_mix.shape[0]
    n_a = w_in_a.shape[0]
    tokens = batch * seq
    group = math.gcd(batch, ATTN_GROUP)

    w_kv_shared = w_kv_shared[None]

    tab = rel_bias.T.astype(F32)
    bias = bias_tiles(tab, *_bucket_tiles())

    moba_scale = HEAD_DIM ** -0.5 * LOG2E
    diff_scale = DIFF_HALF ** -0.5 * LOG2E
    mem_scale = HEAD_DIM ** -0.5 * LOG2E
    ones = functools.partial(jnp.ones, dtype=F32)
    cs_a = jnp.concatenate([jnp.full((SELF_WIDTH,), moba_scale, F32), ones((2 * SELF_WIDTH,)),
                            jnp.full((MEM_WIDTH,), mem_scale, F32)])
    cs_b = jnp.concatenate([jnp.full((SELF_WIDTH,), diff_scale, F32),
                            jnp.full((MEM_WIDTH,), mem_scale, F32)])

    h = x.reshape(tokens, d)
    mem2 = mem.reshape(batch * n_mem, d)
    mkv = mem_kv(mem2, g_mem, w_mem_kv, tn=512)
    kv = None
    for l in range(depth):
        if l < n_a:
            proj = norm_matmul(h, g_mix[l], w_in_a, l, cs_a, tm=1024, tn=1024)
            y_self = moba_attention(proj.reshape(batch, seq, -1), tab, bias,
                                    group=group, heads=ATTN_HEADS, gate_fill=NEG * moba_scale)
            q_col = 3 * SELF_WIDTH
        else:
            j = l - n_a
            if kv is None:
                kv = norm_matmul(h, g_kv, w_kv_shared, 0, ones((2 * SELF_WIDTH,)), tm=1024, tn=1024)
                kv = kv.reshape(batch, seq, -1)
            proj = norm_matmul(h, g_mix[l], w_in_b, j, cs_b, tm=1024, tn=1024)
            lam_init = 0.8 - 0.6 * math.exp(-0.3 * l)
            y_self = diff_attention(proj.reshape(batch, seq, -1), kv, tab, bias,
                                    lambda_qk[j], g_subln[j], group=group, heads=ATTN_HEADS,
                                    lam_init=lam_init)
            q_col = SELF_WIDTH
        y_mem = mem_attention(proj, mkv, l, batch=batch, seq=seq, n_mem=n_mem,
                              q_col=q_col, tq=1024)
        h = residual_matmul2(h, y_self.reshape(tokens, -1), y_mem, w_o, l, tm=2048, tn=512)
        h = ffn_block(h, g_ffn[l], w_gate_up, w_down, l, tm=1024, tc=256,
                      final_gain=g_final if l == depth - 1 else None)
    return h.reshape(batch, seq, d)
```

```python
import functools
import math

import jax
import jax.numpy as jnp
import numpy as np
from jax import lax
from jax.experimental import pallas as pl
from jax.experimental.pallas import tpu as pltpu

D_MODEL = 2048
HEAD_DIM = 128
N_MEM_HEADS = 4
MEM_WIDTH = N_MEM_HEADS * HEAD_DIM
SELF_WIDTH = D_MODEL - MEM_WIDTH
N_SELF_HEADS = SELF_WIDTH // HEAD_DIM
DIFF_HALF = HEAD_DIM // 2
MOBA_BLOCK = 256
MOBA_TOPK = 3
N_BUCKETS = 32
MAX_DISTANCE = 128
RMS_EPS = 1e-6
NEG = -1e30
LOG2E = math.log2(math.e)

BLK = MOBA_BLOCK
V7X_VMEM_BYTES = 64 * 1024 * 1024
VMEM_LIMIT = V7X_VMEM_BYTES - 4 * 1024 * 1024

F32 = jnp.float32
BF16 = jnp.bfloat16
NT_DIMS = (((1,), (1,)), ((), ()))


def _params(n_grid_axes):
    return pltpu.CompilerParams(
        dimension_semantics=("arbitrary",) * n_grid_axes, vmem_limit_bytes=VMEM_LIMIT)


def _rmsnorm_rows(x, g):
    ms = jnp.mean(x * x, axis=-1, keepdims=True)
    return x * lax.rsqrt(ms + RMS_EPS) * g


ROW_PARTS = 4


def _staggered_row_specs(tm, d, n_tiles, n_steps):
    part = tm // ROW_PARTS
    last = n_tiles * ROW_PARTS - 1
    specs = []
    for q in range(ROW_PARTS):
        s_q = 1 + q * n_steps // ROW_PARTS

        def index(i, c, q=q, s_q=s_q):
            ahead = jnp.where(c >= s_q, 1, 0) if s_q < n_steps else 0
            return (jnp.minimum(ROW_PARTS * (i + ahead) + q, last), 0)

        specs.append(pl.BlockSpec((part, d), index))
    return specs


def _row_part(ref, q):
    part = ref.shape[0] // ROW_PARTS
    return slice(q * part, (q + 1) * part)


def _norm_matmul_kernel(*refs):
    x_parts = refs[:ROW_PARTS]
    g_ref, w_ref, cs_ref, o_ref, xn_ref = refs[ROW_PARTS:]

    def project():
        acc = jnp.dot(xn_ref[...], w_ref[...].astype(BF16), preferred_element_type=F32)
        o_ref[...] = (acc * cs_ref[...]).astype(o_ref.dtype)

    @pl.when(pl.program_id(1) == 0)
    def _():
        for q, x_ref in enumerate(x_parts):
            xn_ref[_row_part(xn_ref, q), :] = _rmsnorm_rows(x_ref[...], g_ref[...]).astype(BF16)
        project()

    @pl.when(pl.program_id(1) > 0)
    def _():
        project()


def norm_matmul(x, g, w, layer, col_scale, *, tm, tn):
    m, d = x.shape
    n = w.shape[-1]
    return pl.pallas_call(
        _norm_matmul_kernel,
        grid=(m // tm, n // tn),
        in_specs=_staggered_row_specs(tm, d, m // tm, n // tn) + [
            pl.BlockSpec((1, d), lambda i, j: (0, 0)),
            pl.BlockSpec((None, d, tn), lambda i, j: (layer, 0, j)),
            pl.BlockSpec((1, tn), lambda i, j: (0, j)),
        ],
        out_specs=pl.BlockSpec((tm, tn), lambda i, j: (i, j)),
        out_shape=jax.ShapeDtypeStruct((m, n), BF16),
        scratch_shapes=[pltpu.VMEM((tm, d), BF16)],
        compiler_params=_params(2),
        name="norm_matmul",
    )(*([x] * ROW_PARTS), g.reshape(1, d), w, col_scale.reshape(1, n))


def _ffn_kernel(*refs, n_slab, final_norm):
    h_parts = refs[:ROW_PARTS]
    g_ref, gf_ref, wg_ref, wu_ref, wd_ref, o_ref, xn_ref = refs[ROW_PARTS:]
    c = pl.program_id(1)

    def add_chunk():
        xn = xn_ref[...]
        gate = jnp.dot(xn, wg_ref[...].astype(BF16), preferred_element_type=F32)
        up = jnp.dot(xn, wu_ref[...].astype(BF16), preferred_element_type=F32)
        act = (gate * jax.nn.sigmoid(gate) * up).astype(BF16)
        slab = o_ref.shape[1] // n_slab
        for n in range(n_slab):
            cols = slice(n * slab, (n + 1) * slab)
            o_ref[:, cols] += jnp.dot(act, wd_ref[:, cols].astype(BF16), preferred_element_type=F32)

    @pl.when(c == 0)
    def _():
        for q, h_ref in enumerate(h_parts):
            h = h_ref[...]
            xn_ref[_row_part(xn_ref, q), :] = _rmsnorm_rows(h, g_ref[...]).astype(BF16)
            o_ref[_row_part(o_ref, q), :] = h
        add_chunk()

    @pl.when(c > 0)
    def _():
        add_chunk()

    if final_norm:
        @pl.when(c == pl.num_programs(1) - 1)
        def _():
            o_ref[...] = _rmsnorm_rows(o_ref[...], gf_ref[...])


def ffn_block(h, g, w_gate_up, w_down, layer, *, tm, tc, final_gain=None):
    m, d = h.shape
    d_ff = w_down.shape[1]
    nc = d_ff // tc
    final_norm = final_gain is not None
    gf = final_gain if final_norm else g
    return pl.pallas_call(
        functools.partial(_ffn_kernel, n_slab=4, final_norm=final_norm),
        grid=(m // tm, nc),
        in_specs=_staggered_row_specs(tm, d, m // tm, nc) + [
            pl.BlockSpec((1, d), lambda i, c: (0, 0)),
            pl.BlockSpec((1, d), lambda i, c: (0, 0)),
            pl.BlockSpec((None, d, tc), lambda i, c: (layer, 0, c)),
            pl.BlockSpec((None, d, tc), lambda i, c: (layer, 0, c + nc)),
            pl.BlockSpec((None, tc, d), lambda i, c: (layer, c, 0)),
        ],
        out_specs=pl.BlockSpec((tm, d), lambda i, c: (i, 0)),
        out_shape=jax.ShapeDtypeStruct((m, d), F32),
        scratch_shapes=[pltpu.VMEM((tm, d), BF16)],
        compiler_params=_params(2),
        name="ffn_block",
    )(*([h] * ROW_PARTS), g.reshape(1, d), gf.reshape(1, d), w_gate_up, w_gate_up, w_down)


def _residual_matmul2_kernel(h_ref, a1_ref, a2_ref, w1_ref, w2_ref, o_ref):
    acc = jnp.dot(a1_ref[...], w1_ref[...].astype(BF16), preferred_element_type=F32)
    acc = acc + jnp.dot(a2_ref[...], w2_ref[...].astype(BF16), preferred_element_type=F32)
    o_ref[...] = h_ref[...] + acc


def residual_matmul2(h, a1, a2, w, layer, *, tm, tn):
    m, n = h.shape
    k1, k2 = a1.shape[1], a2.shape[1]
    assert k1 % k2 == 0
    return pl.pallas_call(
        _residual_matmul2_kernel,
        grid=(m // tm, n // tn),
        in_specs=[
            pl.BlockSpec((tm, tn), lambda i, j: (i, j)),
            pl.BlockSpec((tm, k1), lambda i, j: (i, 0)),
            pl.BlockSpec((tm, k2), lambda i, j: (i, 0)),
            pl.BlockSpec((None, k1, tn), lambda i, j: (layer, 0, j)),
            pl.BlockSpec((None, k2, tn), lambda i, j: (layer, k1 // k2, j)),
        ],
        out_specs=pl.BlockSpec((tm, tn), lambda i, j: (i, j)),
        out_shape=jax.ShapeDtypeStruct((m, n), F32),
        compiler_params=_params(2),
        name="residual_matmul2",
    )(h, a1, a2, w, w)


def _rel_bucket_np(rel):
    n = np.maximum(rel, 0)
    max_exact = N_BUCKETS // 2
    nf = np.maximum(n, 1).astype(np.float32)
    large = max_exact + (np.log(nf / np.float32(max_exact)) / np.float32(math.log(MAX_DISTANCE / max_exact))
                         * np.float32(N_BUCKETS - max_exact)).astype(np.int32)
    large = np.minimum(large, N_BUCKETS - 1)
    return np.where(n < max_exact, n, large).astype(np.int32)


def _bucket_tiles():
    qi = np.arange(BLK, dtype=np.int32)[:, None]
    ki = np.arange(BLK, dtype=np.int32)[None, :]
    diag = _rel_bucket_np(qi - ki)
    prev = _rel_bucket_np(qi - ki + BLK)
    assert _rel_bucket_np(np.array([BLK + 1]))[0] == N_BUCKETS - 1
    return jnp.asarray(diag), jnp.asarray(prev)


def _bias_tile(bkt, tab_ref, head):
    out = jnp.zeros(bkt.shape, F32)
    for b in range(N_BUCKETS):
        out = jnp.where(bkt == b, tab_ref[head, b] * LOG2E, out)
    return out


def _causal_tile(tile):
    qi = lax.broadcasted_iota(jnp.int32, tile.shape, 0)
    ki = lax.broadcasted_iota(jnp.int32, tile.shape, 1)
    return jnp.where(ki <= qi, tile, NEG)


def _bias_kernel(tab_ref, bd_ref, bp_ref, o_ref):
    h = pl.program_id(0)
    o_ref[...] = jnp.concatenate(
        [_bias_tile(bp_ref[...], tab_ref, h),
         _causal_tile(_bias_tile(bd_ref[...], tab_ref, h))], axis=1)


def bias_tiles(tab, bkt_diag, bkt_prev):
    n_heads = tab.shape[0]
    return pl.pallas_call(
        _bias_kernel,
        grid=(n_heads,),
        in_specs=[
            pl.BlockSpec(memory_space=pltpu.SMEM),
            pl.BlockSpec((BLK, BLK), lambda h: (0, 0)),
            pl.BlockSpec((BLK, BLK), lambda h: (0, 0)),
        ],
        out_specs=pl.BlockSpec((None, BLK, 2 * BLK), lambda h: (h, 0, 0)),
        out_shape=jax.ShapeDtypeStruct((n_heads, BLK, 2 * BLK), F32),
        compiler_params=_params(1),
        name="bias_tiles",
    )(tab, bkt_diag, bkt_prev)


def _wide(x, width):
    return jnp.tile(x, (1, width // HEAD_DIM))


def _softmax_step(s, v, m_ref, acc_ref, *, first, shift=None):
    v_aug = jnp.concatenate([v, jnp.ones(v.shape, v.dtype)], axis=1)
    m_cur = jnp.max(s, axis=-1, keepdims=True)
    if shift is not None:
        m_cur = m_cur + shift
    if first:
        m_new = jnp.broadcast_to(m_cur, m_ref.shape)
    else:
        m_old = m_ref[...]
        m_new = jnp.maximum(m_old, m_cur)
    p = jnp.exp2(s - _wide(m_new if shift is None else m_new - shift, s.shape[1]))
    pv = jnp.dot(p.astype(BF16), v_aug, preferred_element_type=F32)
    if first:
        acc_ref[...] = pv
    else:
        acc_ref[...] = _wide(jnp.exp2(m_old - m_new), acc_ref.shape[1]) * acc_ref[...] + pv
    m_ref[...] = m_new


def _softmax_result(acc_ref):
    acc = acc_ref[...]
    return acc[:, :HEAD_DIM] / acc[:, HEAD_DIM:]


def _head_cols(hh):
    return slice(hh * HEAD_DIM, (hh + 1) * HEAD_DIM)


def _moba_kernel(tab_ref, bias_ref, q_ref, k_ref, v_ref, o_ref,
                 kmean_ref, qa_ref, m_ref, acc_ref, *, gate_fill, heads):
    hg, qb = pl.program_id(0), pl.program_id(2)
    group, seq, _ = k_ref.shape
    nb = seq // BLK
    chains = [(hh, g) for hh in range(heads) for g in range(group)]

    @pl.when(qb == 0)
    def _():
        for c, (hh, g) in enumerate(chains):
            kf = k_ref[g, :, _head_cols(hh)].astype(F32).reshape(nb, BLK, HEAD_DIM)
            km = jnp.mean(kf, axis=1)
            km_hi = km.astype(BF16)
            km_lo = (km - km_hi.astype(F32)).astype(BF16)
            kmean_ref[c] = jnp.concatenate([km_hi, km_lo], axis=0)

    row = lax.broadcasted_iota(jnp.int32, (nb, BLK), 0)
    past = row < qb
    n_sel = max(1, min(MOBA_TOPK, nb - 1))
    for c, (hh, g) in enumerate(chains):
        q = q_ref[g, :, _head_cols(hh)]
        parts = lax.dot_general(kmean_ref[c], q, NT_DIMS, preferred_element_type=F32)
        gate = jnp.where(past, parts[:nb] + parts[nb:], gate_fill)
        beaten_by = jnp.zeros(gate.shape, F32)
        for i in range(nb):
            gi = gate[i:i + 1, :]
            wins = (gi > gate) | ((gi == gate) & (i < row))
            beaten_by = beaten_by + jnp.where(wins, 1.0, 0.0)
        keep = ((beaten_by < n_sel) & past) | (row == qb)
        sel_neg = jnp.where(keep, 0.0, NEG)
        sel_neg = jnp.concatenate([sel_neg, jnp.zeros((HEAD_DIM - nb, BLK), F32)], axis=0)
        qa_ref[c] = jnp.concatenate([q, sel_neg.T.astype(BF16)], axis=1)


    def span_scores(c, j, n):
        hh, g = chains[c]
        rows = pl.ds(pl.multiple_of(j * BLK, BLK), n * BLK)
        kj = k_ref[g, rows, _head_cols(hh)]
        lane = lax.broadcasted_iota(jnp.int32, (n * BLK, HEAD_DIM), 1)
        blk = j + lax.broadcasted_iota(jnp.int32, (n * BLK, HEAD_DIM), 0) // BLK
        onehot = jnp.where(lane == blk, 1.0, 0.0).astype(BF16)
        k_aug = jnp.concatenate([kj, onehot], axis=1)
        s = lax.dot_general(qa_ref[c], k_aug, NT_DIMS, preferred_element_type=F32)
        return s, v_ref[g, rows, _head_cols(hh)]

    @pl.when(qb == 0)
    def _():
        for c, (hh, g) in enumerate(chains):
            s, v = span_scores(c, qb, 1)
            _softmax_step(s + bias_ref[hh, :, BLK:], v, m_ref.at[c], acc_ref.at[c], first=True)

    @pl.when(qb >= 1)
    def _():
        for c, (hh, g) in enumerate(chains):
            s, v = span_scores(c, qb - 1, 2)
            _softmax_step(s + bias_ref[hh], v, m_ref.at[c], acc_ref.at[c], first=True)

    far_bias = [tab_ref[hg * heads + hh, N_BUCKETS - 1] * LOG2E for hh in range(heads)]

    def far_pair(p, carry):
        for c, (hh, g) in enumerate(chains):
            s, v = span_scores(c, 2 * p, 2)
            _softmax_step(s, v, m_ref.at[c], acc_ref.at[c], first=False, shift=far_bias[hh])
        return carry

    lax.fori_loop(0, jnp.maximum(qb - 1, 0) // 2, far_pair, 0)

    @pl.when((qb >= 2) & (qb % 2 == 0))
    def _():
        for c, (hh, g) in enumerate(chains):
            s, v = span_scores(c, qb - 2, 1)
            _softmax_step(s, v, m_ref.at[c], acc_ref.at[c], first=False, shift=far_bias[hh])

    for c, (hh, g) in enumerate(chains):
        o_ref[g, :, _head_cols(hh)] = _softmax_result(acc_ref.at[c]).astype(o_ref.dtype)


def moba_attention(proj, tab, bias, *, group, heads, gate_fill):
    batch, seq, _ = proj.shape
    nqb = seq // BLK
    k_off = N_SELF_HEADS // heads
    width = heads * HEAD_DIM
    chains = heads * group
    kern = functools.partial(_moba_kernel, gate_fill=gate_fill, heads=heads)
    return pl.pallas_call(
        kern,
        grid=(N_SELF_HEADS // heads, batch // group, nqb),
        in_specs=[
            pl.BlockSpec(memory_space=pltpu.SMEM),
            pl.BlockSpec((heads, BLK, 2 * BLK), lambda h, b, qb: (h, 0, 0)),
            pl.BlockSpec((group, BLK, width), lambda h, b, qb: (b, qb, h)),
            pl.BlockSpec((group, seq, width), lambda h, b, qb: (b, 0, k_off + h)),
            pl.BlockSpec((group, seq, width), lambda h, b, qb: (b, 0, 2 * k_off + h)),
        ],
        out_specs=pl.BlockSpec((group, BLK, width), lambda h, b, qb: (b, qb, h)),
        out_shape=jax.ShapeDtypeStruct((batch, seq, SELF_WIDTH), BF16),
        scratch_shapes=[
            pltpu.VMEM((chains, 2 * (seq // BLK), HEAD_DIM), BF16),
            pltpu.VMEM((chains, BLK, 2 * HEAD_DIM), BF16),
            pltpu.VMEM((chains, BLK, HEAD_DIM), F32), pltpu.VMEM((chains, BLK, 2 * HEAD_DIM), F32),
        ],
        compiler_params=_params(3),
        name="moba_attention",
    )(tab, bias, proj, proj, proj)


def _twice(tile):
    return jnp.concatenate([tile, tile], axis=0)


def _diff_kernel(tab_ref, bias_ref, lq_ref, gs_ref, q_ref, k_ref, v_ref, o_ref,
                 q2_ref, m_ref, acc_ref, *, lam_init, heads):
    hg, qb = pl.program_id(0), pl.program_id(2)
    group = k_ref.shape[0]
    chains = [(hh, g) for hh in range(heads) for g in range(group)]

    lane = lax.broadcasted_iota(jnp.int32, (BLK, HEAD_DIM), 1)
    for c, (hh, g) in enumerate(chains):
        q = q_ref[g, :, _head_cols(hh)]
        zero = jnp.zeros_like(q)
        q2_ref[c] = jnp.concatenate([jnp.where(lane < DIFF_HALF, q, zero),
                                     jnp.where(lane >= DIFF_HALF, q, zero)], axis=0)

    def span_scores(c, j, n):
        hh, g = chains[c]
        rows = pl.ds(pl.multiple_of(j * BLK, BLK), n * BLK)
        s = lax.dot_general(q2_ref[c], k_ref[g, rows, _head_cols(hh)], NT_DIMS,
                            preferred_element_type=F32)
        return s, v_ref[g, rows, _head_cols(hh)]

    @pl.when(qb == 0)
    def _():
        for c, (hh, g) in enumerate(chains):
            s, v = span_scores(c, qb, 1)
            _softmax_step(s + _twice(bias_ref[hh, :, BLK:]), v, m_ref.at[c], acc_ref.at[c],
                          first=True)

    @pl.when(qb >= 1)
    def _():
        for c, (hh, g) in enumerate(chains):
            s, v = span_scores(c, qb - 1, 2)
            _softmax_step(s + _twice(bias_ref[hh]), v, m_ref.at[c], acc_ref.at[c], first=True)

    far_bias = [tab_ref[hg * heads + hh, N_BUCKETS - 1] * LOG2E for hh in range(heads)]

    def far_pair(p, carry):
        for c, (hh, g) in enumerate(chains):
            s, v = span_scores(c, 2 * p, 2)
            _softmax_step(s, v, m_ref.at[c], acc_ref.at[c], first=False, shift=far_bias[hh])
        return carry

    lax.fori_loop(0, jnp.maximum(qb - 1, 0) // 2, far_pair, 0)

    @pl.when((qb >= 2) & (qb % 2 == 0))
    def _():
        for c, (hh, g) in enumerate(chains):
            s, v = span_scores(c, qb - 2, 1)
            _softmax_step(s, v, m_ref.at[c], acc_ref.at[c], first=False, shift=far_bias[hh])

    lq = lq_ref[...]
    lam = (jnp.exp(jnp.sum(lq[0:1] * lq[1:2], axis=-1, keepdims=True))
           - jnp.exp(jnp.sum(lq[2:3] * lq[3:4], axis=-1, keepdims=True)) + lam_init)
    for c, (hh, g) in enumerate(chains):
        o_all = _softmax_result(acc_ref.at[c])
        o = o_all[:BLK] - lam * o_all[BLK:]
        o_ref[g, :, _head_cols(hh)] = (
            _rmsnorm_rows(o, gs_ref[...]) * (1.0 - lam_init)).astype(o_ref.dtype)


def diff_attention(proj, kv, tab, bias, lq, g_subln, *, group, heads, lam_init):
    batch, seq, _ = proj.shape
    nqb = seq // BLK
    v_off = N_SELF_HEADS // heads
    width = heads * HEAD_DIM
    chains = heads * group
    kern = functools.partial(_diff_kernel, lam_init=lam_init, heads=heads)
    return pl.pallas_call(
        kern,
        grid=(N_SELF_HEADS // heads, batch // group, nqb),
        in_specs=[
            pl.BlockSpec(memory_space=pltpu.SMEM),
            pl.BlockSpec((heads, BLK, 2 * BLK), lambda h, b, qb: (h, 0, 0)),
            pl.BlockSpec((4, DIFF_HALF), lambda h, b, qb: (0, 0)),
            pl.BlockSpec((1, HEAD_DIM), lambda h, b, qb: (0, 0)),
            pl.BlockSpec((group, BLK, width), lambda h, b, qb: (b, qb, h)),
            pl.BlockSpec((group, seq, width), lambda h, b, qb: (b, 0, h)),
            pl.BlockSpec((group, seq, width), lambda h, b, qb: (b, 0, v_off + h)),
        ],
        out_specs=pl.BlockSpec((group, BLK, width), lambda h, b, qb: (b, qb, h)),
        out_shape=jax.ShapeDtypeStruct((batch, seq, SELF_WIDTH), BF16),
        scratch_shapes=[
            pltpu.VMEM((chains, 2 * BLK, HEAD_DIM), BF16),
            pltpu.VMEM((chains, 2 * BLK, HEAD_DIM), F32),
            pltpu.VMEM((chains, 2 * BLK, 2 * HEAD_DIM), F32),
        ],
        compiler_params=_params(3),
        name="diff_attention",
    )(tab, bias, lq, g_subln.reshape(1, HEAD_DIM), proj, kv, kv)


def _mem_kernel(q_ref, mk_ref, mv_ref, o_ref):
    for hh in range(N_MEM_HEADS):
        cols = _head_cols(hh)
        s = lax.dot_general(q_ref[:, cols], mk_ref[:, cols], NT_DIMS, preferred_element_type=F32)
        p = jnp.exp2(s - jnp.max(s, axis=-1, keepdims=True))
        mv = mv_ref[:, cols]
        pv = jnp.dot(p.astype(BF16), jnp.concatenate([mv, jnp.ones(mv.shape, mv.dtype)], axis=1),
                     preferred_element_type=F32)
        o_ref[:, cols] = (pv[:, :HEAD_DIM] / pv[:, HEAD_DIM:]).astype(o_ref.dtype)


def _mem_kv_kernel(x_ref, g_ref, w_ref, o_ref, xn_ref):
    @pl.when(pl.program_id(1) == 0)
    def _():
        xn_ref[...] = _rmsnorm_rows(x_ref[...], g_ref[...]).astype(BF16)

    o_ref[...] = jnp.dot(xn_ref[...], w_ref[...].astype(BF16),
                         preferred_element_type=F32).astype(o_ref.dtype)


def mem_kv(mem, g_mem, w_mem_kv, *, tn):
    m, d = mem.shape
    depth, _, n = w_mem_kv.shape
    return pl.pallas_call(
        _mem_kv_kernel,
        grid=(depth, n // tn),
        in_specs=[
            pl.BlockSpec((m, d), lambda l, j: (0, 0)),
            pl.BlockSpec((None, 1, d), lambda l, j: (l, 0, 0)),
            pl.BlockSpec((None, d, tn), lambda l, j: (l, 0, j)),
        ],
        out_specs=pl.BlockSpec((None, m, tn), lambda l, j: (l, 0, j)),
        out_shape=jax.ShapeDtypeStruct((depth, m, n), BF16),
        scratch_shapes=[pltpu.VMEM((m, d), BF16)],
        compiler_params=_params(2),
        name="mem_kv",
    )(mem, g_mem.reshape(depth, 1, d), w_mem_kv)


def mem_attention(proj, mkv, layer, *, batch, seq, n_mem, q_col, tq):
    nsb = seq // tq
    q_blk = q_col // MEM_WIDTH
    return pl.pallas_call(
        _mem_kernel,
        grid=(batch, nsb),
        in_specs=[
            pl.BlockSpec((tq, MEM_WIDTH), lambda b, sb: (b * nsb + sb, q_blk)),
            pl.BlockSpec((None, n_mem, MEM_WIDTH), lambda b, sb: (layer, b, 0)),
            pl.BlockSpec((None, n_mem, MEM_WIDTH), lambda b, sb: (layer, b, 1)),
        ],
        out_specs=pl.BlockSpec((tq, MEM_WIDTH), lambda b, sb: (b * nsb + sb, 0)),
        out_shape=jax.ShapeDtypeStruct((batch * seq, MEM_WIDTH), BF16),
        compiler_params=_params(2),
        name="mem_attention",
    )(proj, mkv, mkv)


ATTN_GROUP = 4
ATTN_HEADS = 4


def kernel(x, mem, rel_bias, g_mix, w_in_a, w_in_b, g_mem, w_mem_kv, w_o, g_ffn, w_gate_up, w_down,
           g_kv, w_kv_shared, lambda_qk, g_subln, g_final):
    batch, seq, d = x.shape
    n_mem = mem.shape[1]
    depth = g_mix.shape[0]
    n_a = w_in_a.shape[0]
    tokens = batch * seq
    group = math.gcd(batch, ATTN_GROUP)

    w_kv_shared = w_kv_shared[None]

    tab = rel_bias.T.astype(F32)
    bias = bias_tiles(tab, *_bucket_tiles())

    moba_scale = HEAD_DIM ** -0.5 * LOG2E
    diff_scale = DIFF_HALF ** -0.5 * LOG2E
    mem_scale = HEAD_DIM ** -0.5 * LOG2E
    ones = functools.partial(jnp.ones, dtype=F32)
    cs_a = jnp.concatenate([jnp.full((SELF_WIDTH,), moba_scale, F32), ones((2 * SELF_WIDTH,)),
                            jnp.full((MEM_WIDTH,), mem_scale, F32)])
    cs_b = jnp.concatenate([jnp.full((SELF_WIDTH,), diff_scale, F32),
                            jnp.full((MEM_WIDTH,), mem_scale, F32)])

    h = x.reshape(tokens, d)
    mem2 = mem.reshape(batch * n_mem, d)
    mkv = mem_kv(mem2, g_mem, w_mem_kv, tn=512)
    kv = None
    for l in range(depth):
        if l < n_a:
            proj = norm_matmul(h, g_mix[l], w_in_a, l, cs_a, tm=1024, tn=1024)
            y_self = moba_attention(proj.reshape(batch, seq, -1), tab, bias,
                                    group=group, heads=ATTN_HEADS, gate_fill=NEG * moba_scale)
            q_col = 3 * SELF_WIDTH
        else:
            j = l - n_a
            if kv is None:
                kv = norm_matmul(h, g_kv, w_kv_shared, 0, ones((2 * SELF_WIDTH,)), tm=1024, tn=1024)
                kv = kv.reshape(batch, seq, -1)
            proj = norm_matmul(h, g_mix[l], w_in_b, j, cs_b, tm=1024, tn=1024)
            lam_init = 0.8 - 0.6 * math.exp(-0.3 * l)
            y_self = diff_attention(proj.reshape(batch, seq, -1), kv, tab, bias,
                                    lambda_qk[j], g_subln[j], group=group, heads=ATTN_HEADS,
                                    lam_init=lam_init)
            q_col = SELF_WIDTH
        y_mem = mem_attention(proj, mkv, l, batch=batch, seq=seq, n_mem=n_mem,
                              q_col=q_col, tq=1024)
        h = residual_matmul2(h, y_self.reshape(tokens, -1), y_mem, w_o, l, tm=2048, tn=512)
        h = ffn_block(h, g_ffn[l], w_gate_up, w_down, l, tm=1024, tc=256,
                      final_gain=g_final if l == depth - 1 else None)
    return h.reshape(batch, seq, d)
```

```python
import functools
import math

import jax
import jax.numpy as jnp
import numpy as np
from jax import lax
from jax.experimental import pallas as pl
from jax.experimental.pallas import tpu as pltpu

D_MODEL = 2048
HEAD_DIM = 128
N_MEM_HEADS = 4
MEM_WIDTH = N_MEM_HEADS * HEAD_DIM
SELF_WIDTH = D_MODEL - MEM_WIDTH
N_SELF_HEADS = SELF_WIDTH // HEAD_DIM
DIFF_HALF = HEAD_DIM // 2
MOBA_BLOCK = 256
MOBA_TOPK = 3
N_BUCKETS = 32
MAX_DISTANCE = 128
RMS_EPS = 1e-6
NEG = -1e30
LOG2E = math.log2(math.e)

BLK = MOBA_BLOCK
V7X_VMEM_BYTES = 64 * 1024 * 1024
VMEM_LIMIT = V7X_VMEM_BYTES - 4 * 1024 * 1024

F32 = jnp.float32
BF16 = jnp.bfloat16
NT_DIMS = (((1,), (1,)), ((), ()))


def _params(n_grid_axes):
    return pltpu.CompilerParams(
        dimension_semantics=("arbitrary",) * n_grid_axes, vmem_limit_bytes=VMEM_LIMIT)


def _rmsnorm_rows(x, g):
    ms = jnp.mean(x * x, axis=-1, keepdims=True)
    return x * lax.rsqrt(ms + RMS_EPS) * g


ROW_PARTS = 4


def _staggered_row_specs(tm, d, n_tiles, n_steps):
    part = tm // ROW_PARTS
    last = n_tiles * ROW_PARTS - 1
    specs = []
    for q in range(ROW_PARTS):
        s_q = 1 + q * n_steps // ROW_PARTS

        def index(i, c, q=q, s_q=s_q):
            ahead = jnp.where(c >= s_q, 1, 0) if s_q < n_steps else 0
            return (jnp.minimum(ROW_PARTS * (i + ahead) + q, last), 0)

        specs.append(pl.BlockSpec((part, d), index))
    return specs


def _row_part(ref, q):
    part = ref.shape[0] // ROW_PARTS
    return slice(q * part, (q + 1) * part)


def _norm_matmul_kernel(*refs):
    x_parts = refs[:ROW_PARTS]
    g_ref, w_ref, cs_ref, o_ref, xn_ref = refs[ROW_PARTS:]

    def project():
        acc = jnp.dot(xn_ref[...], w_ref[...].astype(BF16), preferred_element_type=F32)
        o_ref[...] = (acc * cs_ref[...]).astype(o_ref.dtype)

    @pl.when(pl.program_id(1) == 0)
    def _():
        for q, x_ref in enumerate(x_parts):
            xn_ref[_row_part(xn_ref, q), :] = _rmsnorm_rows(x_ref[...], g_ref[...]).astype(BF16)
        project()

    @pl.when(pl.program_id(1) > 0)
    def _():
        project()


def norm_matmul(x, g, w, layer, col_scale, *, tm, tn):
    m, d = x.shape
    n = w.shape[-1]
    return pl.pallas_call(
        _norm_matmul_kernel,
        grid=(m // tm, n // tn),
        in_specs=_staggered_row_specs(tm, d, m // tm, n // tn) + [
            pl.BlockSpec((1, d), lambda i, j: (0, 0)),
            pl.BlockSpec((None, d, tn), lambda i, j: (layer, 0, j)),
            pl.BlockSpec((1, tn), lambda i, j: (0, j)),
        ],
        out_specs=pl.BlockSpec((tm, tn), lambda i, j: (i, j)),
        out_shape=jax.ShapeDtypeStruct((m, n), BF16),
        scratch_shapes=[pltpu.VMEM((tm, d), BF16)],
        compiler_params=_params(2),
        name="norm_matmul",
    )(*([x] * ROW_PARTS), g.reshape(1, d), w, col_scale.reshape(1, n))


def _ffn_kernel(*refs, n_slab, final_norm):
    h_parts = refs[:ROW_PARTS]
    g_ref, gf_ref, wg_ref, wu_ref, wd_ref, o_ref, xn_ref = refs[ROW_PARTS:]
    c = pl.program_id(1)

    def add_chunk():
        xn = xn_ref[...]
        gate = jnp.dot(xn, wg_ref[...].astype(BF16), preferred_element_type=F32)
        up = jnp.dot(xn, wu_ref[...].astype(BF16), preferred_element_type=F32)
        act = (gate * jax.nn.sigmoid(gate) * up).astype(BF16)
        slab = o_ref.shape[1] // n_slab
        for n in range(n_slab):
            cols = slice(n * slab, (n + 1) * slab)
            o_ref[:, cols] += jnp.dot(act, wd_ref[:, cols].astype(BF16), preferred_element_type=F32)

    @pl.when(c == 0)
    def _():
        for q, h_ref in enumerate(h_parts):
            h = h_ref[...]
            xn_ref[_row_part(xn_ref, q), :] = _rmsnorm_rows(h, g_ref[...]).astype(BF16)
            o_ref[_row_part(o_ref, q), :] = h
        add_chunk()

    @pl.when(c > 0)
    def _():
        add_chunk()

    if final_norm:
        @pl.when(c == pl.num_programs(1) - 1)
        def _():
            o_ref[...] = _rmsnorm_rows(o_ref[...], gf_ref[...])


def ffn_block(h, g, w_gate_up, w_down, layer, *, tm, tc, final_gain=None):
    m, d = h.shape
    d_ff = w_down.shape[1]
    nc = d_ff // tc
    final_norm = final_gain is not None
    gf = final_gain if final_norm else g
    return pl.pallas_call(
        functools.partial(_ffn_kernel, n_slab=4, final_norm=final_norm),
        grid=(m // tm, nc),
        in_specs=_staggered_row_specs(tm, d, m // tm, nc) + [
            pl.BlockSpec((1, d), lambda i, c: (0, 0)),
            pl.BlockSpec((1, d), lambda i, c: (0, 0)),
            pl.BlockSpec((None, d, tc), lambda i, c: (layer, 0, c)),
            pl.BlockSpec((None, d, tc), lambda i, c: (layer, 0, c + nc)),
            pl.BlockSpec((None, tc, d), lambda i, c: (layer, c, 0)),
        ],
        out_specs=pl.BlockSpec((tm, d), lambda i, c: (i, 0)),
        out_shape=jax.ShapeDtypeStruct((m, d), F32),
        scratch_shapes=[pltpu.VMEM((tm, d), BF16)],
        compiler_params=_params(2),
        name="ffn_block",
    )(*([h] * ROW_PARTS), g.reshape(1, d), gf.reshape(1, d), w_gate_up, w_gate_up, w_down)


def _residual_matmul2_kernel(h_ref, a1_ref, a2_ref, w1_ref, w2_ref, o_ref):
    acc = jnp.dot(a1_ref[...], w1_ref[...].astype(BF16), preferred_element_type=F32)
    acc = acc + jnp.dot(a2_ref[...], w2_ref[...].astype(BF16), preferred_element_type=F32)
    o_ref[...] = h_ref[...] + acc


def residual_matmul2(h, a1, a2, w, layer, *, tm, tn):
    m, n = h.shape
    k1, k2 = a1.shape[1], a2.shape[1]
    assert k1 % k2 == 0
    return pl.pallas_call(
        _residual_matmul2_kernel,
        grid=(n // tn, m // tm),
        in_specs=[
            pl.BlockSpec((tm, tn), lambda j, i: (i, j)),
            pl.BlockSpec((tm, k1), lambda j, i: (i, 0)),
            pl.BlockSpec((tm, k2), lambda j, i: (i, 0)),
            pl.BlockSpec((None, k1, tn), lambda j, i: (layer, 0, j)),
            pl.BlockSpec((None, k2, tn), lambda j, i: (layer, k1 // k2, j)),
        ],
        out_specs=pl.BlockSpec((tm, tn), lambda j, i: (i, j)),
        out_shape=jax.ShapeDtypeStruct((m, n), F32),
        compiler_params=_params(2),
        name="residual_matmul2",
    )(h, a1, a2, w, w)


def _rel_bucket_np(rel):
    n = np.maximum(rel, 0)
    max_exact = N_BUCKETS // 2
    nf = np.maximum(n, 1).astype(np.float32)
    large = max_exact + (np.log(nf / np.float32(max_exact)) / np.float32(math.log(MAX_DISTANCE / max_exact))
                         * np.float32(N_BUCKETS - max_exact)).astype(np.int32)
    large = np.minimum(large, N_BUCKETS - 1)
    return np.where(n < max_exact, n, large).astype(np.int32)


def _bucket_tiles():
    qi = np.arange(BLK, dtype=np.int32)[:, None]
    ki = np.arange(BLK, dtype=np.int32)[None, :]
    diag = _rel_bucket_np(qi - ki)
    prev = _rel_bucket_np(qi - ki + BLK)
    assert _rel_bucket_np(np.array([BLK + 1]))[0] == N_BUCKETS - 1
    return jnp.asarray(diag), jnp.asarray(prev)


def _bias_tile(bkt, tab_ref, head):
    out = jnp.zeros(bkt.shape, F32)
    for b in range(N_BUCKETS):
        out = jnp.where(bkt == b, tab_ref[head, b] * LOG2E, out)
    return out


def _causal_tile(tile):
    qi = lax.broadcasted_iota(jnp.int32, tile.shape, 0)
    ki = lax.broadcasted_iota(jnp.int32, tile.shape, 1)
    return jnp.where(ki <= qi, tile, NEG)


def _bias_kernel(tab_ref, bd_ref, bp_ref, o_ref):
    h = pl.program_id(0)
    o_ref[...] = jnp.concatenate(
        [_bias_tile(bp_ref[...], tab_ref, h),
         _causal_tile(_bias_tile(bd_ref[...], tab_ref, h))], axis=1)


def bias_tiles(tab, bkt_diag, bkt_prev):
    n_heads = tab.shape[0]
    return pl.pallas_call(
        _bias_kernel,
        grid=(n_heads,),
        in_specs=[
            pl.BlockSpec(memory_space=pltpu.SMEM),
            pl.BlockSpec((BLK, BLK), lambda h: (0, 0)),
            pl.BlockSpec((BLK, BLK), lambda h: (0, 0)),
        ],
        out_specs=pl.BlockSpec((None, BLK, 2 * BLK), lambda h: (h, 0, 0)),
        out_shape=jax.ShapeDtypeStruct((n_heads, BLK, 2 * BLK), F32),
        compiler_params=_params(1),
        name="bias_tiles",
    )(tab, bkt_diag, bkt_prev)


def _wide(x, width):
    return jnp.tile(x, (1, width // HEAD_DIM))


def _softmax_step(s, v, m_ref, acc_ref, *, first, shift=None):
    v_aug = jnp.concatenate([v, jnp.ones(v.shape, v.dtype)], axis=1)
    m_cur = jnp.max(s, axis=-1, keepdims=True)
    if shift is not None:
        m_cur = m_cur + shift
    if first:
        m_new = jnp.broadcast_to(m_cur, m_ref.shape)
    else:
        m_old = m_ref[...]
        m_new = jnp.maximum(m_old, m_cur)
    p = jnp.exp2(s - _wide(m_new if shift is None else m_new - shift, s.shape[1]))
    pv = jnp.dot(p.astype(BF16), v_aug, preferred_element_type=F32)
    if first:
        acc_ref[...] = pv
    else:
        acc_ref[...] = _wide(jnp.exp2(m_old - m_new), acc_ref.shape[1]) * acc_ref[...] + pv
    m_ref[...] = m_new


def _softmax_result(acc_ref):
    acc = acc_ref[...]
    return acc[:, :HEAD_DIM] / acc[:, HEAD_DIM:]


def _head_cols(hh):
    return slice(hh * HEAD_DIM, (hh + 1) * HEAD_DIM)


def _moba_kernel(tab_ref, bias_ref, q_ref, k_ref, v_ref, o_ref,
                 kmean_ref, qa_ref, m_ref, acc_ref, *, gate_fill, heads):
    hg, qb = pl.program_id(0), pl.program_id(2)
    group, seq, _ = k_ref.shape
    nb = seq // BLK
    chains = [(hh, g) for hh in range(heads) for g in range(group)]

    @pl.when(qb == 0)
    def _():
        for c, (hh, g) in enumerate(chains):
            kf = k_ref[g, :, _head_cols(hh)].astype(F32).reshape(nb, BLK, HEAD_DIM)
            km = jnp.mean(kf, axis=1)
            km_hi = km.astype(BF16)
            km_lo = (km - km_hi.astype(F32)).astype(BF16)
            kmean_ref[c] = jnp.concatenate([km_hi, km_lo], axis=0)

    row = lax.broadcasted_iota(jnp.int32, (nb, BLK), 0)
    past = row < qb
    n_sel = max(1, min(MOBA_TOPK, nb - 1))
    for c, (hh, g) in enumerate(chains):
        q = q_ref[g, :, _head_cols(hh)]
        parts = lax.dot_general(kmean_ref[c], q, NT_DIMS, preferred_element_type=F32)
        gate = jnp.where(past, parts[:nb] + parts[nb:], gate_fill)
        beaten_by = jnp.zeros(gate.shape, F32)
        for i in range(nb):
            gi = gate[i:i + 1, :]
            wins = (gi > gate) | ((gi == gate) & (i < row))
            beaten_by = beaten_by + jnp.where(wins, 1.0, 0.0)
        keep = ((beaten_by < n_sel) & past) | (row == qb)
        sel_neg = jnp.where(keep, 0.0, NEG)
        sel_neg = jnp.concatenate([sel_neg, jnp.zeros((HEAD_DIM - nb, BLK), F32)], axis=0)
        qa_ref[c] = jnp.concatenate([q, sel_neg.T.astype(BF16)], axis=1)


    def span_scores(c, j, n):
        hh, g = chains[c]
        rows = pl.ds(pl.multiple_of(j * BLK, BLK), n * BLK)
        kj = k_ref[g, rows, _head_cols(hh)]
        lane = lax.broadcasted_iota(jnp.int32, (n * BLK, HEAD_DIM), 1)
        blk = j + lax.broadcasted_iota(jnp.int32, (n * BLK, HEAD_DIM), 0) // BLK
        onehot = jnp.where(lane == blk, 1.0, 0.0).astype(BF16)
        k_aug = jnp.concatenate([kj, onehot], axis=1)
        s = lax.dot_general(qa_ref[c], k_aug, NT_DIMS, preferred_element_type=F32)
        return s, v_ref[g, rows, _head_cols(hh)]

    @pl.when(qb == 0)
    def _():
        for c, (hh, g) in enumerate(chains):
            s, v = span_scores(c, qb, 1)
            _softmax_step(s + bias_ref[hh, :, BLK:], v, m_ref.at[c], acc_ref.at[c], first=True)

    @pl.when(qb >= 1)
    def _():
        for c, (hh, g) in enumerate(chains):
            s, v = span_scores(c, qb - 1, 2)
            _softmax_step(s + bias_ref[hh], v, m_ref.at[c], acc_ref.at[c], first=True)

    far_bias = [tab_ref[hg * heads + hh, N_BUCKETS - 1] * LOG2E for hh in range(heads)]

    def far_pair(p, carry):
        for c, (hh, g) in enumerate(chains):
            s, v = span_scores(c, 2 * p, 2)
            _softmax_step(s, v, m_ref.at[c], acc_ref.at[c], first=False, shift=far_bias[hh])
        return carry

    lax.fori_loop(0, jnp.maximum(qb - 1, 0) // 2, far_pair, 0)

    @pl.when((qb >= 2) & (qb % 2 == 0))
    def _():
        for c, (hh, g) in enumerate(chains):
            s, v = span_scores(c, qb - 2, 1)
            _softmax_step(s, v, m_ref.at[c], acc_ref.at[c], first=False, shift=far_bias[hh])

    for c, (hh, g) in enumerate(chains):
        o_ref[g, :, _head_cols(hh)] = _softmax_result(acc_ref.at[c]).astype(o_ref.dtype)


def moba_attention(proj, tab, bias, *, group, heads, gate_fill):
    batch, seq, _ = proj.shape
    nqb = seq // BLK
    k_off = N_SELF_HEADS // heads
    width = heads * HEAD_DIM
    chains = heads * group
    kern = functools.partial(_moba_kernel, gate_fill=gate_fill, heads=heads)
    return pl.pallas_call(
        kern,
        grid=(N_SELF_HEADS // heads, batch // group, nqb),
        in_specs=[
            pl.BlockSpec(memory_space=pltpu.SMEM),
            pl.BlockSpec((heads, BLK, 2 * BLK), lambda h, b, qb: (h, 0, 0)),
            pl.BlockSpec((group, BLK, width), lambda h, b, qb: (b, qb, h)),
            pl.BlockSpec((group, seq, width), lambda h, b, qb: (b, 0, k_off + h)),
            pl.BlockSpec((group, seq, width), lambda h, b, qb: (b, 0, 2 * k_off + h)),
        ],
        out_specs=pl.BlockSpec((group, BLK, width), lambda h, b, qb: (b, qb, h)),
        out_shape=jax.ShapeDtypeStruct((batch, seq, SELF_WIDTH), BF16),
        scratch_shapes=[
            pltpu.VMEM((chains, 2 * (seq // BLK), HEAD_DIM), BF16),
            pltpu.VMEM((chains, BLK, 2 * HEAD_DIM), BF16),
            pltpu.VMEM((chains, BLK, HEAD_DIM), F32), pltpu.VMEM((chains, BLK, 2 * HEAD_DIM), F32),
        ],
        compiler_params=_params(3),
        name="moba_attention",
    )(tab, bias, proj, proj, proj)


def _twice(tile):
    return jnp.concatenate([tile, tile], axis=0)


def _diff_kernel(tab_ref, bias_ref, lq_ref, gs_ref, q_ref, k_ref, v_ref, o_ref,
                 q2_ref, m_ref, acc_ref, *, lam_init, heads):
    hg, qb = pl.program_id(0), pl.program_id(2)
    group = k_ref.shape[0]
    chains = [(hh, g) for hh in range(heads) for g in range(group)]

    lane = lax.broadcasted_iota(jnp.int32, (BLK, HEAD_DIM), 1)
    for c, (hh, g) in enumerate(chains):
        q = q_ref[g, :, _head_cols(hh)]
        zero = jnp.zeros_like(q)
        q2_ref[c] = jnp.concatenate([jnp.where(lane < DIFF_HALF, q, zero),
                                     jnp.where(lane >= DIFF_HALF, q, zero)], axis=0)

    def span_scores(c, j, n):
        hh, g = chains[c]
        rows = pl.ds(pl.multiple_of(j * BLK, BLK), n * BLK)
        s = lax.dot_general(q2_ref[c], k_ref[g, rows, _head_cols(hh)], NT_DIMS,
                            preferred_element_type=F32)
        return s, v_ref[g, rows, _head_cols(hh)]

    @pl.when(qb == 0)
    def _():
        for c, (hh, g) in enumerate(chains):
            s, v = span_scores(c, qb, 1)
            _softmax_step(s + _twice(bias_ref[hh, :, BLK:]), v, m_ref.at[c], acc_ref.at[c],
                          first=True)

    @pl.when(qb >= 1)
    def _():
        for c, (hh, g) in enumerate(chains):
            s, v = span_scores(c, qb - 1, 2)
            _softmax_step(s + _twice(bias_ref[hh]), v, m_ref.at[c], acc_ref.at[c], first=True)

    far_bias = [tab_ref[hg * heads + hh, N_BUCKETS - 1] * LOG2E for hh in range(heads)]

    def far_pair(p, carry):
        for c, (hh, g) in enumerate(chains):
            s, v = span_scores(c, 2 * p, 2)
            _softmax_step(s, v, m_ref.at[c], acc_ref.at[c], first=False, shift=far_bias[hh])
        return carry

    lax.fori_loop(0, jnp.maximum(qb - 1, 0) // 2, far_pair, 0)

    @pl.when((qb >= 2) & (qb % 2 == 0))
    def _():
        for c, (hh, g) in enumerate(chains):
            s, v = span_scores(c, qb - 2, 1)
            _softmax_step(s, v, m_ref.at[c], acc_ref.at[c], first=False, shift=far_bias[hh])

    lq = lq_ref[...]
    lam = (jnp.exp(jnp.sum(lq[0:1] * lq[1:2], axis=-1, keepdims=True))
           - jnp.exp(jnp.sum(lq[2:3] * lq[3:4], axis=-1, keepdims=True)) + lam_init)
    for c, (hh, g) in enumerate(chains):
        o_all = _softmax_result(acc_ref.at[c])
        o = o_all[:BLK] - lam * o_all[BLK:]
        o_ref[g, :, _head_cols(hh)] = (
            _rmsnorm_rows(o, gs_ref[...]) * (1.0 - lam_init)).astype(o_ref.dtype)


def diff_attention(proj, kv, tab, bias, lq, g_subln, *, group, heads, lam_init):
    batch, seq, _ = proj.shape
    nqb = seq // BLK
    v_off = N_SELF_HEADS // heads
    width = heads * HEAD_DIM
    chains = heads * group
    kern = functools.partial(_diff_kernel, lam_init=lam_init, heads=heads)
    return pl.pallas_call(
        kern,
        grid=(N_SELF_HEADS // heads, batch // group, nqb),
        in_specs=[
            pl.BlockSpec(memory_space=pltpu.SMEM),
            pl.BlockSpec((heads, BLK, 2 * BLK), lambda h, b, qb: (h, 0, 0)),
            pl.BlockSpec((4, DIFF_HALF), lambda h, b, qb: (0, 0)),
            pl.BlockSpec((1, HEAD_DIM), lambda h, b, qb: (0, 0)),
            pl.BlockSpec((group, BLK, width), lambda h, b, qb: (b, qb, h)),
            pl.BlockSpec((group, seq, width), lambda h, b, qb: (b, 0, h)),
            pl.BlockSpec((group, seq, width), lambda h, b, qb: (b, 0, v_off + h)),
        ],
        out_specs=pl.BlockSpec((group, BLK, width), lambda h, b, qb: (b, qb, h)),
        out_shape=jax.ShapeDtypeStruct((batch, seq, SELF_WIDTH), BF16),
        scratch_shapes=[
            pltpu.VMEM((chains, 2 * BLK, HEAD_DIM), BF16),
            pltpu.VMEM((chains, 2 * BLK, HEAD_DIM), F32),
            pltpu.VMEM((chains, 2 * BLK, 2 * HEAD_DIM), F32),
        ],
        compiler_params=_params(3),
        name="diff_attention",
    )(tab, bias, lq, g_subln.reshape(1, HEAD_DIM), proj, kv, kv)


def _mem_kernel(q_ref, mk_ref, mv_ref, o_ref):
    for hh in range(N_MEM_HEADS):
        cols = _head_cols(hh)
        s = lax.dot_general(q_ref[:, cols], mk_ref[:, cols], NT_DIMS, preferred_element_type=F32)
        p = jnp.exp2(s - jnp.max(s, axis=-1, keepdims=True))
        mv = mv_ref[:, cols]
        pv = jnp.dot(p.astype(BF16), jnp.concatenate([mv, jnp.ones(mv.shape, mv.dtype)], axis=1),
                     preferred_element_type=F32)
        o_ref[:, cols] = (pv[:, :HEAD_DIM] / pv[:, HEAD_DIM:]).astype(o_ref.dtype)


def _mem_kv_kernel(x_ref, g_ref, w_ref, o_ref, xn_ref):
    @pl.when(pl.program_id(1) == 0)
    def _():
        xn_ref[...] = _rmsnorm_rows(x_ref[...], g_ref[...]).astype(BF16)

    o_ref[...] = jnp.dot(xn_ref[...], w_ref[...].astype(BF16),
                         preferred_element_type=F32).astype(o_ref.dtype)


def mem_kv(mem, g_mem, w_mem_kv, *, tn):
    m, d = mem.shape
    depth, _, n = w_mem_kv.shape
    return pl.pallas_call(
        _mem_kv_kernel,
        grid=(depth, n // tn),
        in_specs=[
            pl.BlockSpec((m, d), lambda l, j: (0, 0)),
            pl.BlockSpec((None, 1, d), lambda l, j: (l, 0, 0)),
            pl.BlockSpec((None, d, tn), lambda l, j: (l, 0, j)),
        ],
        out_specs=pl.BlockSpec((None, m, tn), lambda l, j: (l, 0, j)),
        out_shape=jax.ShapeDtypeStruct((depth, m, n), BF16),
        scratch_shapes=[pltpu.VMEM((m, d), BF16)],
        compiler_params=_params(2),
        name="mem_kv",
    )(mem, g_mem.reshape(depth, 1, d), w_mem_kv)


def mem_attention(proj, mkv, layer, *, batch, seq, n_mem, q_col, tq):
    nsb = seq // tq
    q_blk = q_col // MEM_WIDTH
    return pl.pallas_call(
        _mem_kernel,
        grid=(batch, nsb),
        in_specs=[
            pl.BlockSpec((tq, MEM_WIDTH), lambda b, sb: (b * nsb + sb, q_blk)),
            pl.BlockSpec((None, n_mem, MEM_WIDTH), lambda b, sb: (layer, b, 0)),
            pl.BlockSpec((None, n_mem, MEM_WIDTH), lambda b, sb: (layer, b, 1)),
        ],
        out_specs=pl.BlockSpec((tq, MEM_WIDTH), lambda b, sb: (b * nsb + sb, 0)),
        out_shape=jax.ShapeDtypeStruct((batch * seq, MEM_WIDTH), BF16),
        compiler_params=_params(2),
        name="mem_attention",
    )(proj, mkv, mkv)


ATTN_GROUP = 4
ATTN_HEADS = 4


def kernel(x, mem, rel_bias, g_mix, w_in_a, w_in_b, g_mem, w_mem_kv, w_o, g_ffn, w_gate_up, w_down,
           g_kv, w_kv_shared, lambda_qk, g_subln, g_final):
    batch, seq, d = x.shape
    n_mem = mem.shape[1]
    depth = g_mix.shape[0]
    n_a = w_in_a.shape[0]
    tokens = batch * seq
    group = math.gcd(batch, ATTN_GROUP)

    w_kv_shared = w_kv_shared[None]

    tab = rel_bias.T.astype(F32)
    bias = bias_tiles(tab, *_bucket_tiles())

    moba_scale = HEAD_DIM ** -0.5 * LOG2E
    diff_scale = DIFF_HALF ** -0.5 * LOG2E
    mem_scale = HEAD_DIM ** -0.5 * LOG2E
    ones = functools.partial(jnp.ones, dtype=F32)
    cs_a = jnp.concatenate([jnp.full((SELF_WIDTH,), moba_scale, F32), ones((2 * SELF_WIDTH,)),
                            jnp.full((MEM_WIDTH,), mem_scale, F32)])
    cs_b = jnp.concatenate([jnp.full((SELF_WIDTH,), diff_scale, F32),
                            jnp.full((MEM_WIDTH,), mem_scale, F32)])

    h = x.reshape(tokens, d)
    mem2 = mem.reshape(batch * n_mem, d)
    mkv = mem_kv(mem2, g_mem, w_mem_kv, tn=512)
    kv = None
    for l in range(depth):
        if l < n_a:
            proj = norm_matmul(h, g_mix[l], w_in_a, l, cs_a, tm=1024, tn=1024)
            y_self = moba_attention(proj.reshape(batch, seq, -1), tab, bias,
                                    group=group, heads=ATTN_HEADS, gate_fill=NEG * moba_scale)
            q_col = 3 * SELF_WIDTH
        else:
            j = l - n_a
            if kv is None:
                kv = norm_matmul(h, g_kv, w_kv_shared, 0, ones((2 * SELF_WIDTH,)), tm=1024, tn=1024)
                kv = kv.reshape(batch, seq, -1)
            proj = norm_matmul(h, g_mix[l], w_in_b, j, cs_b, tm=1024, tn=1024)
            lam_init = 0.8 - 0.6 * math.exp(-0.3 * l)
            y_self = diff_attention(proj.reshape(batch, seq, -1), kv, tab, bias,
                                    lambda_qk[j], g_subln[j], group=group, heads=ATTN_HEADS,
                                    lam_init=lam_init)
            q_col = SELF_WIDTH
        y_mem = mem_attention(proj, mkv, l, batch=batch, seq=seq, n_mem=n_mem,
                              q_col=q_col, tq=1024)
        h = residual_matmul2(h, y_self.reshape(tokens, -1), y_mem, w_o, l, tm=1024, tn=1024)
        h = ffn_block(h, g_ffn[l], w_gate_up, w_down, l, tm=1024, tc=256,
                      final_gain=g_final if l == depth - 1 else None)
    return h.reshape(batch, seq, d)
```

```python
import functools
import math

import jax
import jax.numpy as jnp
import numpy as np
from jax import lax
from jax.experimental import pallas as pl
from jax.experimental.pallas import tpu as pltpu

D_MODEL = 2048
HEAD_DIM = 128
N_MEM_HEADS = 4
MEM_WIDTH = N_MEM_HEADS * HEAD_DIM
SELF_WIDTH = D_MODEL - MEM_WIDTH
N_SELF_HEADS = SELF_WIDTH // HEAD_DIM
DIFF_HALF = HEAD_DIM // 2
MOBA_BLOCK = 256
MOBA_TOPK = 3
N_BUCKETS = 32
MAX_DISTANCE = 128
RMS_EPS = 1e-6
NEG = -1e30
LOG2E = math.log2(math.e)

BLK = MOBA_BLOCK
V7X_VMEM_BYTES = 64 * 1024 * 1024
VMEM_LIMIT = V7X_VMEM_BYTES - 4 * 1024 * 1024

F32 = jnp.float32
BF16 = jnp.bfloat16
NT_DIMS = (((1,), (1,)), ((), ()))


def _params(n_grid_axes):
    return pltpu.CompilerParams(
        dimension_semantics=("arbitrary",) * n_grid_axes, vmem_limit_bytes=VMEM_LIMIT)


def _rmsnorm_rows(x, g):
    ms = jnp.mean(x * x, axis=-1, keepdims=True)
    return x * lax.rsqrt(ms + RMS_EPS) * g


ROW_PARTS = 4


def _staggered_row_specs(tm, d, n_tiles, n_steps):
    part = tm // ROW_PARTS
    last = n_tiles * ROW_PARTS - 1
    specs = []
    for q in range(ROW_PARTS):
        s_q = 1 + q * n_steps // ROW_PARTS

        def index(i, c, q=q, s_q=s_q):
            ahead = jnp.where(c >= s_q, 1, 0) if s_q < n_steps else 0
            return (jnp.minimum(ROW_PARTS * (i + ahead) + q, last), 0)

        specs.append(pl.BlockSpec((part, d), index))
    return specs


def _row_part(ref, q):
    part = ref.shape[0] // ROW_PARTS
    return slice(q * part, (q + 1) * part)


def _norm_matmul_kernel(*refs):
    x_parts = refs[:ROW_PARTS]
    g_ref, w_ref, cs_ref, o_ref, xn_ref = refs[ROW_PARTS:]

    def project():
        acc = jnp.dot(xn_ref[...], w_ref[...].astype(BF16), preferred_element_type=F32)
        o_ref[...] = (acc * cs_ref[...]).astype(o_ref.dtype)

    @pl.when(pl.program_id(1) == 0)
    def _():
        for q, x_ref in enumerate(x_parts):
            xn_ref[_row_part(xn_ref, q), :] = _rmsnorm_rows(x_ref[...], g_ref[...]).astype(BF16)
        project()

    @pl.when(pl.program_id(1) > 0)
    def _():
        project()


def norm_matmul(x, g, w, layer, col_scale, *, tm, tn):
    m, d = x.shape
    n = w.shape[-1]
    return pl.pallas_call(
        _norm_matmul_kernel,
        grid=(m // tm, n // tn),
        in_specs=_staggered_row_specs(tm, d, m // tm, n // tn) + [
            pl.BlockSpec((1, d), lambda i, j: (0, 0)),
            pl.BlockSpec((None, d, tn), lambda i, j: (layer, 0, j)),
            pl.BlockSpec((1, tn), lambda i, j: (0, j)),
        ],
        out_specs=pl.BlockSpec((tm, tn), lambda i, j: (i, j)),
        out_shape=jax.ShapeDtypeStruct((m, n), BF16),
        scratch_shapes=[pltpu.VMEM((tm, d), BF16)],
        compiler_params=_params(2),
        name="norm_matmul",
    )(*([x] * ROW_PARTS), g.reshape(1, d), w, col_scale.reshape(1, n))


def _ffn_kernel(h_ref, g_ref, gf_ref, wgu_hbm, wd_hbm, o_ref, xn_ref, *, layer, tc, n_slab,
                final_norm):
    d = o_ref.shape[1]
    n_chunks = wd_hbm.shape[1] // tc
    h = h_ref[...]
    xn_ref[...] = _rmsnorm_rows(h, g_ref[...]).astype(BF16)
    o_ref[...] = h

    def add_chunk(wg_ref, wu_ref, wd_ref):
        xn = xn_ref[...]
        gate = jnp.dot(xn, wg_ref[...].astype(BF16), preferred_element_type=F32)
        up = jnp.dot(xn, wu_ref[...].astype(BF16), preferred_element_type=F32)
        act = (gate * jax.nn.sigmoid(gate) * up).astype(BF16)
        slab = d // n_slab
        for n in range(n_slab):
            cols = slice(n * slab, (n + 1) * slab)
            o_ref[:, cols] += jnp.dot(act, wd_ref[:, cols].astype(BF16), preferred_element_type=F32)

    pltpu.emit_pipeline(
        add_chunk,
        grid=(n_chunks,),
        in_specs=[
            pl.BlockSpec((d, tc), lambda c: (0, c)),
            pl.BlockSpec((d, tc), lambda c: (0, c + n_chunks)),
            pl.BlockSpec((tc, d), lambda c: (c, 0)),
        ],
    )(wgu_hbm.at[layer], wgu_hbm.at[layer], wd_hbm.at[layer])

    if final_norm:
        o_ref[...] = _rmsnorm_rows(o_ref[...], gf_ref[...])


def ffn_block(h, g, w_gate_up, w_down, layer, *, tm, tc, final_gain=None):
    m, d = h.shape
    final_norm = final_gain is not None
    gf = final_gain if final_norm else g
    return pl.pallas_call(
        functools.partial(_ffn_kernel, layer=layer, tc=tc, n_slab=4, final_norm=final_norm),
        grid=(m // tm,),
        in_specs=[
            pl.BlockSpec((tm, d), lambda i: (i, 0)),
            pl.BlockSpec((1, d), lambda i: (0, 0)),
            pl.BlockSpec((1, d), lambda i: (0, 0)),
            pl.BlockSpec(memory_space=pl.ANY),
            pl.BlockSpec(memory_space=pl.ANY),
        ],
        out_specs=pl.BlockSpec((tm, d), lambda i: (i, 0)),
        out_shape=jax.ShapeDtypeStruct((m, d), F32),
        scratch_shapes=[pltpu.VMEM((tm, d), BF16)],
        compiler_params=_params(1),
        name="ffn_block",
    )(h, g.reshape(1, d), gf.reshape(1, d), w_gate_up, w_down)


def _residual_matmul2_kernel(h_ref, a1_ref, a2_ref, w1_ref, w2_ref, o_ref):
    acc = jnp.dot(a1_ref[...], w1_ref[...].astype(BF16), preferred_element_type=F32)
    acc = acc + jnp.dot(a2_ref[...], w2_ref[...].astype(BF16), preferred_element_type=F32)
    o_ref[...] = h_ref[...] + acc


def residual_matmul2(h, a1, a2, w, layer, *, tm, tn):
    m, n = h.shape
    k1, k2 = a1.shape[1], a2.shape[1]
    assert k1 % k2 == 0
    return pl.pallas_call(
        _residual_matmul2_kernel,
        grid=(m // tm, n // tn),
        in_specs=[
            pl.BlockSpec((tm, tn), lambda i, j: (i, j)),
            pl.BlockSpec((tm, k1), lambda i, j: (i, 0)),
            pl.BlockSpec((tm, k2), lambda i, j: (i, 0)),
            pl.BlockSpec((None, k1, tn), lambda i, j: (layer, 0, j)),
            pl.BlockSpec((None, k2, tn), lambda i, j: (layer, k1 // k2, j)),
        ],
        out_specs=pl.BlockSpec((tm, tn), lambda i, j: (i, j)),
        out_shape=jax.ShapeDtypeStruct((m, n), F32),
        compiler_params=_params(2),
        name="residual_matmul2",
    )(h, a1, a2, w, w)


def _rel_bucket_np(rel):
    n = np.maximum(rel, 0)
    max_exact = N_BUCKETS // 2
    nf = np.maximum(n, 1).astype(np.float32)
    large = max_exact + (np.log(nf / np.float32(max_exact)) / np.float32(math.log(MAX_DISTANCE / max_exact))
                         * np.float32(N_BUCKETS - max_exact)).astype(np.int32)
    large = np.minimum(large, N_BUCKETS - 1)
    return np.where(n < max_exact, n, large).astype(np.int32)


def _bucket_tiles():
    qi = np.arange(BLK, dtype=np.int32)[:, None]
    ki = np.arange(BLK, dtype=np.int32)[None, :]
    diag = _rel_bucket_np(qi - ki)
    prev = _rel_bucket_np(qi - ki + BLK)
    assert _rel_bucket_np(np.array([BLK + 1]))[0] == N_BUCKETS - 1
    return jnp.asarray(diag), jnp.asarray(prev)


def _bias_tile(bkt, tab_ref, head):
    out = jnp.zeros(bkt.shape, F32)
    for b in range(N_BUCKETS):
        out = jnp.where(bkt == b, tab_ref[head, b] * LOG2E, out)
    return out


def _causal_tile(tile):
    qi = lax.broadcasted_iota(jnp.int32, tile.shape, 0)
    ki = lax.broadcasted_iota(jnp.int32, tile.shape, 1)
    return jnp.where(ki <= qi, tile, NEG)


def _bias_kernel(tab_ref, bd_ref, bp_ref, o_ref):
    h = pl.program_id(0)
    o_ref[...] = jnp.concatenate(
        [_bias_tile(bp_ref[...], tab_ref, h),
         _causal_tile(_bias_tile(bd_ref[...], tab_ref, h))], axis=1)


def bias_tiles(tab, bkt_diag, bkt_prev):
    n_heads = tab.shape[0]
    return pl.pallas_call(
        _bias_kernel,
        grid=(n_heads,),
        in_specs=[
            pl.BlockSpec(memory_space=pltpu.SMEM),
            pl.BlockSpec((BLK, BLK), lambda h: (0, 0)),
            pl.BlockSpec((BLK, BLK), lambda h: (0, 0)),
        ],
        out_specs=pl.BlockSpec((None, BLK, 2 * BLK), lambda h: (h, 0, 0)),
        out_shape=jax.ShapeDtypeStruct((n_heads, BLK, 2 * BLK), F32),
        compiler_params=_params(1),
        name="bias_tiles",
    )(tab, bkt_diag, bkt_prev)


def _wide(x, width):
    return jnp.tile(x, (1, width // HEAD_DIM))


def _softmax_step(s, v, m_ref, acc_ref, *, first, shift=None):
    v_aug = jnp.concatenate([v, jnp.ones(v.shape, v.dtype)], axis=1)
    m_cur = jnp.max(s, axis=-1, keepdims=True)
    if shift is not None:
        m_cur = m_cur + shift
    if first:
        m_new = jnp.broadcast_to(m_cur, m_ref.shape)
    else:
        m_old = m_ref[...]
        m_new = jnp.maximum(m_old, m_cur)
    p = jnp.exp2(s - _wide(m_new if shift is None else m_new - shift, s.shape[1]))
    pv = jnp.dot(p.astype(BF16), v_aug, preferred_element_type=F32)
    if first:
        acc_ref[...] = pv
    else:
        acc_ref[...] = _wide(jnp.exp2(m_old - m_new), acc_ref.shape[1]) * acc_ref[...] + pv
    m_ref[...] = m_new


def _softmax_result(acc_ref):
    acc = acc_ref[...]
    return acc[:, :HEAD_DIM] / acc[:, HEAD_DIM:]


def _head_cols(hh):
    return slice(hh * HEAD_DIM, (hh + 1) * HEAD_DIM)


def _moba_kernel(tab_ref, bias_ref, q_ref, k_ref, v_ref, o_ref,
                 kmean_ref, qa_ref, m_ref, acc_ref, *, gate_fill, heads):
    hg, qb = pl.program_id(0), pl.program_id(2)
    group, seq, _ = k_ref.shape
    nb = seq // BLK
    chains = [(hh, g) for hh in range(heads) for g in range(group)]

    @pl.when(qb == 0)
    def _():
        for c, (hh, g) in enumerate(chains):
            kf = k_ref[g, :, _head_cols(hh)].astype(F32).reshape(nb, BLK, HEAD_DIM)
            km = jnp.mean(kf, axis=1)
            km_hi = km.astype(BF16)
            km_lo = (km - km_hi.astype(F32)).astype(BF16)
            kmean_ref[c] = jnp.concatenate([km_hi, km_lo], axis=0)

    row = lax.broadcasted_iota(jnp.int32, (nb, BLK), 0)
    past = row < qb
    n_sel = max(1, min(MOBA_TOPK, nb - 1))
    for c, (hh, g) in enumerate(chains):
        q = q_ref[g, :, _head_cols(hh)]
        parts = lax.dot_general(kmean_ref[c], q, NT_DIMS, preferred_element_type=F32)
        gate = jnp.where(past, parts[:nb] + parts[nb:], gate_fill)
        beaten_by = jnp.zeros(gate.shape, F32)
        for i in range(nb):
            gi = gate[i:i + 1, :]
            wins = (gi > gate) | ((gi == gate) & (i < row))
            beaten_by = beaten_by + jnp.where(wins, 1.0, 0.0)
        keep = ((beaten_by < n_sel) & past) | (row == qb)
        sel_neg = jnp.where(keep, 0.0, NEG)
        sel_neg = jnp.concatenate([sel_neg, jnp.zeros((HEAD_DIM - nb, BLK), F32)], axis=0)
        qa_ref[c] = jnp.concatenate([q, sel_neg.T.astype(BF16)], axis=1)


    def span_scores(c, j, n):
        hh, g = chains[c]
        rows = pl.ds(pl.multiple_of(j * BLK, BLK), n * BLK)
        kj = k_ref[g, rows, _head_cols(hh)]
        lane = lax.broadcasted_iota(jnp.int32, (n * BLK, HEAD_DIM), 1)
        blk = j + lax.broadcasted_iota(jnp.int32, (n * BLK, HEAD_DIM), 0) // BLK
        onehot = jnp.where(lane == blk, 1.0, 0.0).astype(BF16)
        k_aug = jnp.concatenate([kj, onehot], axis=1)
        s = lax.dot_general(qa_ref[c], k_aug, NT_DIMS, preferred_element_type=F32)
        return s, v_ref[g, rows, _head_cols(hh)]

    @pl.when(qb == 0)
    def _():
        for c, (hh, g) in enumerate(chains):
            s, v = span_scores(c, qb, 1)
            _softmax_step(s + bias_ref[hh, :, BLK:], v, m_ref.at[c], acc_ref.at[c], first=True)

    @pl.when(qb >= 1)
    def _():
        for c, (hh, g) in enumerate(chains):
            s, v = span_scores(c, qb - 1, 2)
            _softmax_step(s + bias_ref[hh], v, m_ref.at[c], acc_ref.at[c], first=True)

    far_bias = [tab_ref[hg * heads + hh, N_BUCKETS - 1] * LOG2E for hh in range(heads)]

    def far_pair(p, carry):
        for c, (hh, g) in enumerate(chains):
            s, v = span_scores(c, 2 * p, 2)
            _softmax_step(s, v, m_ref.at[c], acc_ref.at[c], first=False, shift=far_bias[hh])
        return carry

    lax.fori_loop(0, jnp.maximum(qb - 1, 0) // 2, far_pair, 0)

    @pl.when((qb >= 2) & (qb % 2 == 0))
    def _():
        for c, (hh, g) in enumerate(chains):
            s, v = span_scores(c, qb - 2, 1)
            _softmax_step(s, v, m_ref.at[c], acc_ref.at[c], first=False, shift=far_bias[hh])

    for c, (hh, g) in enumerate(chains):
        o_ref[g, :, _head_cols(hh)] = _softmax_result(acc_ref.at[c]).astype(o_ref.dtype)


def moba_attention(proj, tab, bias, *, group, heads, gate_fill):
    batch, seq, _ = proj.shape
    nqb = seq // BLK
    k_off = N_SELF_HEADS // heads
    width = heads * HEAD_DIM
    chains = heads * group
    kern = functools.partial(_moba_kernel, gate_fill=gate_fill, heads=heads)
    return pl.pallas_call(
        kern,
        grid=(N_SELF_HEADS // heads, batch // group, nqb),
        in_specs=[
            pl.BlockSpec(memory_space=pltpu.SMEM),
            pl.BlockSpec((heads, BLK, 2 * BLK), lambda h, b, qb: (h, 0, 0)),
            pl.BlockSpec((group, BLK, width), lambda h, b, qb: (b, qb, h)),
            pl.BlockSpec((group, seq, width), lambda h, b, qb: (b, 0, k_off + h)),
            pl.BlockSpec((group, seq, width), lambda h, b, qb: (b, 0, 2 * k_off + h)),
        ],
        out_specs=pl.BlockSpec((group, BLK, width), lambda h, b, qb: (b, qb, h)),
        out_shape=jax.ShapeDtypeStruct((batch, seq, SELF_WIDTH), BF16),
        scratch_shapes=[
            pltpu.VMEM((chains, 2 * (seq // BLK), HEAD_DIM), BF16),
            pltpu.VMEM((chains, BLK, 2 * HEAD_DIM), BF16),
            pltpu.VMEM((chains, BLK, HEAD_DIM), F32), pltpu.VMEM((chains, BLK, 2 * HEAD_DIM), F32),
        ],
        compiler_params=_params(3),
        name="moba_attention",
    )(tab, bias, proj, proj, proj)


def _twice(tile):
    return jnp.concatenate([tile, tile], axis=0)


def _diff_kernel(tab_ref, bias_ref, lq_ref, gs_ref, q_ref, k_ref, v_ref, o_ref,
                 q2_ref, m_ref, acc_ref, *, lam_init, heads):
    hg, qb = pl.program_id(0), pl.program_id(2)
    group = k_ref.shape[0]
    chains = [(hh, g) for hh in range(heads) for g in range(group)]

    lane = lax.broadcasted_iota(jnp.int32, (BLK, HEAD_DIM), 1)
    for c, (hh, g) in enumerate(chains):
        q = q_ref[g, :, _head_cols(hh)]
        zero = jnp.zeros_like(q)
        q2_ref[c] = jnp.concatenate([jnp.where(lane < DIFF_HALF, q, zero),
                                     jnp.where(lane >= DIFF_HALF, q, zero)], axis=0)

    def span_scores(c, j, n):
        hh, g = chains[c]
        rows = pl.ds(pl.multiple_of(j * BLK, BLK), n * BLK)
        s = lax.dot_general(q2_ref[c], k_ref[g, rows, _head_cols(hh)], NT_DIMS,
                            preferred_element_type=F32)
        return s, v_ref[g, rows, _head_cols(hh)]

    @pl.when(qb == 0)
    def _():
        for c, (hh, g) in enumerate(chains):
            s, v = span_scores(c, qb, 1)
            _softmax_step(s + _twice(bias_ref[hh, :, BLK:]), v, m_ref.at[c], acc_ref.at[c],
                          first=True)

    @pl.when(qb >= 1)
    def _():
        for c, (hh, g) in enumerate(chains):
            s, v = span_scores(c, qb - 1, 2)
            _softmax_step(s + _twice(bias_ref[hh]), v, m_ref.at[c], acc_ref.at[c], first=True)

    far_bias = [tab_ref[hg * heads + hh, N_BUCKETS - 1] * LOG2E for hh in range(heads)]

    def far_pair(p, carry):
        for c, (hh, g) in enumerate(chains):
            s, v = span_scores(c, 2 * p, 2)
            _softmax_step(s, v, m_ref.at[c], acc_ref.at[c], first=False, shift=far_bias[hh])
        return carry

    lax.fori_loop(0, jnp.maximum(qb - 1, 0) // 2, far_pair, 0)

    @pl.when((qb >= 2) & (qb % 2 == 0))
    def _():
        for c, (hh, g) in enumerate(chains):
            s, v = span_scores(c, qb - 2, 1)
            _softmax_step(s, v, m_ref.at[c], acc_ref.at[c], first=False, shift=far_bias[hh])

    lq = lq_ref[...]
    lam = (jnp.exp(jnp.sum(lq[0:1] * lq[1:2], axis=-1, keepdims=True))
           - jnp.exp(jnp.sum(lq[2:3] * lq[3:4], axis=-1, keepdims=True)) + lam_init)
    for c, (hh, g) in enumerate(chains):
        o_all = _softmax_result(acc_ref.at[c])
        o = o_all[:BLK] - lam * o_all[BLK:]
        o_ref[g, :, _head_cols(hh)] = (
            _rmsnorm_rows(o, gs_ref[...]) * (1.0 - lam_init)).astype(o_ref.dtype)


def diff_attention(proj, kv, tab, bias, lq, g_subln, *, group, heads, lam_init):
    batch, seq, _ = proj.shape
    nqb = seq // BLK
    v_off = N_SELF_HEADS // heads
    width = heads * HEAD_DIM
    chains = heads * group
    kern = functools.partial(_diff_kernel, lam_init=lam_init, heads=heads)
    return pl.pallas_call(
        kern,
        grid=(N_SELF_HEADS // heads, batch // group, nqb),
        in_specs=[
            pl.BlockSpec(memory_space=pltpu.SMEM),
            pl.BlockSpec((heads, BLK, 2 * BLK), lambda h, b, qb: (h, 0, 0)),
            pl.BlockSpec((4, DIFF_HALF), lambda h, b, qb: (0, 0)),
            pl.BlockSpec((1, HEAD_DIM), lambda h, b, qb: (0, 0)),
            pl.BlockSpec((group, BLK, width), lambda h, b, qb: (b, qb, h)),
            pl.BlockSpec((group, seq, width), lambda h, b, qb: (b, 0, h)),
            pl.BlockSpec((group, seq, width), lambda h, b, qb: (b, 0, v_off + h)),
        ],
        out_specs=pl.BlockSpec((group, BLK, width), lambda h, b, qb: (b, qb, h)),
        out_shape=jax.ShapeDtypeStruct((batch, seq, SELF_WIDTH), BF16),
        scratch_shapes=[
            pltpu.VMEM((chains, 2 * BLK, HEAD_DIM), BF16),
            pltpu.VMEM((chains, 2 * BLK, HEAD_DIM), F32),
            pltpu.VMEM((chains, 2 * BLK, 2 * HEAD_DIM), F32),
        ],
        compiler_params=_params(3),
        name="diff_attention",
    )(tab, bias, lq, g_subln.reshape(1, HEAD_DIM), proj, kv, kv)


def _mem_kernel(q_ref, mk_ref, mv_ref, o_ref):
    for hh in range(N_MEM_HEADS):
        cols = _head_cols(hh)
        s = lax.dot_general(q_ref[:, cols], mk_ref[:, cols], NT_DIMS, preferred_element_type=F32)
        p = jnp.exp2(s - jnp.max(s, axis=-1, keepdims=True))
        mv = mv_ref[:, cols]
        pv = jnp.dot(p.astype(BF16), jnp.concatenate([mv, jnp.ones(mv.shape, mv.dtype)], axis=1),
                     preferred_element_type=F32)
        o_ref[:, cols] = (pv[:, :HEAD_DIM] / pv[:, HEAD_DIM:]).astype(o_ref.dtype)


def _mem_kv_kernel(x_ref, g_ref, w_ref, o_ref, xn_ref):
    @pl.when(pl.program_id(1) == 0)
    def _():
        xn_ref[...] = _rmsnorm_rows(x_ref[...], g_ref[...]).astype(BF16)

    o_ref[...] = jnp.dot(xn_ref[...], w_ref[...].astype(BF16),
                         preferred_element_type=F32).astype(o_ref.dtype)


def mem_kv(mem, g_mem, w_mem_kv, *, tn):
    m, d = mem.shape
    depth, _, n = w_mem_kv.shape
    return pl.pallas_call(
        _mem_kv_kernel,
        grid=(depth, n // tn),
        in_specs=[
            pl.BlockSpec((m, d), lambda l, j: (0, 0)),
            pl.BlockSpec((None, 1, d), lambda l, j: (l, 0, 0)),
            pl.BlockSpec((None, d, tn), lambda l, j: (l, 0, j)),
        ],
        out_specs=pl.BlockSpec((None, m, tn), lambda l, j: (l, 0, j)),
        out_shape=jax.ShapeDtypeStruct((depth, m, n), BF16),
        scratch_shapes=[pltpu.VMEM((m, d), BF16)],
        compiler_params=_params(2),
        name="mem_kv",
    )(mem, g_mem.reshape(depth, 1, d), w_mem_kv)


def mem_attention(proj, mkv, layer, *, batch, seq, n_mem, q_col, tq):
    nsb = seq // tq
    q_blk = q_col // MEM_WIDTH
    return pl.pallas_call(
        _mem_kernel,
        grid=(batch, nsb),
        in_specs=[
            pl.BlockSpec((tq, MEM_WIDTH), lambda b, sb: (b * nsb + sb, q_blk)),
            pl.BlockSpec((None, n_mem, MEM_WIDTH), lambda b, sb: (layer, b, 0)),
            pl.BlockSpec((None, n_mem, MEM_WIDTH), lambda b, sb: (layer, b, 1)),
        ],
        out_specs=pl.BlockSpec((tq, MEM_WIDTH), lambda b, sb: (b * nsb + sb, 0)),
        out_shape=jax.ShapeDtypeStruct((batch * seq, MEM_WIDTH), BF16),
        compiler_params=_params(2),
        name="mem_attention",
    )(proj, mkv, mkv)


ATTN_GROUP = 4
ATTN_HEADS = 4


def kernel(x, mem, rel_bias, g_mix, w_in_a, w_in_b, g_mem, w_mem_kv, w_o, g_ffn, w_gate_up, w_down,
           g_kv, w_kv_shared, lambda_qk, g_subln, g_final):
    batch, seq, d = x.shape
    n_mem = mem.shape[1]
    depth = g_mix.shape[0]
    n_a = w_in_a.shape[0]
    tokens = batch * seq
    group = math.gcd(batch, ATTN_GROUP)

    w_kv_shared = w_kv_shared[None]

    tab = rel_bias.T.astype(F32)
    bias = bias_tiles(tab, *_bucket_tiles())

    moba_scale = HEAD_DIM ** -0.5 * LOG2E
    diff_scale = DIFF_HALF ** -0.5 * LOG2E
    mem_scale = HEAD_DIM ** -0.5 * LOG2E
    ones = functools.partial(jnp.ones, dtype=F32)
    cs_a = jnp.concatenate([jnp.full((SELF_WIDTH,), moba_scale, F32), ones((2 * SELF_WIDTH,)),
                            jnp.full((MEM_WIDTH,), mem_scale, F32)])
    cs_b = jnp.concatenate([jnp.full((SELF_WIDTH,), diff_scale, F32),
                            jnp.full((MEM_WIDTH,), mem_scale, F32)])

    h = x.reshape(tokens, d)
    mem2 = mem.reshape(batch * n_mem, d)
    mkv = mem_kv(mem2, g_mem, w_mem_kv, tn=512)
    kv = None
    for l in range(depth):
        if l < n_a:
            proj = norm_matmul(h, g_mix[l], w_in_a, l, cs_a, tm=1024, tn=1024)
            y_self = moba_attention(proj.reshape(batch, seq, -1), tab, bias,
                                    group=group, heads=ATTN_HEADS, gate_fill=NEG * moba_scale)
            q_col = 3 * SELF_WIDTH
        else:
            j = l - n_a
            if kv is None:
                kv = norm_matmul(h, g_kv, w_kv_shared, 0, ones((2 * SELF_WIDTH,)), tm=1024, tn=1024)
                kv = kv.reshape(batch, seq, -1)
            proj = norm_matmul(h, g_mix[l], w_in_b, j, cs_b, tm=1024, tn=1024)
            lam_init = 0.8 - 0.6 * math.exp(-0.3 * l)
            y_self = diff_attention(proj.reshape(batch, seq, -1), kv, tab, bias,
                                    lambda_qk[j], g_subln[j], group=group, heads=ATTN_HEADS,
                                    lam_init=lam_init)
            q_col = SELF_WIDTH
        y_mem = mem_attention(proj, mkv, l, batch=batch, seq=seq, n_mem=n_mem,
                              q_col=q_col, tq=1024)
        h = residual_matmul2(h, y_self.reshape(tokens, -1), y_mem, w_o, l, tm=2048, tn=512)
        h = ffn_block(h, g_ffn[l], w_gate_up, w_down, l, tm=1024, tc=256,
                      final_gain=g_final if l == depth - 1 else None)
    return h.reshape(batch, seq, d)
```

```python
import functools
import math

import jax
import jax.numpy as jnp
import numpy as np
from jax import lax
from jax.experimental import pallas as pl
from jax.experimental.pallas import tpu as pltpu

D_MODEL = 2048
HEAD_DIM = 128
N_MEM_HEADS = 4
MEM_WIDTH = N_MEM_HEADS * HEAD_DIM
SELF_WIDTH = D_MODEL - MEM_WIDTH
N_SELF_HEADS = SELF_WIDTH // HEAD_DIM
DIFF_HALF = HEAD_DIM // 2
MOBA_BLOCK = 256
MOBA_TOPK = 3
N_BUCKETS = 32
MAX_DISTANCE = 128
RMS_EPS = 1e-6
NEG = -1e30
LOG2E = math.log2(math.e)

BLK = MOBA_BLOCK
V7X_VMEM_BYTES = 64 * 1024 * 1024
VMEM_LIMIT = V7X_VMEM_BYTES - 4 * 1024 * 1024

F32 = jnp.float32
BF16 = jnp.bfloat16
NT_DIMS = (((1,), (1,)), ((), ()))


def _params(n_grid_axes):
    return pltpu.CompilerParams(
        dimension_semantics=("arbitrary",) * n_grid_axes, vmem_limit_bytes=VMEM_LIMIT)


def _rmsnorm_rows(x, g):
    ms = jnp.mean(x * x, axis=-1, keepdims=True)
    return x * lax.rsqrt(ms + RMS_EPS) * g


ROW_PARTS = 4


def _staggered_row_specs(tm, d, n_tiles, n_steps):
    part = tm // ROW_PARTS
    last = n_tiles * ROW_PARTS - 1
    specs = []
    for q in range(ROW_PARTS):
        s_q = 1 + q * n_steps // ROW_PARTS

        def index(i, c, q=q, s_q=s_q):
            ahead = jnp.where(c >= s_q, 1, 0) if s_q < n_steps else 0
            return (jnp.minimum(ROW_PARTS * (i + ahead) + q, last), 0)

        specs.append(pl.BlockSpec((part, d), index))
    return specs


def _row_part(ref, q):
    part = ref.shape[0] // ROW_PARTS
    return slice(q * part, (q + 1) * part)


def _norm_matmul_kernel(*refs):
    x_parts = refs[:ROW_PARTS]
    g_ref, w_ref, cs_ref, o_ref, xn_ref = refs[ROW_PARTS:]

    def project():
        acc = jnp.dot(xn_ref[...], w_ref[...].astype(BF16), preferred_element_type=F32)
        o_ref[...] = (acc * cs_ref[...]).astype(o_ref.dtype)

    @pl.when(pl.program_id(1) == 0)
    def _():
        for q, x_ref in enumerate(x_parts):
            xn_ref[_row_part(xn_ref, q), :] = _rmsnorm_rows(x_ref[...], g_ref[...]).astype(BF16)
        project()

    @pl.when(pl.program_id(1) > 0)
    def _():
        project()


def norm_matmul(x, g, w, layer, col_scale, *, tm, tn):
    m, d = x.shape
    n = w.shape[-1]
    return pl.pallas_call(
        _norm_matmul_kernel,
        grid=(m // tm, n // tn),
        in_specs=_staggered_row_specs(tm, d, m // tm, n // tn) + [
            pl.BlockSpec((1, d), lambda i, j: (0, 0)),
            pl.BlockSpec((None, d, tn), lambda i, j: (layer, 0, j)),
            pl.BlockSpec((1, tn), lambda i, j: (0, j)),
        ],
        out_specs=pl.BlockSpec((tm, tn), lambda i, j: (i, j)),
        out_shape=jax.ShapeDtypeStruct((m, n), BF16),
        scratch_shapes=[pltpu.VMEM((tm, d), BF16)],
        compiler_params=_params(2),
        name="norm_matmul",
    )(*([x] * ROW_PARTS), g.reshape(1, d), w, col_scale.reshape(1, n))


def _ffn_kernel(*refs, n_slab, final_norm):
    h_parts = refs[:ROW_PARTS]
    g_ref, gf_ref, wg_ref, wu_ref, wd_ref, o_ref, xn_ref = refs[ROW_PARTS:]
    c = pl.program_id(1)

    def add_chunk():
        xn = xn_ref[...]
        gate = jnp.dot(xn, wg_ref[...].astype(BF16), preferred_element_type=F32)
        up = jnp.dot(xn, wu_ref[...].astype(BF16), preferred_element_type=F32)
        act = (gate * jax.nn.sigmoid(gate) * up).astype(BF16)
        slab = o_ref.shape[1] // n_slab
        for n in range(n_slab):
            cols = slice(n * slab, (n + 1) * slab)
            o_ref[:, cols] += jnp.dot(act, wd_ref[:, cols].astype(BF16), preferred_element_type=F32)

    @pl.when(c == 0)
    def _():
        for q, h_ref in enumerate(h_parts):
            h = h_ref[...]
            xn_ref[_row_part(xn_ref, q), :] = _rmsnorm_rows(h, g_ref[...]).astype(BF16)
            o_ref[_row_part(o_ref, q), :] = h
        add_chunk()

    @pl.when(c > 0)
    def _():
        add_chunk()

    if final_norm:
        @pl.when(c == pl.num_programs(1) - 1)
        def _():
            o_ref[...] = _rmsnorm_rows(o_ref[...], gf_ref[...])


def ffn_block(h, g, w_gate_up, w_down, layer, *, tm, tc, final_gain=None):
    m, d = h.shape
    d_ff = w_down.shape[1]
    nc = d_ff // tc
    final_norm = final_gain is not None
    gf = final_gain if final_norm else g
    return pl.pallas_call(
        functools.partial(_ffn_kernel, n_slab=4, final_norm=final_norm),
        grid=(m // tm, nc),
        in_specs=_staggered_row_specs(tm, d, m // tm, nc) + [
            pl.BlockSpec((1, d), lambda i, c: (0, 0)),
            pl.BlockSpec((1, d), lambda i, c: (0, 0)),
            pl.BlockSpec((None, d, tc), lambda i, c: (layer, 0, c)),
            pl.BlockSpec((None, d, tc), lambda i, c: (layer, 0, c + nc)),
            pl.BlockSpec((None, tc, d), lambda i, c: (layer, c, 0)),
        ],
        out_specs=pl.BlockSpec((tm, d), lambda i, c: (i, 0)),
        out_shape=jax.ShapeDtypeStruct((m, d), F32),
        scratch_shapes=[pltpu.VMEM((tm, d), BF16)],
        compiler_params=_params(2),
        name="ffn_block",
    )(*([h] * ROW_PARTS), g.reshape(1, d), gf.reshape(1, d), w_gate_up, w_gate_up, w_down)


def _attn_out_kernel(h_ref, a1_ref, qm_ref, mk_ref, mv_ref, w1_ref, w2_ref, o_ref, ymem_ref, *, tq):
    def project():
        acc = jnp.dot(a1_ref[...], w1_ref[...].astype(BF16), preferred_element_type=F32)
        acc = acc + jnp.dot(ymem_ref[...], w2_ref[...].astype(BF16), preferred_element_type=F32)
        o_ref[...] = h_ref[...] + acc

    @pl.when(pl.program_id(1) == 0)
    def _():
        for r in range(qm_ref.shape[0] // tq):
            rows = slice(r * tq, (r + 1) * tq)
            for hh in range(N_MEM_HEADS):
                cols = _head_cols(hh)
                s = lax.dot_general(qm_ref[rows, cols], mk_ref[:, cols], NT_DIMS,
                                    preferred_element_type=F32)
                p = jnp.exp2(s - jnp.max(s, axis=-1, keepdims=True))
                mv = mv_ref[:, cols]
                pv = jnp.dot(p.astype(BF16),
                             jnp.concatenate([mv, jnp.ones(mv.shape, mv.dtype)], axis=1),
                             preferred_element_type=F32)
                ymem_ref[rows, cols] = (pv[:, :HEAD_DIM] / pv[:, HEAD_DIM:]).astype(ymem_ref.dtype)
        project()

    @pl.when(pl.program_id(1) > 0)
    def _():
        project()


def attn_out_block(h, y_self, proj, mkv, w, layer, *, seq, n_mem, q_col, tn, tq):
    m, n = h.shape
    k1 = y_self.shape[1]
    assert k1 % MEM_WIDTH == 0
    q_blk = q_col // MEM_WIDTH
    return pl.pallas_call(
        functools.partial(_attn_out_kernel, tq=tq),
        grid=(m // seq, n // tn),
        in_specs=[
            pl.BlockSpec((seq, tn), lambda i, j: (i, j)),
            pl.BlockSpec((seq, k1), lambda i, j: (i, 0)),
            pl.BlockSpec((seq, MEM_WIDTH), lambda i, j: (i, q_blk)),
            pl.BlockSpec((None, n_mem, MEM_WIDTH), lambda i, j: (layer, i, 0)),
            pl.BlockSpec((None, n_mem, MEM_WIDTH), lambda i, j: (layer, i, 1)),
            pl.BlockSpec((None, k1, tn), lambda i, j: (layer, 0, j)),
            pl.BlockSpec((None, MEM_WIDTH, tn), lambda i, j: (layer, k1 // MEM_WIDTH, j)),
        ],
        out_specs=pl.BlockSpec((seq, tn), lambda i, j: (i, j)),
        out_shape=jax.ShapeDtypeStruct((m, n), F32),
        scratch_shapes=[pltpu.VMEM((seq, MEM_WIDTH), BF16)],
        compiler_params=_params(2),
        name="attn_out_block",
    )(h, y_self, proj, mkv, mkv, w, w)


def _rel_bucket_np(rel):
    n = np.maximum(rel, 0)
    max_exact = N_BUCKETS // 2
    nf = np.maximum(n, 1).astype(np.float32)
    large = max_exact + (np.log(nf / np.float32(max_exact)) / np.float32(math.log(MAX_DISTANCE / max_exact))
                         * np.float32(N_BUCKETS - max_exact)).astype(np.int32)
    large = np.minimum(large, N_BUCKETS - 1)
    return np.where(n < max_exact, n, large).astype(np.int32)


def _bucket_tiles():
    qi = np.arange(BLK, dtype=np.int32)[:, None]
    ki = np.arange(BLK, dtype=np.int32)[None, :]
    diag = _rel_bucket_np(qi - ki)
    prev = _rel_bucket_np(qi - ki + BLK)
    assert _rel_bucket_np(np.array([BLK + 1]))[0] == N_BUCKETS - 1
    return jnp.asarray(diag), jnp.asarray(prev)


def _bias_tile(bkt, tab_ref, head):
    out = jnp.zeros(bkt.shape, F32)
    for b in range(N_BUCKETS):
        out = jnp.where(bkt == b, tab_ref[head, b] * LOG2E, out)
    return out


def _causal_tile(tile):
    qi = lax.broadcasted_iota(jnp.int32, tile.shape, 0)
    ki = lax.broadcasted_iota(jnp.int32, tile.shape, 1)
    return jnp.where(ki <= qi, tile, NEG)


def _bias_kernel(tab_ref, bd_ref, bp_ref, o_ref):
    h = pl.program_id(0)
    o_ref[...] = jnp.concatenate(
        [_bias_tile(bp_ref[...], tab_ref, h),
         _causal_tile(_bias_tile(bd_ref[...], tab_ref, h))], axis=1)


def bias_tiles(tab, bkt_diag, bkt_prev):
    n_heads = tab.shape[0]
    return pl.pallas_call(
        _bias_kernel,
        grid=(n_heads,),
        in_specs=[
            pl.BlockSpec(memory_space=pltpu.SMEM),
            pl.BlockSpec((BLK, BLK), lambda h: (0, 0)),
            pl.BlockSpec((BLK, BLK), lambda h: (0, 0)),
        ],
        out_specs=pl.BlockSpec((None, BLK, 2 * BLK), lambda h: (h, 0, 0)),
        out_shape=jax.ShapeDtypeStruct((n_heads, BLK, 2 * BLK), F32),
        compiler_params=_params(1),
        name="bias_tiles",
    )(tab, bkt_diag, bkt_prev)


def _wide(x, width):
    return jnp.tile(x, (1, width // HEAD_DIM))


def _softmax_step(s, v, m_ref, acc_ref, *, first, shift=None):
    v_aug = jnp.concatenate([v, jnp.ones(v.shape, v.dtype)], axis=1)
    m_cur = jnp.max(s, axis=-1, keepdims=True)
    if shift is not None:
        m_cur = m_cur + shift
    if first:
        m_new = jnp.broadcast_to(m_cur, m_ref.shape)
    else:
        m_old = m_ref[...]
        m_new = jnp.maximum(m_old, m_cur)
    p = jnp.exp2(s - _wide(m_new if shift is None else m_new - shift, s.shape[1]))
    pv = jnp.dot(p.astype(BF16), v_aug, preferred_element_type=F32)
    if first:
        acc_ref[...] = pv
    else:
        acc_ref[...] = _wide(jnp.exp2(m_old - m_new), acc_ref.shape[1]) * acc_ref[...] + pv
    m_ref[...] = m_new


def _softmax_result(acc_ref):
    acc = acc_ref[...]
    return acc[:, :HEAD_DIM] / acc[:, HEAD_DIM:]


def _head_cols(hh):
    return slice(hh * HEAD_DIM, (hh + 1) * HEAD_DIM)


def _moba_kernel(tab_ref, bias_ref, q_ref, k_ref, v_ref, o_ref,
                 kmean_ref, qa_ref, m_ref, acc_ref, *, gate_fill, heads):
    hg, qb = pl.program_id(0), pl.program_id(2)
    group, seq, _ = k_ref.shape
    nb = seq // BLK
    chains = [(hh, g) for hh in range(heads) for g in range(group)]

    @pl.when(qb == 0)
    def _():
        for c, (hh, g) in enumerate(chains):
            kf = k_ref[g, :, _head_cols(hh)].astype(F32).reshape(nb, BLK, HEAD_DIM)
            km = jnp.mean(kf, axis=1)
            km_hi = km.astype(BF16)
            km_lo = (km - km_hi.astype(F32)).astype(BF16)
            kmean_ref[c] = jnp.concatenate([km_hi, km_lo], axis=0)

    row = lax.broadcasted_iota(jnp.int32, (nb, BLK), 0)
    past = row < qb
    n_sel = max(1, min(MOBA_TOPK, nb - 1))
    for c, (hh, g) in enumerate(chains):
        q = q_ref[g, :, _head_cols(hh)]
        parts = lax.dot_general(kmean_ref[c], q, NT_DIMS, preferred_element_type=F32)
        gate = jnp.where(past, parts[:nb] + parts[nb:], gate_fill)
        beaten_by = jnp.zeros(gate.shape, F32)
        for i in range(nb):
            gi = gate[i:i + 1, :]
            wins = (gi > gate) | ((gi == gate) & (i < row))
            beaten_by = beaten_by + jnp.where(wins, 1.0, 0.0)
        keep = ((beaten_by < n_sel) & past) | (row == qb)
        sel_neg = jnp.where(keep, 0.0, NEG)
        sel_neg = jnp.concatenate([sel_neg, jnp.zeros((HEAD_DIM - nb, BLK), F32)], axis=0)
        qa_ref[c] = jnp.concatenate([q, sel_neg.T.astype(BF16)], axis=1)


    def span_scores(c, j, n):
        hh, g = chains[c]
        rows = pl.ds(pl.multiple_of(j * BLK, BLK), n * BLK)
        kj = k_ref[g, rows, _head_cols(hh)]
        lane = lax.broadcasted_iota(jnp.int32, (n * BLK, HEAD_DIM), 1)
        blk = j + lax.broadcasted_iota(jnp.int32, (n * BLK, HEAD_DIM), 0) // BLK
        onehot = jnp.where(lane == blk, 1.0, 0.0).astype(BF16)
        k_aug = jnp.concatenate([kj, onehot], axis=1)
        s = lax.dot_general(qa_ref[c], k_aug, NT_DIMS, preferred_element_type=F32)
        return s, v_ref[g, rows, _head_cols(hh)]

    @pl.when(qb == 0)
    def _():
        for c, (hh, g) in enumerate(chains):
            s, v = span_scores(c, qb, 1)
            _softmax_step(s + bias_ref[hh, :, BLK:], v, m_ref.at[c], acc_ref.at[c], first=True)

    @pl.when(qb >= 1)
    def _():
        for c, (hh, g) in enumerate(chains):
            s, v = span_scores(c, qb - 1, 2)
            _softmax_step(s + bias_ref[hh], v, m_ref.at[c], acc_ref.at[c], first=True)

    far_bias = [tab_ref[hg * heads + hh, N_BUCKETS - 1] * LOG2E for hh in range(heads)]

    def far_pair(p, carry):
        for c, (hh, g) in enumerate(chains):
            s, v = span_scores(c, 2 * p, 2)
            _softmax_step(s, v, m_ref.at[c], acc_ref.at[c], first=False, shift=far_bias[hh])
        return carry

    lax.fori_loop(0, jnp.maximum(qb - 1, 0) // 2, far_pair, 0)

    @pl.when((qb >= 2) & (qb % 2 == 0))
    def _():
        for c, (hh, g) in enumerate(chains):
            s, v = span_scores(c, qb - 2, 1)
            _softmax_step(s, v, m_ref.at[c], acc_ref.at[c], first=False, shift=far_bias[hh])

    for c, (hh, g) in enumerate(chains):
        o_ref[g, :, _head_cols(hh)] = _softmax_result(acc_ref.at[c]).astype(o_ref.dtype)


def moba_attention(proj, tab, bias, *, group, heads, gate_fill):
    batch, seq, _ = proj.shape
    nqb = seq // BLK
    k_off = N_SELF_HEADS // heads
    width = heads * HEAD_DIM
    chains = heads * group
    kern = functools.partial(_moba_kernel, gate_fill=gate_fill, heads=heads)
    return pl.pallas_call(
        kern,
        grid=(N_SELF_HEADS // heads, batch // group, nqb),
        in_specs=[
            pl.BlockSpec(memory_space=pltpu.SMEM),
            pl.BlockSpec((heads, BLK, 2 * BLK), lambda h, b, qb: (h, 0, 0)),
            pl.BlockSpec((group, BLK, width), lambda h, b, qb: (b, qb, h)),
            pl.BlockSpec((group, seq, width), lambda h, b, qb: (b, 0, k_off + h)),
            pl.BlockSpec((group, seq, width), lambda h, b, qb: (b, 0, 2 * k_off + h)),
        ],
        out_specs=pl.BlockSpec((group, BLK, width), lambda h, b, qb: (b, qb, h)),
        out_shape=jax.ShapeDtypeStruct((batch, seq, SELF_WIDTH), BF16),
        scratch_shapes=[
            pltpu.VMEM((chains, 2 * (seq // BLK), HEAD_DIM), BF16),
            pltpu.VMEM((chains, BLK, 2 * HEAD_DIM), BF16),
            pltpu.VMEM((chains, BLK, HEAD_DIM), F32), pltpu.VMEM((chains, BLK, 2 * HEAD_DIM), F32),
        ],
        compiler_params=_params(3),
        name="moba_attention",
    )(tab, bias, proj, proj, proj)


def _twice(tile):
    return jnp.concatenate([tile, tile], axis=0)


def _diff_kernel(tab_ref, bias_ref, lq_ref, gs_ref, q_ref, k_ref, v_ref, o_ref,
                 q2_ref, m_ref, acc_ref, *, lam_init, heads):
    hg, qb = pl.program_id(0), pl.program_id(2)
    group = k_ref.shape[0]
    chains = [(hh, g) for hh in range(heads) for g in range(group)]

    lane = lax.broadcasted_iota(jnp.int32, (BLK, HEAD_DIM), 1)
    for c, (hh, g) in enumerate(chains):
        q = q_ref[g, :, _head_cols(hh)]
        zero = jnp.zeros_like(q)
        q2_ref[c] = jnp.concatenate([jnp.where(lane < DIFF_HALF, q, zero),
                                     jnp.where(lane >= DIFF_HALF, q, zero)], axis=0)

    def span_scores(c, j, n):
        hh, g = chains[c]
        rows = pl.ds(pl.multiple_of(j * BLK, BLK), n * BLK)
        s = lax.dot_general(q2_ref[c], k_ref[g, rows, _head_cols(hh)], NT_DIMS,
                            preferred_element_type=F32)
        return s, v_ref[g, rows, _head_cols(hh)]

    @pl.when(qb == 0)
    def _():
        for c, (hh, g) in enumerate(chains):
            s, v = span_scores(c, qb, 1)
            _softmax_step(s + _twice(bias_ref[hh, :, BLK:]), v, m_ref.at[c], acc_ref.at[c],
                          first=True)

    @pl.when(qb >= 1)
    def _():
        for c, (hh, g) in enumerate(chains):
            s, v = span_scores(c, qb - 1, 2)
            _softmax_step(s + _twice(bias_ref[hh]), v, m_ref.at[c], acc_ref.at[c], first=True)

    far_bias = [tab_ref[hg * heads + hh, N_BUCKETS - 1] * LOG2E for hh in range(heads)]

    def far_pair(p, carry):
        for c, (hh, g) in enumerate(chains):
            s, v = span_scores(c, 2 * p, 2)
            _softmax_step(s, v, m_ref.at[c], acc_ref.at[c], first=False, shift=far_bias[hh])
        return carry

    lax.fori_loop(0, jnp.maximum(qb - 1, 0) // 2, far_pair, 0)

    @pl.when((qb >= 2) & (qb % 2 == 0))
    def _():
        for c, (hh, g) in enumerate(chains):
            s, v = span_scores(c, qb - 2, 1)
            _softmax_step(s, v, m_ref.at[c], acc_ref.at[c], first=False, shift=far_bias[hh])

    lq = lq_ref[...]
    lam = (jnp.exp(jnp.sum(lq[0:1] * lq[1:2], axis=-1, keepdims=True))
           - jnp.exp(jnp.sum(lq[2:3] * lq[3:4], axis=-1, keepdims=True)) + lam_init)
    for c, (hh, g) in enumerate(chains):
        o_all = _softmax_result(acc_ref.at[c])
        o = o_all[:BLK] - lam * o_all[BLK:]
        o_ref[g, :, _head_cols(hh)] = (
            _rmsnorm_rows(o, gs_ref[...]) * (1.0 - lam_init)).astype(o_ref.dtype)


def diff_attention(proj, kv, tab, bias, lq, g_subln, *, group, heads, lam_init):
    batch, seq, _ = proj.shape
    nqb = seq // BLK
    v_off = N_SELF_HEADS // heads
    width = heads * HEAD_DIM
    chains = heads * group
    kern = functools.partial(_diff_kernel, lam_init=lam_init, heads=heads)
    return pl.pallas_call(
        kern,
        grid=(N_SELF_HEADS // heads, batch // group, nqb),
        in_specs=[
            pl.BlockSpec(memory_space=pltpu.SMEM),
            pl.BlockSpec((heads, BLK, 2 * BLK), lambda h, b, qb: (h, 0, 0)),
            pl.BlockSpec((4, DIFF_HALF), lambda h, b, qb: (0, 0)),
            pl.BlockSpec((1, HEAD_DIM), lambda h, b, qb: (0, 0)),
            pl.BlockSpec((group, BLK, width), lambda h, b, qb: (b, qb, h)),
            pl.BlockSpec((group, seq, width), lambda h, b, qb: (b, 0, h)),
            pl.BlockSpec((group, seq, width), lambda h, b, qb: (b, 0, v_off + h)),
        ],
        out_specs=pl.BlockSpec((group, BLK, width), lambda h, b, qb: (b, qb, h)),
        out_shape=jax.ShapeDtypeStruct((batch, seq, SELF_WIDTH), BF16),
        scratch_shapes=[
            pltpu.VMEM((chains, 2 * BLK, HEAD_DIM), BF16),
            pltpu.VMEM((chains, 2 * BLK, HEAD_DIM), F32),
            pltpu.VMEM((chains, 2 * BLK, 2 * HEAD_DIM), F32),
        ],
        compiler_params=_params(3),
        name="diff_attention",
    )(tab, bias, lq, g_subln.reshape(1, HEAD_DIM), proj, kv, kv)


def _mem_kv_kernel(x_ref, g_ref, w_ref, o_ref, xn_ref):
    @pl.when(pl.program_id(1) == 0)
    def _():
        xn_ref[...] = _rmsnorm_rows(x_ref[...], g_ref[...]).astype(BF16)

    o_ref[...] = jnp.dot(xn_ref[...], w_ref[...].astype(BF16),
                         preferred_element_type=F32).astype(o_ref.dtype)


def mem_kv(mem, g_mem, w_mem_kv, *, tn):
    m, d = mem.shape
    depth, _, n = w_mem_kv.shape
    return pl.pallas_call(
        _mem_kv_kernel,
        grid=(depth, n // tn),
        in_specs=[
            pl.BlockSpec((m, d), lambda l, j: (0, 0)),
            pl.BlockSpec((None, 1, d), lambda l, j: (l, 0, 0)),
            pl.BlockSpec((None, d, tn), lambda l, j: (l, 0, j)),
        ],
        out_specs=pl.BlockSpec((None, m, tn), lambda l, j: (l, 0, j)),
        out_shape=jax.ShapeDtypeStruct((depth, m, n), BF16),
        scratch_shapes=[pltpu.VMEM((m, d), BF16)],
        compiler_params=_params(2),
        name="mem_kv",
    )(mem, g_mem.reshape(depth, 1, d), w_mem_kv)


ATTN_GROUP = 4
ATTN_HEADS = 4


def kernel(x, mem, rel_bias, g_mix, w_in_a, w_in_b, g_mem, w_mem_kv, w_o, g_ffn, w_gate_up, w_down,
           g_kv, w_kv_shared, lambda_qk, g_subln, g_final):
    batch, seq, d = x.shape
    n_mem = mem.shape[1]
    depth = g_mix.shape[0]
    n_a = w_in_a.shape[0]
    tokens = batch * seq
    group = math.gcd(batch, ATTN_GROUP)

    w_kv_shared = w_kv_shared[None]

    tab = rel_bias.T.astype(F32)
    bias = bias_tiles(tab, *_bucket_tiles())

    moba_scale = HEAD_DIM ** -0.5 * LOG2E
    diff_scale = DIFF_HALF ** -0.5 * LOG2E
    mem_scale = HEAD_DIM ** -0.5 * LOG2E
    ones = functools.partial(jnp.ones, dtype=F32)
    cs_a = jnp.concatenate([jnp.full((SELF_WIDTH,), moba_scale, F32), ones((2 * SELF_WIDTH,)),
                            jnp.full((MEM_WIDTH,), mem_scale, F32)])
    cs_b = jnp.concatenate([jnp.full((SELF_WIDTH,), diff_scale, F32),
                            jnp.full((MEM_WIDTH,), mem_scale, F32)])

    h = x.reshape(tokens, d)
    mem2 = mem.reshape(batch * n_mem, d)
    mkv = mem_kv(mem2, g_mem, w_mem_kv, tn=512)
    kv = None
    for l in range(depth):
        if l < n_a:
            proj = norm_matmul(h, g_mix[l], w_in_a, l, cs_a, tm=1024, tn=1024)
            y_self = moba_attention(proj.reshape(batch, seq, -1), tab, bias,
                                    group=group, heads=ATTN_HEADS, gate_fill=NEG * moba_scale)
            q_col = 3 * SELF_WIDTH
        else:
            j = l - n_a
            if kv is None:
                kv = norm_matmul(h, g_kv, w_kv_shared, 0, ones((2 * SELF_WIDTH,)), tm=1024, tn=1024)
                kv = kv.reshape(batch, seq, -1)
            proj = norm_matmul(h, g_mix[l], w_in_b, j, cs_b, tm=1024, tn=1024)
            lam_init = 0.8 - 0.6 * math.exp(-0.3 * l)
            y_self = diff_attention(proj.reshape(batch, seq, -1), kv, tab, bias,
                                    lambda_qk[j], g_subln[j], group=group, heads=ATTN_HEADS,
                                    lam_init=lam_init)
            q_col = SELF_WIDTH
        h = attn_out_block(h, y_self.reshape(tokens, -1), proj, mkv, w_o, l, seq=seq, n_mem=n_mem,
                           q_col=q_col, tn=512, tq=1024)
        h = ffn_block(h, g_ffn[l], w_gate_up, w_down, l, tm=1024, tc=256,
                      final_gain=g_final if l == depth - 1 else None)
    return h.reshape(batch, seq, d)
```

```python
import functools
import math

import jax
import jax.numpy as jnp
import numpy as np
from jax import lax
from jax.experimental import pallas as pl
from jax.experimental.pallas import tpu as pltpu

D_MODEL = 2048
HEAD_DIM = 128
N_MEM_HEADS = 4
MEM_WIDTH = N_MEM_HEADS * HEAD_DIM
SELF_WIDTH = D_MODEL - MEM_WIDTH
N_SELF_HEADS = SELF_WIDTH // HEAD_DIM
DIFF_HALF = HEAD_DIM // 2
MOBA_BLOCK = 256
MOBA_TOPK = 3
N_BUCKETS = 32
MAX_DISTANCE = 128
RMS_EPS = 1e-6
NEG = -1e30
LOG2E = math.log2(math.e)

BLK = MOBA_BLOCK
V7X_VMEM_BYTES = 64 * 1024 * 1024
VMEM_LIMIT = V7X_VMEM_BYTES - 4 * 1024 * 1024

F32 = jnp.float32
BF16 = jnp.bfloat16
NT_DIMS = (((1,), (1,)), ((), ()))


def _params(n_grid_axes):
    return pltpu.CompilerParams(
        dimension_semantics=("arbitrary",) * n_grid_axes, vmem_limit_bytes=VMEM_LIMIT)


def _rmsnorm_rows(x, g):
    ms = jnp.mean(x * x, axis=-1, keepdims=True)
    return x * lax.rsqrt(ms + RMS_EPS) * g


ROW_PARTS = 4


def _staggered_row_specs(tm, d, n_tiles, n_steps):
    part = tm // ROW_PARTS
    last = n_tiles * ROW_PARTS - 1
    specs = []
    for q in range(ROW_PARTS):
        s_q = 1 + q * n_steps // ROW_PARTS

        def index(i, c, q=q, s_q=s_q):
            ahead = jnp.where(c >= s_q, 1, 0) if s_q < n_steps else 0
            return (jnp.minimum(ROW_PARTS * (i + ahead) + q, last), 0)

        specs.append(pl.BlockSpec((part, d), index))
    return specs


def _row_part(ref, q):
    part = ref.shape[0] // ROW_PARTS
    return slice(q * part, (q + 1) * part)


def _norm_matmul_kernel(*refs):
    x_parts = refs[:ROW_PARTS]
    g_ref, w_ref, cs_ref, o_ref, xn_ref = refs[ROW_PARTS:]

    def project():
        acc = jnp.dot(xn_ref[...], w_ref[...].astype(BF16), preferred_element_type=F32)
        o_ref[...] = (acc * cs_ref[...]).astype(o_ref.dtype)

    @pl.when(pl.program_id(1) == 0)
    def _():
        for q, x_ref in enumerate(x_parts):
            xn_ref[_row_part(xn_ref, q), :] = _rmsnorm_rows(x_ref[...], g_ref[...]).astype(BF16)
        project()

    @pl.when(pl.program_id(1) > 0)
    def _():
        project()


def norm_matmul(x, g, w, layer, col_scale, *, tm, tn):
    m, d = x.shape
    n = w.shape[-1]
    return pl.pallas_call(
        _norm_matmul_kernel,
        grid=(m // tm, n // tn),
        in_specs=_staggered_row_specs(tm, d, m // tm, n // tn) + [
            pl.BlockSpec((1, d), lambda i, j: (0, 0)),
            pl.BlockSpec((None, d, tn), lambda i, j: (layer, 0, j)),
            pl.BlockSpec((1, tn), lambda i, j: (0, j)),
        ],
        out_specs=pl.BlockSpec((tm, tn), lambda i, j: (i, j)),
        out_shape=jax.ShapeDtypeStruct((m, n), BF16),
        scratch_shapes=[pltpu.VMEM((tm, d), BF16)],
        compiler_params=_params(2),
        name="norm_matmul",
    )(*([x] * ROW_PARTS), g.reshape(1, d), w, col_scale.reshape(1, n))


def _ffn_kernel(*refs, n_slab, final_norm):
    h_parts = refs[:ROW_PARTS]
    g_ref, gf_ref, wg_ref, wu_ref, wd_ref, o_ref, xn_ref = refs[ROW_PARTS:]
    c = pl.program_id(1)

    def add_chunk():
        xn = xn_ref[...]
        gate = jnp.dot(xn, wg_ref[...].astype(BF16), preferred_element_type=F32)
        up = jnp.dot(xn, wu_ref[...].astype(BF16), preferred_element_type=F32)
        act = (gate * jax.nn.sigmoid(gate) * up).astype(BF16)
        slab = o_ref.shape[1] // n_slab
        for n in range(n_slab):
            cols = slice(n * slab, (n + 1) * slab)
            o_ref[:, cols] += jnp.dot(act, wd_ref[:, cols].astype(BF16), preferred_element_type=F32)

    @pl.when(c == 0)
    def _():
        for q, h_ref in enumerate(h_parts):
            h = h_ref[...]
            xn_ref[_row_part(xn_ref, q), :] = _rmsnorm_rows(h, g_ref[...]).astype(BF16)
            o_ref[_row_part(o_ref, q), :] = h
        add_chunk()

    @pl.when(c > 0)
    def _():
        add_chunk()

    if final_norm:
        @pl.when(c == pl.num_programs(1) - 1)
        def _():
            o_ref[...] = _rmsnorm_rows(o_ref[...], gf_ref[...])


def ffn_block(h, g, w_gate_up, w_down, layer, *, tm, tc, final_gain=None):
    m, d = h.shape
    d_ff = w_down.shape[1]
    nc = d_ff // tc
    final_norm = final_gain is not None
    gf = final_gain if final_norm else g
    return pl.pallas_call(
        functools.partial(_ffn_kernel, n_slab=4, final_norm=final_norm),
        grid=(m // tm, nc),
        in_specs=_staggered_row_specs(tm, d, m // tm, nc) + [
            pl.BlockSpec((1, d), lambda i, c: (0, 0)),
            pl.BlockSpec((1, d), lambda i, c: (0, 0)),
            pl.BlockSpec((None, d, tc), lambda i, c: (layer, 0, c)),
            pl.BlockSpec((None, d, tc), lambda i, c: (layer, 0, c + nc)),
            pl.BlockSpec((None, tc, d), lambda i, c: (layer, c, 0)),
        ],
        out_specs=pl.BlockSpec((tm, d), lambda i, c: (i, 0)),
        out_shape=jax.ShapeDtypeStruct((m, d), F32),
        scratch_shapes=[pltpu.VMEM((tm, d), BF16)],
        compiler_params=_params(2),
        name="ffn_block",
    )(*([h] * ROW_PARTS), g.reshape(1, d), gf.reshape(1, d), w_gate_up, w_gate_up, w_down)


def _attn_out_kernel(h_ref, a1_ref, gs_ref, qm_ref, mk_ref, mv_ref, w1_ref, w2_ref, o_ref,
                     ymem_ref, yn_ref, *, tq, subln_scale):
    y_ref = a1_ref if subln_scale is None else yn_ref

    def project():
        acc = jnp.dot(y_ref[...], w1_ref[...].astype(BF16), preferred_element_type=F32)
        acc = acc + jnp.dot(ymem_ref[...], w2_ref[...].astype(BF16), preferred_element_type=F32)
        o_ref[...] = h_ref[...] + acc

    @pl.when(pl.program_id(1) == 0)
    def _():
        if subln_scale is not None:
            gain = gs_ref[...] * subln_scale
            for hh in range(a1_ref.shape[1] // HEAD_DIM):
                cols = _head_cols(hh)
                yn_ref[:, cols] = _rmsnorm_rows(a1_ref[:, cols].astype(F32), gain).astype(BF16)
        for r in range(qm_ref.shape[0] // tq):
            rows = slice(r * tq, (r + 1) * tq)
            for hh in range(N_MEM_HEADS):
                cols = _head_cols(hh)
                s = lax.dot_general(qm_ref[rows, cols], mk_ref[:, cols], NT_DIMS,
                                    preferred_element_type=F32)
                p = jnp.exp2(s - jnp.max(s, axis=-1, keepdims=True))
                mv = mv_ref[:, cols]
                pv = jnp.dot(p.astype(BF16),
                             jnp.concatenate([mv, jnp.ones(mv.shape, mv.dtype)], axis=1),
                             preferred_element_type=F32)
                ymem_ref[rows, cols] = (pv[:, :HEAD_DIM] / pv[:, HEAD_DIM:]).astype(ymem_ref.dtype)
        project()

    @pl.when(pl.program_id(1) > 0)
    def _():
        project()


def attn_out_block(h, y_self, proj, mkv, w, layer, *, seq, n_mem, q_col, tn, tq,
                   subln_gain=None, subln_scale=None):
    m, n = h.shape
    k1 = y_self.shape[1]
    assert k1 % MEM_WIDTH == 0
    q_blk = q_col // MEM_WIDTH
    if subln_gain is None:
        subln_gain = jnp.ones((HEAD_DIM,), F32)
    yn_rows = seq if subln_scale is not None else 8
    return pl.pallas_call(
        functools.partial(_attn_out_kernel, tq=tq, subln_scale=subln_scale),
        grid=(m // seq, n // tn),
        in_specs=[
            pl.BlockSpec((seq, tn), lambda i, j: (i, j)),
            pl.BlockSpec((seq, k1), lambda i, j: (i, 0)),
            pl.BlockSpec((1, HEAD_DIM), lambda i, j: (0, 0)),
            pl.BlockSpec((seq, MEM_WIDTH), lambda i, j: (i, q_blk)),
            pl.BlockSpec((None, n_mem, MEM_WIDTH), lambda i, j: (layer, i, 0)),
            pl.BlockSpec((None, n_mem, MEM_WIDTH), lambda i, j: (layer, i, 1)),
            pl.BlockSpec((None, k1, tn), lambda i, j: (layer, 0, j)),
            pl.BlockSpec((None, MEM_WIDTH, tn), lambda i, j: (layer, k1 // MEM_WIDTH, j)),
        ],
        out_specs=pl.BlockSpec((seq, tn), lambda i, j: (i, j)),
        out_shape=jax.ShapeDtypeStruct((m, n), F32),
        scratch_shapes=[pltpu.VMEM((seq, MEM_WIDTH), BF16), pltpu.VMEM((yn_rows, k1), BF16)],
        compiler_params=_params(2),
        name="attn_out_block",
    )(h, y_self, subln_gain.reshape(1, HEAD_DIM), proj, mkv, mkv, w, w)


def _rel_bucket_np(rel):
    n = np.maximum(rel, 0)
    max_exact = N_BUCKETS // 2
    nf = np.maximum(n, 1).astype(np.float32)
    large = max_exact + (np.log(nf / np.float32(max_exact)) / np.float32(math.log(MAX_DISTANCE / max_exact))
                         * np.float32(N_BUCKETS - max_exact)).astype(np.int32)
    large = np.minimum(large, N_BUCKETS - 1)
    return np.where(n < max_exact, n, large).astype(np.int32)


def _bucket_tiles():
    qi = np.arange(BLK, dtype=np.int32)[:, None]
    ki = np.arange(BLK, dtype=np.int32)[None, :]
    diag = _rel_bucket_np(qi - ki)
    prev = _rel_bucket_np(qi - ki + BLK)
    assert _rel_bucket_np(np.array([BLK + 1]))[0] == N_BUCKETS - 1
    return jnp.asarray(diag), jnp.asarray(prev)


def _bias_tile(bkt, tab_ref, head):
    out = jnp.zeros(bkt.shape, F32)
    for b in range(N_BUCKETS):
        out = jnp.where(bkt == b, tab_ref[head, b] * LOG2E, out)
    return out


def _causal_tile(tile):
    qi = lax.broadcasted_iota(jnp.int32, tile.shape, 0)
    ki = lax.broadcasted_iota(jnp.int32, tile.shape, 1)
    return jnp.where(ki <= qi, tile, NEG)


def _bias_kernel(tab_ref, bd_ref, bp_ref, o_ref):
    h = pl.program_id(0)
    o_ref[...] = jnp.concatenate(
        [_bias_tile(bp_ref[...], tab_ref, h),
         _causal_tile(_bias_tile(bd_ref[...], tab_ref, h))], axis=1)


def bias_tiles(tab, bkt_diag, bkt_prev):
    n_heads = tab.shape[0]
    return pl.pallas_call(
        _bias_kernel,
        grid=(n_heads,),
        in_specs=[
            pl.BlockSpec(memory_space=pltpu.SMEM),
            pl.BlockSpec((BLK, BLK), lambda h: (0, 0)),
            pl.BlockSpec((BLK, BLK), lambda h: (0, 0)),
        ],
        out_specs=pl.BlockSpec((None, BLK, 2 * BLK), lambda h: (h, 0, 0)),
        out_shape=jax.ShapeDtypeStruct((n_heads, BLK, 2 * BLK), F32),
        compiler_params=_params(1),
        name="bias_tiles",
    )(tab, bkt_diag, bkt_prev)


def _wide(x, width):
    return jnp.tile(x, (1, width // HEAD_DIM))


def _softmax_step(s, v, m_ref, acc_ref, *, first, shift=None):
    v_aug = jnp.concatenate([v, jnp.ones(v.shape, v.dtype)], axis=1)
    m_cur = jnp.max(s, axis=-1, keepdims=True)
    if shift is not None:
        m_cur = m_cur + shift
    if first:
        m_new = jnp.broadcast_to(m_cur, m_ref.shape)
    else:
        m_old = m_ref[...]
        m_new = jnp.maximum(m_old, m_cur)
    p = jnp.exp2(s - _wide(m_new if shift is None else m_new - shift, s.shape[1]))
    pv = jnp.dot(p.astype(BF16), v_aug, preferred_element_type=F32)
    if first:
        acc_ref[...] = pv
    else:
        acc_ref[...] = _wide(jnp.exp2(m_old - m_new), acc_ref.shape[1]) * acc_ref[...] + pv
    m_ref[...] = m_new


def _softmax_result(acc_ref):
    acc = acc_ref[...]
    return acc[:, :HEAD_DIM] / acc[:, HEAD_DIM:]


def _head_cols(hh):
    return slice(hh * HEAD_DIM, (hh + 1) * HEAD_DIM)


def _moba_kernel(tab_ref, bias_ref, q_ref, k_ref, v_ref, o_ref,
                 kmean_ref, qa_ref, m_ref, acc_ref, *, gate_fill, heads):
    hg, qb = pl.program_id(0), pl.program_id(2)
    group, seq, _ = k_ref.shape
    nb = seq // BLK
    chains = [(hh, g) for hh in range(heads) for g in range(group)]

    @pl.when(qb == 0)
    def _():
        for c, (hh, g) in enumerate(chains):
            kf = k_ref[g, :, _head_cols(hh)].astype(F32).reshape(nb, BLK, HEAD_DIM)
            km = jnp.mean(kf, axis=1)
            km_hi = km.astype(BF16)
            km_lo = (km - km_hi.astype(F32)).astype(BF16)
            kmean_ref[c] = jnp.concatenate([km_hi, km_lo], axis=0)

    row = lax.broadcasted_iota(jnp.int32, (nb, BLK), 0)
    past = row < qb
    n_sel = max(1, min(MOBA_TOPK, nb - 1))
    for c, (hh, g) in enumerate(chains):
        q = q_ref[g, :, _head_cols(hh)]
        parts = lax.dot_general(kmean_ref[c], q, NT_DIMS, preferred_element_type=F32)
        gate = jnp.where(past, parts[:nb] + parts[nb:], gate_fill)
        beaten_by = jnp.zeros(gate.shape, F32)
        for i in range(nb):
            gi = gate[i:i + 1, :]
            wins = (gi > gate) | ((gi == gate) & (i < row))
            beaten_by = beaten_by + jnp.where(wins, 1.0, 0.0)
        keep = ((beaten_by < n_sel) & past) | (row == qb)
        sel_neg = jnp.where(keep, 0.0, NEG)
        sel_neg = jnp.concatenate([sel_neg, jnp.zeros((HEAD_DIM - nb, BLK), F32)], axis=0)
        qa_ref[c] = jnp.concatenate([q, sel_neg.T.astype(BF16)], axis=1)


    def span_scores(c, j, n):
        hh, g = chains[c]
        rows = pl.ds(pl.multiple_of(j * BLK, BLK), n * BLK)
        kj = k_ref[g, rows, _head_cols(hh)]
        lane = lax.broadcasted_iota(jnp.int32, (n * BLK, HEAD_DIM), 1)
        blk = j + lax.broadcasted_iota(jnp.int32, (n * BLK, HEAD_DIM), 0) // BLK
        onehot = jnp.where(lane == blk, 1.0, 0.0).astype(BF16)
        k_aug = jnp.concatenate([kj, onehot], axis=1)
        s = lax.dot_general(qa_ref[c], k_aug, NT_DIMS, preferred_element_type=F32)
        return s, v_ref[g, rows, _head_cols(hh)]

    @pl.when(qb == 0)
    def _():
        for c, (hh, g) in enumerate(chains):
            s, v = span_scores(c, qb, 1)
            _softmax_step(s + bias_ref[hh, :, BLK:], v, m_ref.at[c], acc_ref.at[c], first=True)

    @pl.when(qb >= 1)
    def _():
        for c, (hh, g) in enumerate(chains):
            s, v = span_scores(c, qb - 1, 2)
            _softmax_step(s + bias_ref[hh], v, m_ref.at[c], acc_ref.at[c], first=True)

    far_bias = [tab_ref[hg * heads + hh, N_BUCKETS - 1] * LOG2E for hh in range(heads)]

    def far_pair(p, carry):
        for c, (hh, g) in enumerate(chains):
            s, v = span_scores(c, 2 * p, 2)
            _softmax_step(s, v, m_ref.at[c], acc_ref.at[c], first=False, shift=far_bias[hh])
        return carry

    lax.fori_loop(0, jnp.maximum(qb - 1, 0) // 2, far_pair, 0)

    @pl.when((qb >= 2) & (qb % 2 == 0))
    def _():
        for c, (hh, g) in enumerate(chains):
            s, v = span_scores(c, qb - 2, 1)
            _softmax_step(s, v, m_ref.at[c], acc_ref.at[c], first=False, shift=far_bias[hh])

    for c, (hh, g) in enumerate(chains):
        o_ref[g, :, _head_cols(hh)] = _softmax_result(acc_ref.at[c]).astype(o_ref.dtype)


def moba_attention(proj, tab, bias, *, group, heads, gate_fill):
    batch, seq, _ = proj.shape
    nqb = seq // BLK
    k_off = N_SELF_HEADS // heads
    width = heads * HEAD_DIM
    chains = heads * group
    kern = functools.partial(_moba_kernel, gate_fill=gate_fill, heads=heads)
    return pl.pallas_call(
        kern,
        grid=(N_SELF_HEADS // heads, batch // group, nqb),
        in_specs=[
            pl.BlockSpec(memory_space=pltpu.SMEM),
            pl.BlockSpec((heads, BLK, 2 * BLK), lambda h, b, qb: (h, 0, 0)),
            pl.BlockSpec((group, BLK, width), lambda h, b, qb: (b, qb, h)),
            pl.BlockSpec((group, seq, width), lambda h, b, qb: (b, 0, k_off + h)),
            pl.BlockSpec((group, seq, width), lambda h, b, qb: (b, 0, 2 * k_off + h)),
        ],
        out_specs=pl.BlockSpec((group, BLK, width), lambda h, b, qb: (b, qb, h)),
        out_shape=jax.ShapeDtypeStruct((batch, seq, SELF_WIDTH), BF16),
        scratch_shapes=[
            pltpu.VMEM((chains, 2 * (seq // BLK), HEAD_DIM), BF16),
            pltpu.VMEM((chains, BLK, 2 * HEAD_DIM), BF16),
            pltpu.VMEM((chains, BLK, HEAD_DIM), F32), pltpu.VMEM((chains, BLK, 2 * HEAD_DIM), F32),
        ],
        compiler_params=_params(3),
        name="moba_attention",
    )(tab, bias, proj, proj, proj)


def _twice(tile):
    return jnp.concatenate([tile, tile], axis=0)


def _diff_kernel(tab_ref, bias_ref, lq_ref, q_ref, k_ref, v_ref, o_ref,
                 q2_ref, m_ref, acc_ref, *, lam_init, heads):
    hg, qb = pl.program_id(0), pl.program_id(2)
    group = k_ref.shape[0]
    chains = [(hh, g) for hh in range(heads) for g in range(group)]

    lane = lax.broadcasted_iota(jnp.int32, (BLK, HEAD_DIM), 1)
    for c, (hh, g) in enumerate(chains):
        q = q_ref[g, :, _head_cols(hh)]
        zero = jnp.zeros_like(q)
        q2_ref[c] = jnp.concatenate([jnp.where(lane < DIFF_HALF, q, zero),
                                     jnp.where(lane >= DIFF_HALF, q, zero)], axis=0)

    def span_scores(c, j, n):
        hh, g = chains[c]
        rows = pl.ds(pl.multiple_of(j * BLK, BLK), n * BLK)
        s = lax.dot_general(q2_ref[c], k_ref[g, rows, _head_cols(hh)], NT_DIMS,
                            preferred_element_type=F32)
        return s, v_ref[g, rows, _head_cols(hh)]

    @pl.when(qb == 0)
    def _():
        for c, (hh, g) in enumerate(chains):
            s, v = span_scores(c, qb, 1)
            _softmax_step(s + _twice(bias_ref[hh, :, BLK:]), v, m_ref.at[c], acc_ref.at[c],
                          first=True)

    @pl.when(qb >= 1)
    def _():
        for c, (hh, g) in enumerate(chains):
            s, v = span_scores(c, qb - 1, 2)
            _softmax_step(s + _twice(bias_ref[hh]), v, m_ref.at[c], acc_ref.at[c], first=True)

    far_bias = [tab_ref[hg * heads + hh, N_BUCKETS - 1] * LOG2E for hh in range(heads)]

    def far_pair(p, carry):
        for c, (hh, g) in enumerate(chains):
            s, v = span_scores(c, 2 * p, 2)
            _softmax_step(s, v, m_ref.at[c], acc_ref.at[c], first=False, shift=far_bias[hh])
        return carry

    lax.fori_loop(0, jnp.maximum(qb - 1, 0) // 2, far_pair, 0)

    @pl.when((qb >= 2) & (qb % 2 == 0))
    def _():
        for c, (hh, g) in enumerate(chains):
            s, v = span_scores(c, qb - 2, 1)
            _softmax_step(s, v, m_ref.at[c], acc_ref.at[c], first=False, shift=far_bias[hh])

    lq = lq_ref[...]
    lam = (jnp.exp(jnp.sum(lq[0:1] * lq[1:2], axis=-1, keepdims=True))
           - jnp.exp(jnp.sum(lq[2:3] * lq[3:4], axis=-1, keepdims=True)) + lam_init)
    for c, (hh, g) in enumerate(chains):
        o_all = _softmax_result(acc_ref.at[c])
        o_ref[g, :, _head_cols(hh)] = (o_all[:BLK] - lam * o_all[BLK:]).astype(o_ref.dtype)


def diff_attention(proj, kv, tab, bias, lq, *, group, heads, lam_init):
    batch, seq, _ = proj.shape
    nqb = seq // BLK
    v_off = N_SELF_HEADS // heads
    width = heads * HEAD_DIM
    chains = heads * group
    kern = functools.partial(_diff_kernel, lam_init=lam_init, heads=heads)
    return pl.pallas_call(
        kern,
        grid=(N_SELF_HEADS // heads, batch // group, nqb),
        in_specs=[
            pl.BlockSpec(memory_space=pltpu.SMEM),
            pl.BlockSpec((heads, BLK, 2 * BLK), lambda h, b, qb: (h, 0, 0)),
            pl.BlockSpec((4, DIFF_HALF), lambda h, b, qb: (0, 0)),
            pl.BlockSpec((group, BLK, width), lambda h, b, qb: (b, qb, h)),
            pl.BlockSpec((group, seq, width), lambda h, b, qb: (b, 0, h)),
            pl.BlockSpec((group, seq, width), lambda h, b, qb: (b, 0, v_off + h)),
        ],
        out_specs=pl.BlockSpec((group, BLK, width), lambda h, b, qb: (b, qb, h)),
        out_shape=jax.ShapeDtypeStruct((batch, seq, SELF_WIDTH), BF16),
        scratch_shapes=[
            pltpu.VMEM((chains, 2 * BLK, HEAD_DIM), BF16),
            pltpu.VMEM((chains, 2 * BLK, HEAD_DIM), F32),
            pltpu.VMEM((chains, 2 * BLK, 2 * HEAD_DIM), F32),
        ],
        compiler_params=_params(3),
        name="diff_attention",
    )(tab, bias, lq, proj, kv, kv)


def _mem_kv_kernel(x_ref, g_ref, w_ref, o_ref, xn_ref):
    @pl.when(pl.program_id(1) == 0)
    def _():
        xn_ref[...] = _rmsnorm_rows(x_ref[...], g_ref[...]).astype(BF16)

    o_ref[...] = jnp.dot(xn_ref[...], w_ref[...].astype(BF16),
                         preferred_element_type=F32).astype(o_ref.dtype)


def mem_kv(mem, g_mem, w_mem_kv, *, tn):
    m, d = mem.shape
    depth, _, n = w_mem_kv.shape
    return pl.pallas_call(
        _mem_kv_kernel,
        grid=(depth, n // tn),
        in_specs=[
            pl.BlockSpec((m, d), lambda l, j: (0, 0)),
            pl.BlockSpec((None, 1, d), lambda l, j: (l, 0, 0)),
            pl.BlockSpec((None, d, tn), lambda l, j: (l, 0, j)),
        ],
        out_specs=pl.BlockSpec((None, m, tn), lambda l, j: (l, 0, j)),
        out_shape=jax.ShapeDtypeStruct((depth, m, n), BF16),
        scratch_shapes=[pltpu.VMEM((m, d), BF16)],
        compiler_params=_params(2),
        name="mem_kv",
    )(mem, g_mem.reshape(depth, 1, d), w_mem_kv)


ATTN_GROUP = 4
ATTN_HEADS = 4


def kernel(x, mem, rel_bias, g_mix, w_in_a, w_in_b, g_mem, w_mem_kv, w_o, g_ffn, w_gate_up, w_down,
           g_kv, w_kv_shared, lambda_qk, g_subln, g_final):
    batch, seq, d = x.shape
    n_mem = mem.shape[1]
    depth = g_mix.shape[0]
    n_a = w_in_a.shape[0]
    tokens = batch * seq
    group = math.gcd(batch, ATTN_GROUP)

    w_kv_shared = w_kv_shared[None]

    tab = rel_bias.T.astype(F32)
    bias = bias_tiles(tab, *_bucket_tiles())

    moba_scale = HEAD_DIM ** -0.5 * LOG2E
    diff_scale = DIFF_HALF ** -0.5 * LOG2E
    mem_scale = HEAD_DIM ** -0.5 * LOG2E
    ones = functools.partial(jnp.ones, dtype=F32)
    cs_a = jnp.concatenate([jnp.full((SELF_WIDTH,), moba_scale, F32), ones((2 * SELF_WIDTH,)),
                            jnp.full((MEM_WIDTH,), mem_scale, F32)])
    cs_b = jnp.concatenate([jnp.full((SELF_WIDTH,), diff_scale, F32),
                            jnp.full((MEM_WIDTH,), mem_scale, F32)])

    h = x.reshape(tokens, d)
    mem2 = mem.reshape(batch * n_mem, d)
    mkv = mem_kv(mem2, g_mem, w_mem_kv, tn=512)
    kv = None
    for l in range(depth):
        if l < n_a:
            proj = norm_matmul(h, g_mix[l], w_in_a, l, cs_a, tm=1024, tn=1024)
            y_self = moba_attention(proj.reshape(batch, seq, -1), tab, bias,
                                    group=group, heads=ATTN_HEADS, gate_fill=NEG * moba_scale)
            q_col = 3 * SELF_WIDTH
            subln = {}
        else:
            j = l - n_a
            if kv is None:
                kv = norm_matmul(h, g_kv, w_kv_shared, 0, ones((2 * SELF_WIDTH,)), tm=1024, tn=1024)
                kv = kv.reshape(batch, seq, -1)
            proj = norm_matmul(h, g_mix[l], w_in_b, j, cs_b, tm=1024, tn=1024)
            lam_init = 0.8 - 0.6 * math.exp(-0.3 * l)
            y_self = diff_attention(proj.reshape(batch, seq, -1), kv, tab, bias,
                                    lambda_qk[j], group=group, heads=ATTN_HEADS, lam_init=lam_init)
            q_col = SELF_WIDTH
            subln = dict(subln_gain=g_subln[j], subln_scale=1.0 - lam_init)
        h = attn_out_block(h, y_self.reshape(tokens, -1), proj, mkv, w_o, l, seq=seq, n_mem=n_mem,
                           q_col=q_col, tn=512, tq=1024, **subln)
        h = ffn_block(h, g_ffn[l], w_gate_up, w_down, l, tm=1024, tc=256,
                      final_gain=g_final if l == depth - 1 else None)
    return h.reshape(batch, seq, d)
```

```python
import functools
import math

import jax
import jax.numpy as jnp
import numpy as np
from jax import lax
from jax.experimental import pallas as pl
from jax.experimental.pallas import tpu as pltpu

D_MODEL = 2048
HEAD_DIM = 128
N_MEM_HEADS = 4
MEM_WIDTH = N_MEM_HEADS * HEAD_DIM
SELF_WIDTH = D_MODEL - MEM_WIDTH
N_SELF_HEADS = SELF_WIDTH // HEAD_DIM
DIFF_HALF = HEAD_DIM // 2
MOBA_BLOCK = 256
MOBA_TOPK = 3
N_BUCKETS = 32
MAX_DISTANCE = 128
RMS_EPS = 1e-6
NEG = -1e30
LOG2E = math.log2(math.e)

BLK = MOBA_BLOCK
V7X_VMEM_BYTES = 64 * 1024 * 1024
VMEM_LIMIT = V7X_VMEM_BYTES - 4 * 1024 * 1024

F32 = jnp.float32
BF16 = jnp.bfloat16
NT_DIMS = (((1,), (1,)), ((), ()))


def _params(n_grid_axes):
    return pltpu.CompilerParams(
        dimension_semantics=("arbitrary",) * n_grid_axes, vmem_limit_bytes=VMEM_LIMIT)


def _rmsnorm_rows(x, g):
    ms = jnp.mean(x * x, axis=-1, keepdims=True)
    return x * lax.rsqrt(ms + RMS_EPS) * g


ROW_PARTS = 4


def _staggered_row_specs(tm, d, n_tiles, n_steps):
    part = tm // ROW_PARTS
    last = n_tiles * ROW_PARTS - 1
    specs = []
    for q in range(ROW_PARTS):
        s_q = 1 + q * n_steps // ROW_PARTS

        def index(i, c, q=q, s_q=s_q):
            ahead = jnp.where(c >= s_q, 1, 0) if s_q < n_steps else 0
            return (jnp.minimum(ROW_PARTS * (i + ahead) + q, last), 0)

        specs.append(pl.BlockSpec((part, d), index))
    return specs


def _row_part(ref, q):
    part = ref.shape[0] // ROW_PARTS
    return slice(q * part, (q + 1) * part)


def _norm_matmul_kernel(*refs):
    x_parts = refs[:ROW_PARTS]
    g_ref, w_ref, cs_ref, o_ref, xn_ref = refs[ROW_PARTS:]

    def project():
        acc = jnp.dot(xn_ref[...], w_ref[...].astype(BF16), preferred_element_type=F32)
        o_ref[...] = (acc * cs_ref[...]).astype(o_ref.dtype)

    @pl.when(pl.program_id(1) == 0)
    def _():
        for q, x_ref in enumerate(x_parts):
            xn_ref[_row_part(xn_ref, q), :] = _rmsnorm_rows(x_ref[...], g_ref[...]).astype(BF16)
        project()

    @pl.when(pl.program_id(1) > 0)
    def _():
        project()


def norm_matmul(x, g, w, layer, col_scale, *, tm, tn):
    m, d = x.shape
    n = w.shape[-1]
    return pl.pallas_call(
        _norm_matmul_kernel,
        grid=(m // tm, n // tn),
        in_specs=_staggered_row_specs(tm, d, m // tm, n // tn) + [
            pl.BlockSpec((1, d), lambda i, j: (0, 0)),
            pl.BlockSpec((None, d, tn), lambda i, j: (layer, 0, j)),
            pl.BlockSpec((1, tn), lambda i, j: (0, j)),
        ],
        out_specs=pl.BlockSpec((tm, tn), lambda i, j: (i, j)),
        out_shape=jax.ShapeDtypeStruct((m, n), BF16),
        scratch_shapes=[pltpu.VMEM((tm, d), BF16)],
        compiler_params=_params(2),
        name="norm_matmul",
    )(*([x] * ROW_PARTS), g.reshape(1, d), w, col_scale.reshape(1, n))


def _ffn_kernel(*refs, n_slab, final_norm):
    h_parts = refs[:ROW_PARTS]
    g_ref, gf_ref, wg_ref, wu_ref, wd_ref, o_ref, xn_ref = refs[ROW_PARTS:]
    c = pl.program_id(1)

    def add_chunk():
        xn = xn_ref[...]
        gate = jnp.dot(xn, wg_ref[...].astype(BF16), preferred_element_type=F32)
        up = jnp.dot(xn, wu_ref[...].astype(BF16), preferred_element_type=F32)
        act = (gate * jax.nn.sigmoid(gate) * up).astype(BF16)
        slab = o_ref.shape[1] // n_slab
        for n in range(n_slab):
            cols = slice(n * slab, (n + 1) * slab)
            o_ref[:, cols] += jnp.dot(act, wd_ref[:, cols].astype(BF16), preferred_element_type=F32)

    @pl.when(c == 0)
    def _():
        for q, h_ref in enumerate(h_parts):
            h = h_ref[...]
            xn_ref[_row_part(xn_ref, q), :] = _rmsnorm_rows(h, g_ref[...]).astype(BF16)
            o_ref[_row_part(o_ref, q), :] = h
        add_chunk()

    @pl.when(c > 0)
    def _():
        add_chunk()

    if final_norm:
        @pl.when(c == pl.num_programs(1) - 1)
        def _():
            o_ref[...] = _rmsnorm_rows(o_ref[...], gf_ref[...])


def ffn_block(h, g, w_gate_up, w_down, layer, *, tm, tc, final_gain=None):
    m, d = h.shape
    d_ff = w_down.shape[1]
    nc = d_ff // tc
    final_norm = final_gain is not None
    gf = final_gain if final_norm else g
    return pl.pallas_call(
        functools.partial(_ffn_kernel, n_slab=4, final_norm=final_norm),
        grid=(m // tm, nc),
        in_specs=_staggered_row_specs(tm, d, m // tm, nc) + [
            pl.BlockSpec((1, d), lambda i, c: (0, 0)),
            pl.BlockSpec((1, d), lambda i, c: (0, 0)),
            pl.BlockSpec((None, d, tc), lambda i, c: (layer, 0, c)),
            pl.BlockSpec((None, d, tc), lambda i, c: (layer, 0, c + nc)),
            pl.BlockSpec((None, tc, d), lambda i, c: (layer, c, 0)),
        ],
        out_specs=pl.BlockSpec((tm, d), lambda i, c: (i, 0)),
        out_shape=jax.ShapeDtypeStruct((m, d), F32),
        scratch_shapes=[pltpu.VMEM((tm, d), BF16)],
        compiler_params=_params(2),
        name="ffn_block",
    )(*([h] * ROW_PARTS), g.reshape(1, d), gf.reshape(1, d), w_gate_up, w_gate_up, w_down)


def _attn_out_kernel(h_ref, a1_ref, gs_ref, qm_ref, mk_ref, mv_ref, w1_ref, w2_ref, o_ref,
                     ymem_ref, yn_ref, *, tq, subln_scale):
    y_ref = a1_ref if subln_scale is None else yn_ref

    def project():
        acc = jnp.dot(y_ref[...], w1_ref[...].astype(BF16), preferred_element_type=F32)
        acc = acc + jnp.dot(ymem_ref[...], w2_ref[...].astype(BF16), preferred_element_type=F32)
        o_ref[...] = h_ref[...] + acc

    @pl.when(pl.program_id(1) == 0)
    def _():
        if subln_scale is not None:
            gain = gs_ref[...] * subln_scale
            for hh in range(a1_ref.shape[1] // HEAD_DIM):
                cols = _head_cols(hh)
                yn_ref[:, cols] = _rmsnorm_rows(a1_ref[:, cols].astype(F32), gain).astype(BF16)
        for r in range(qm_ref.shape[0] // tq):
            rows = slice(r * tq, (r + 1) * tq)
            for hh in range(N_MEM_HEADS):
                cols = _head_cols(hh)
                s = lax.dot_general(qm_ref[rows, cols], mk_ref[:, cols], NT_DIMS,
                                    preferred_element_type=F32)
                p = jnp.exp2(s - jnp.max(s, axis=-1, keepdims=True))
                mv = mv_ref[:, cols]
                pv = jnp.dot(p.astype(BF16),
                             jnp.concatenate([mv, jnp.ones(mv.shape, mv.dtype)], axis=1),
                             preferred_element_type=F32)
                ymem_ref[rows, cols] = (pv[:, :HEAD_DIM] / pv[:, HEAD_DIM:]).astype(ymem_ref.dtype)
        project()

    @pl.when(pl.program_id(1) > 0)
    def _():
        project()


def attn_out_block(h, y_self, proj, mkv, w, layer, *, seq, n_mem, q_col, tn, tq,
                   subln_gain=None, subln_scale=None):
    m, n = h.shape
    k1 = y_self.shape[1]
    assert k1 % MEM_WIDTH == 0
    q_blk = q_col // MEM_WIDTH
    if subln_gain is None:
        subln_gain = jnp.ones((HEAD_DIM,), F32)
    yn_rows = seq if subln_scale is not None else 8
    return pl.pallas_call(
        functools.partial(_attn_out_kernel, tq=tq, subln_scale=subln_scale),
        grid=(m // seq, n // tn),
        in_specs=[
            pl.BlockSpec((seq, tn), lambda i, j: (i, j)),
            pl.BlockSpec((seq, k1), lambda i, j: (i, 0)),
            pl.BlockSpec((1, HEAD_DIM), lambda i, j: (0, 0)),
            pl.BlockSpec((seq, MEM_WIDTH), lambda i, j: (i, q_blk)),
            pl.BlockSpec((None, n_mem, MEM_WIDTH), lambda i, j: (layer, i, 0)),
            pl.BlockSpec((None, n_mem, MEM_WIDTH), lambda i, j: (layer, i, 1)),
            pl.BlockSpec((None, k1, tn), lambda i, j: (layer, 0, j)),
            pl.BlockSpec((None, MEM_WIDTH, tn), lambda i, j: (layer, k1 // MEM_WIDTH, j)),
        ],
        out_specs=pl.BlockSpec((seq, tn), lambda i, j: (i, j)),
        out_shape=jax.ShapeDtypeStruct((m, n), F32),
        scratch_shapes=[pltpu.VMEM((seq, MEM_WIDTH), BF16), pltpu.VMEM((yn_rows, k1), BF16)],
        compiler_params=_params(2),
        name="attn_out_block",
    )(h, y_self, subln_gain.reshape(1, HEAD_DIM), proj, mkv, mkv, w, w)


def _rel_bucket_np(rel):
    n = np.maximum(rel, 0)
    max_exact = N_BUCKETS // 2
    nf = np.maximum(n, 1).astype(np.float32)
    large = max_exact + (np.log(nf / np.float32(max_exact)) / np.float32(math.log(MAX_DISTANCE / max_exact))
                         * np.float32(N_BUCKETS - max_exact)).astype(np.int32)
    large = np.minimum(large, N_BUCKETS - 1)
    return np.where(n < max_exact, n, large).astype(np.int32)


def _bucket_tiles():
    qi = np.arange(BLK, dtype=np.int32)[:, None]
    ki = np.arange(BLK, dtype=np.int32)[None, :]
    diag = _rel_bucket_np(qi - ki)
    prev = _rel_bucket_np(qi - ki + BLK)
    assert _rel_bucket_np(np.array([BLK + 1]))[0] == N_BUCKETS - 1
    return jnp.asarray(diag), jnp.asarray(prev)


def _bias_tile(bkt, tab_ref, head):
    out = jnp.zeros(bkt.shape, F32)
    for b in range(N_BUCKETS):
        out = jnp.where(bkt == b, tab_ref[head, b] * LOG2E, out)
    return out


def _causal_tile(tile):
    qi = lax.broadcasted_iota(jnp.int32, tile.shape, 0)
    ki = lax.broadcasted_iota(jnp.int32, tile.shape, 1)
    return jnp.where(ki <= qi, tile, NEG)


def _bias_kernel(tab_ref, bd_ref, bp_ref, o_ref):
    h = pl.program_id(0)
    o_ref[...] = jnp.concatenate(
        [_bias_tile(bp_ref[...], tab_ref, h),
         _causal_tile(_bias_tile(bd_ref[...], tab_ref, h))], axis=1)


def bias_tiles(tab, bkt_diag, bkt_prev):
    n_heads = tab.shape[0]
    return pl.pallas_call(
        _bias_kernel,
        grid=(n_heads,),
        in_specs=[
            pl.BlockSpec(memory_space=pltpu.SMEM),
            pl.BlockSpec((BLK, BLK), lambda h: (0, 0)),
            pl.BlockSpec((BLK, BLK), lambda h: (0, 0)),
        ],
        out_specs=pl.BlockSpec((None, BLK, 2 * BLK), lambda h: (h, 0, 0)),
        out_shape=jax.ShapeDtypeStruct((n_heads, BLK, 2 * BLK), F32),
        compiler_params=_params(1),
        name="bias_tiles",
    )(tab, bkt_diag, bkt_prev)


def _wide(x, width):
    return jnp.tile(x, (1, width // HEAD_DIM))


def _softmax_step(s, v, m_ref, acc_ref, *, first, shift=None):
    v_aug = jnp.concatenate([v, jnp.ones(v.shape, v.dtype)], axis=1)
    m_cur = jnp.max(s, axis=-1, keepdims=True)
    if shift is not None:
        m_cur = m_cur + shift
    if first:
        m_new = jnp.broadcast_to(m_cur, m_ref.shape)
    else:
        m_old = m_ref[...]
        m_new = jnp.maximum(m_old, m_cur)
    p = jnp.exp2(s - _wide(m_new if shift is None else m_new - shift, s.shape[1]))
    pv = jnp.dot(p.astype(BF16), v_aug, preferred_element_type=F32)
    if first:
        acc_ref[...] = pv
    else:
        acc_ref[...] = _wide(jnp.exp2(m_old - m_new), acc_ref.shape[1]) * acc_ref[...] + pv
    m_ref[...] = m_new


def _softmax_result(acc_ref):
    acc = acc_ref[...]
    return acc[:, :HEAD_DIM] / acc[:, HEAD_DIM:]


def _head_cols(hh):
    return slice(hh * HEAD_DIM, (hh + 1) * HEAD_DIM)


def _moba_kernel(tab_ref, bias_ref, q_ref, k_ref, v_ref, o_ref,
                 kmean_ref, qa_ref, m_ref, acc_ref, *, gate_fill, heads):
    hg, qb = pl.program_id(0), pl.program_id(2)
    group, seq, _ = k_ref.shape
    nb = seq // BLK
    chains = [(hh, g) for hh in range(heads) for g in range(group)]

    @pl.when(qb == 0)
    def _():
        for c, (hh, g) in enumerate(chains):
            kf = k_ref[g, :, _head_cols(hh)].astype(F32).reshape(nb, BLK, HEAD_DIM)
            km = jnp.mean(kf, axis=1)
            km_hi = km.astype(BF16)
            km_lo = (km - km_hi.astype(F32)).astype(BF16)
            kmean_ref[c] = jnp.concatenate([km_hi, km_lo], axis=0)

    row = lax.broadcasted_iota(jnp.int32, (nb, BLK), 0)
    past = row < qb
    n_sel = max(1, min(MOBA_TOPK, nb - 1))
    for c, (hh, g) in enumerate(chains):
        q = q_ref[g, :, _head_cols(hh)]
        parts = lax.dot_general(kmean_ref[c], q, NT_DIMS, preferred_element_type=F32)
        gate = jnp.where(past, parts[:nb] + parts[nb:], gate_fill)
        beaten_by = jnp.zeros(gate.shape, F32)
        for i in range(nb):
            gi = gate[i:i + 1, :]
            wins = (gi > gate) | ((gi == gate) & (i < row))
            beaten_by = beaten_by + jnp.where(wins, 1.0, 0.0)
        keep = ((beaten_by < n_sel) & past) | (row == qb)
        sel_neg = jnp.where(keep, 0.0, NEG)
        sel_neg = jnp.concatenate([sel_neg, jnp.zeros((HEAD_DIM - nb, BLK), F32)], axis=0)
        qa_ref[c] = jnp.concatenate([q, sel_neg.T.astype(BF16)], axis=1)


    def span_scores(c, j, n):
        hh, g = chains[c]
        rows = pl.ds(pl.multiple_of(j * BLK, BLK), n * BLK)
        kj = k_ref[g, rows, _head_cols(hh)]
        lane = lax.broadcasted_iota(jnp.int32, (n * BLK, HEAD_DIM), 1)
        blk = j + lax.broadcasted_iota(jnp.int32, (n * BLK, HEAD_DIM), 0) // BLK
        onehot = jnp.where(lane == blk, 1.0, 0.0).astype(BF16)
        k_aug = jnp.concatenate([kj, onehot], axis=1)
        s = lax.dot_general(qa_ref[c], k_aug, NT_DIMS, preferred_element_type=F32)
        return s, v_ref[g, rows, _head_cols(hh)]

    @pl.when(qb == 0)
    def _():
        for c, (hh, g) in enumerate(chains):
            s, v = span_scores(c, qb, 1)
            _softmax_step(s + bias_ref[hh, :, BLK:], v, m_ref.at[c], acc_ref.at[c], first=True)

    @pl.when(qb >= 1)
    def _():
        for c, (hh, g) in enumerate(chains):
            s, v = span_scores(c, qb - 1, 2)
            _softmax_step(s + bias_ref[hh], v, m_ref.at[c], acc_ref.at[c], first=True)

    far_bias = [tab_ref[hg * heads + hh, N_BUCKETS - 1] * LOG2E for hh in range(heads)]

    def far_pair(p, carry):
        for c, (hh, g) in enumerate(chains):
            s, v = span_scores(c, 2 * p, 2)
            _softmax_step(s, v, m_ref.at[c], acc_ref.at[c], first=False, shift=far_bias[hh])
        return carry

    lax.fori_loop(0, jnp.maximum(qb - 1, 0) // 2, far_pair, 0)

    @pl.when((qb >= 2) & (qb % 2 == 0))
    def _():
        for c, (hh, g) in enumerate(chains):
            s, v = span_scores(c, qb - 2, 1)
            _softmax_step(s, v, m_ref.at[c], acc_ref.at[c], first=False, shift=far_bias[hh])

    for c, (hh, g) in enumerate(chains):
        o_ref[g, :, _head_cols(hh)] = _softmax_result(acc_ref.at[c]).astype(o_ref.dtype)


def moba_attention(proj, tab, bias, *, group, heads, gate_fill):
    batch, seq, _ = proj.shape
    nqb = seq // BLK
    k_off = N_SELF_HEADS // heads
    width = heads * HEAD_DIM
    chains = heads * group
    kern = functools.partial(_moba_kernel, gate_fill=gate_fill, heads=heads)
    return pl.pallas_call(
        kern,
        grid=(N_SELF_HEADS // heads, batch // group, nqb),
        in_specs=[
            pl.BlockSpec(memory_space=pltpu.SMEM),
            pl.BlockSpec((heads, BLK, 2 * BLK), lambda h, b, qb: (h, 0, 0)),
            pl.BlockSpec((group, BLK, width), lambda h, b, qb: (b, qb, h)),
            pl.BlockSpec((group, seq, width), lambda h, b, qb: (b, 0, k_off + h)),
            pl.BlockSpec((group, seq, width), lambda h, b, qb: (b, 0, 2 * k_off + h)),
        ],
        out_specs=pl.BlockSpec((group, BLK, width), lambda h, b, qb: (b, qb, h)),
        out_shape=jax.ShapeDtypeStruct((batch, seq, SELF_WIDTH), BF16),
        scratch_shapes=[
            pltpu.VMEM((chains, 2 * (seq // BLK), HEAD_DIM), BF16),
            pltpu.VMEM((chains, BLK, 2 * HEAD_DIM), BF16),
            pltpu.VMEM((chains, BLK, HEAD_DIM), F32), pltpu.VMEM((chains, BLK, 2 * HEAD_DIM), F32),
        ],
        compiler_params=_params(3),
        name="moba_attention",
    )(tab, bias, proj, proj, proj)


def _twice(tile):
    return jnp.concatenate([tile, tile], axis=0)


def _diff_kernel(tab_ref, bias_ref, lq_ref, q_ref, k_ref, v_ref, o_ref,
                 q2_ref, m_ref, acc_ref, *, lam_init, heads):
    hg, qb = pl.program_id(0), pl.program_id(2)
    group = k_ref.shape[0]
    chains = [(hh, g) for hh in range(heads) for g in range(group)]

    lane = lax.broadcasted_iota(jnp.int32, (BLK, HEAD_DIM), 1)
    for c, (hh, g) in enumerate(chains):
        q = q_ref[g, :, _head_cols(hh)]
        zero = jnp.zeros_like(q)
        q2_ref[c] = jnp.concatenate([jnp.where(lane < DIFF_HALF, q, zero),
                                     jnp.where(lane >= DIFF_HALF, q, zero)], axis=0)

    def span_scores(c, j, n):
        hh, g = chains[c]
        rows = pl.ds(pl.multiple_of(j * BLK, BLK), n * BLK)
        s = lax.dot_general(q2_ref[c], k_ref[g, rows, _head_cols(hh)], NT_DIMS,
                            preferred_element_type=F32)
        return s, v_ref[g, rows, _head_cols(hh)]

    @pl.when(qb == 0)
    def _():
        for c, (hh, g) in enumerate(chains):
            s, v = span_scores(c, qb, 1)
            _softmax_step(s + _twice(bias_ref[hh, :, BLK:]), v, m_ref.at[c], acc_ref.at[c],
                          first=True)

    @pl.when(qb >= 1)
    def _():
        for c, (hh, g) in enumerate(chains):
            s, v = span_scores(c, qb - 1, 2)
            _softmax_step(s + _twice(bias_ref[hh]), v, m_ref.at[c], acc_ref.at[c], first=True)

    far_bias = [tab_ref[hg * heads + hh, N_BUCKETS - 1] * LOG2E for hh in range(heads)]

    def far_pair(p, carry):
        for c, (hh, g) in enumerate(chains):
            s, v = span_scores(c, 2 * p, 2)
            _softmax_step(s, v, m_ref.at[c], acc_ref.at[c], first=False, shift=far_bias[hh])
        return carry

    lax.fori_loop(0, jnp.maximum(qb - 1, 0) // 2, far_pair, 0)

    @pl.when((qb >= 2) & (qb % 2 == 0))
    def _():
        for c, (hh, g) in enumerate(chains):
            s, v = span_scores(c, qb - 2, 1)
            _softmax_step(s, v, m_ref.at[c], acc_ref.at[c], first=False, shift=far_bias[hh])

    lq = lq_ref[...]
    lam = (jnp.exp(jnp.sum(lq[0:1] * lq[1:2], axis=-1, keepdims=True))
           - jnp.exp(jnp.sum(lq[2:3] * lq[3:4], axis=-1, keepdims=True)) + lam_init)
    for c, (hh, g) in enumerate(chains):
        o_all = _softmax_result(acc_ref.at[c])
        o_ref[g, :, _head_cols(hh)] = (o_all[:BLK] - lam * o_all[BLK:]).astype(o_ref.dtype)


def diff_attention(proj, kv, tab, bias, lq, *, group, heads, lam_init):
    batch, seq, _ = proj.shape
    nqb = seq // BLK
    v_off = N_SELF_HEADS // heads
    width = heads * HEAD_DIM
    chains = heads * group
    kern = functools.partial(_diff_kernel, lam_init=lam_init, heads=heads)
    return pl.pallas_call(
        kern,
        grid=(N_SELF_HEADS // heads, batch // group, nqb),
        in_specs=[
            pl.BlockSpec(memory_space=pltpu.SMEM),
            pl.BlockSpec((heads, BLK, 2 * BLK), lambda h, b, qb: (h, 0, 0)),
            pl.BlockSpec((4, DIFF_HALF), lambda h, b, qb: (0, 0)),
            pl.BlockSpec((group, BLK, width), lambda h, b, qb: (b, qb, h)),
            pl.BlockSpec((group, seq, width), lambda h, b, qb: (b, 0, h)),
            pl.BlockSpec((group, seq, width), lambda h, b, qb: (b, 0, v_off + h)),
        ],
        out_specs=pl.BlockSpec((group, BLK, width), lambda h, b, qb: (b, qb, h)),
        out_shape=jax.ShapeDtypeStruct((batch, seq, SELF_WIDTH), BF16),
        scratch_shapes=[
            pltpu.VMEM((chains, 2 * BLK, HEAD_DIM), BF16),
            pltpu.VMEM((chains, 2 * BLK, HEAD_DIM), F32),
            pltpu.VMEM((chains, 2 * BLK, 2 * HEAD_DIM), F32),
        ],
        compiler_params=_params(3),
        name="diff_attention",
    )(tab, bias, lq, proj, kv, kv)


def _mem_kv_kernel(x_ref, g_ref, w_ref, o_ref, xn_ref):
    def project():
        o_ref[...] = jnp.dot(xn_ref[...], w_ref[...].astype(BF16),
                             preferred_element_type=F32).astype(o_ref.dtype)

    @pl.when(pl.program_id(1) == 0)
    def _():
        xn_ref[...] = _rmsnorm_rows(x_ref[...], g_ref[...]).astype(BF16)
        project()

    @pl.when(pl.program_id(1) > 0)
    def _():
        project()


def mem_kv(mem, g_mem, w_mem_kv, *, tn):
    m, d = mem.shape
    depth, _, n = w_mem_kv.shape
    return pl.pallas_call(
        _mem_kv_kernel,
        grid=(depth, n // tn),
        in_specs=[
            pl.BlockSpec((m, d), lambda l, j: (0, 0)),
            pl.BlockSpec((None, 1, d), lambda l, j: (l, 0, 0)),
            pl.BlockSpec((None, d, tn), lambda l, j: (l, 0, j)),
        ],
        out_specs=pl.BlockSpec((None, m, tn), lambda l, j: (l, 0, j)),
        out_shape=jax.ShapeDtypeStruct((depth, m, n), BF16),
        scratch_shapes=[pltpu.VMEM((m, d), BF16)],
        compiler_params=_params(2),
        name="mem_kv",
    )(mem, g_mem.reshape(depth, 1, d), w_mem_kv)


ATTN_GROUP = 4
ATTN_HEADS = 4


def kernel(x, mem, rel_bias, g_mix, w_in_a, w_in_b, g_mem, w_mem_kv, w_o, g_ffn, w_gate_up, w_down,
           g_kv, w_kv_shared, lambda_qk, g_subln, g_final):
    batch, seq, d = x.shape
    n_mem = mem.shape[1]
    depth = g_mix.shape[0]
    n_a = w_in_a.shape[0]
    tokens = batch * seq
    group = math.gcd(batch, ATTN_GROUP)

    w_kv_shared = w_kv_shared[None]

    tab = rel_bias.T.astype(F32)
    bias = bias_tiles(tab, *_bucket_tiles())

    moba_scale = HEAD_DIM ** -0.5 * LOG2E
    diff_scale = DIFF_HALF ** -0.5 * LOG2E
    mem_scale = HEAD_DIM ** -0.5 * LOG2E
    ones = functools.partial(jnp.ones, dtype=F32)
    cs_a = jnp.concatenate([jnp.full((SELF_WIDTH,), moba_scale, F32), ones((2 * SELF_WIDTH,)),
                            jnp.full((MEM_WIDTH,), mem_scale, F32)])
    cs_b = jnp.concatenate([jnp.full((SELF_WIDTH,), diff_scale, F32),
                            jnp.full((MEM_WIDTH,), mem_scale, F32)])

    h = x.reshape(tokens, d)
    mem2 = mem.reshape(batch * n_mem, d)
    mkv = mem_kv(mem2, g_mem, w_mem_kv, tn=512)
    kv = None
    for l in range(depth):
        if l < n_a:
            proj = norm_matmul(h, g_mix[l], w_in_a, l, cs_a, tm=1024, tn=1024)
            y_self = moba_attention(proj.reshape(batch, seq, -1), tab, bias,
                                    group=group, heads=ATTN_HEADS, gate_fill=NEG * moba_scale)
            q_col = 3 * SELF_WIDTH
            subln = {}
        else:
            j = l - n_a
            if kv is None:
                kv = norm_matmul(h, g_kv, w_kv_shared, 0, ones((2 * SELF_WIDTH,)), tm=1024, tn=1024)
                kv = kv.reshape(batch, seq, -1)
            proj = norm_matmul(h, g_mix[l], w_in_b, j, cs_b, tm=1024, tn=1024)
            lam_init = 0.8 - 0.6 * math.exp(-0.3 * l)
            y_self = diff_attention(proj.reshape(batch, seq, -1), kv, tab, bias,
                                    lambda_qk[j], group=group, heads=ATTN_HEADS, lam_init=lam_init)
            q_col = SELF_WIDTH
            subln = dict(subln_gain=g_subln[j], subln_scale=1.0 - lam_init)
        h = attn_out_block(h, y_self.reshape(tokens, -1), proj, mkv, w_o, l, seq=seq, n_mem=n_mem,
                           q_col=q_col, tn=512, tq=1024, **subln)
        h = ffn_block(h, g_ffn[l], w_gate_up, w_down, l, tm=1024, tc=256,
                      final_gain=g_final if l == depth - 1 else None)
    return h.reshape(batch, seq, d)
```
